```python
import math
import jax, jax.numpy as jnp
from jax import lax
import numpy as np

D_MODEL = 1024
BATCH = 4
SEQ = 4096
DEPTH = 2
DEC_BATCH = 8
DEC_SEQ = 2048
PAST_LEN = 128

GRID_W = 64
N_HEADS = 8
N_KV_HEADS = 2
HEAD_DIM = 64
ATTN_WIDTH = N_HEADS * HEAD_DIM
KV_WIDTH = N_KV_HEADS * HEAD_DIM
Q_BLOCK = 128
ROPE_THETA = 10000.0
SSM_WIDTH = D_MODEL - ATTN_WIDTH
SSM_GROUP = 16
SSM_GROUPS = SSM_WIDTH // SSM_GROUP
SSM_STATE = 64
DT_MIN = 0.001
DT_MAX = 0.1
SGU_WIDTH = D_MODEL
SGU_HEADS = 8
SGU_HEAD_DIM = SGU_WIDTH // SGU_HEADS
SGU_CHUNK = 128
D_FF = 4 * D_MODEL
N_AB = (DEPTH + 1) // 2
N_C = DEPTH // 2
AB_IN = ATTN_WIDTH + 2 * KV_WIDTH + SSM_WIDTH
ALPHA = (2 * DEPTH) ** 0.25
BETA = (8 * DEPTH) ** -0.25
EPS = 1e-6

kernel_name = 'hybrid_bidir_gqa_s5_sgu_encoder'


def _layer_norm(x, g, b):
    xf = x.astype(jnp.float32)
    mu = jnp.mean(xf, axis=-1, keepdims=True)
    var = jnp.mean(jnp.square(xf - mu), axis=-1, keepdims=True)
    y = (xf - mu) * lax.rsqrt(var + EPS)
    return (y * g.astype(jnp.float32) + b.astype(jnp.float32)).astype(x.dtype)


def _rms_norm(x, g):
    xf = x.astype(jnp.float32)
    y = xf * lax.rsqrt(jnp.mean(xf * xf, axis=-1, keepdims=True) + EPS)
    return (y * g.astype(jnp.float32)).astype(x.dtype)


def _axial_rope(seq_len):
    rows = seq_len // GRID_W
    row = jnp.repeat(jnp.arange(rows, dtype=jnp.float32), GRID_W)
    col = jnp.tile(jnp.arange(GRID_W, dtype=jnp.float32), rows)
    axis_dim = HEAD_DIM // 2
    inv_freq = ROPE_THETA ** (-jnp.arange(0, axis_dim, 2, dtype=jnp.float32) / axis_dim)
    ang = jnp.concatenate([row[:, None] * inv_freq, col[:, None] * inv_freq], axis=-1)
    return jnp.cos(ang), jnp.sin(ang)


def _apply_rope(x, cos, sin):
    xf = x.astype(jnp.float32).reshape(x.shape[:-1] + (HEAD_DIM // 2, 2))
    x0 = xf[..., 0]
    x1 = xf[..., 1]
    c = cos[None, :, None, :]
    s = sin[None, :, None, :]
    out = jnp.stack([x0 * c - x1 * s, x0 * s + x1 * c], axis=-1)
    return out.reshape(x.shape).astype(x.dtype)


def _block_attention(q, k, v):
    bsz, seq_len = q.shape[0], q.shape[1]
    n_blk = seq_len // Q_BLOCK
    rep = N_HEADS // N_KV_HEADS
    scale = HEAD_DIM ** -0.5
    qb = q.reshape(bsz, n_blk, Q_BLOCK, N_KV_HEADS, rep, HEAD_DIM).transpose(1, 0, 2, 3, 4, 5)

    def one_block(q_blk):
        s = jnp.einsum('bqgrd,bkgd->bgrqk', q_blk, k, preferred_element_type=jnp.float32) * scale
        p = jax.nn.softmax(s, axis=-1)
        return jnp.einsum('bgrqk,bkgd->bqgrd', p.astype(v.dtype), v)

    o = lax.map(one_block, qb)
    return o.transpose(1, 0, 2, 3, 4, 5).reshape(bsz, seq_len, ATTN_WIDTH)


def _ssm_combine(left, right):
    a_l, b_l = left
    a_r, b_r = right
    return a_l * a_r, a_r * b_l + b_r


def _s5(u, a_re_f, a_im_f, log_step_f, a_re_b, a_im_b, log_step_b,
        b_re, b_im, c_re, c_im, d, w_glu, b_glu):
    bsz, seq_len = u.shape[0], u.shape[1]
    uf = u.astype(jnp.float32).reshape(bsz, seq_len, SSM_GROUPS, SSM_GROUP)
    uc = lax.complex(uf, jnp.zeros_like(uf))
    b_mat = lax.complex(b_re.astype(jnp.float32), b_im.astype(jnp.float32))
    c_mat = lax.complex(c_re.astype(jnp.float32), c_im.astype(jnp.float32))

    def scan_dir(a_re, a_im, log_step, reverse):
        lam = lax.complex(a_re.astype(jnp.float32), a_im.astype(jnp.float32))
        dt = jnp.exp(log_step.astype(jnp.float32))[:, None]
        a_bar = jnp.exp(lam * dt)
        b_bar = ((a_bar - 1.0) / lam)[..., None] * b_mat
        bu = jnp.einsum('gph,blgh->blgp', b_bar, uc)
        _, xs = lax.associative_scan(
            _ssm_combine, (jnp.broadcast_to(a_bar, bu.shape), bu), reverse=reverse, axis=1)
        return xs

    xs = scan_dir(a_re_f, a_im_f, log_step_f, False) + scan_dir(a_re_b, a_im_b, log_step_b, True)
    y = jnp.einsum('ghp,blgp->blgh', c_mat, xs).real + d.astype(jnp.float32).reshape(SSM_GROUPS, SSM_GROUP) * uf
    y = jax.nn.gelu(y.reshape(bsz, seq_len, SSM_WIDTH))
    y = y * jax.nn.sigmoid(y @ w_glu.astype(jnp.float32) + b_glu.astype(jnp.float32))
    return y.astype(u.dtype)


def _mixer_ab(x, cos, sin, w_in, q_norm, k_norm, a_re_f, a_im_f, log_step_f, a_re_b, a_im_b, log_step_b,
              b_re, b_im, c_re, c_im, d, w_glu, b_glu, w_out):
    bsz, seq_len = x.shape[0], x.shape[1]
    h = x @ w_in
    q, k, v, u = jnp.split(h, [ATTN_WIDTH, ATTN_WIDTH + KV_WIDTH, ATTN_WIDTH + 2 * KV_WIDTH], axis=-1)
    q = _apply_rope(_rms_norm(q.reshape(bsz, seq_len, N_HEADS, HEAD_DIM), q_norm), cos, sin)
    k = _apply_rope(_rms_norm(k.reshape(bsz, seq_len, N_KV_HEADS, HEAD_DIM), k_norm), cos, sin)
    v = v.reshape(bsz, seq_len, N_KV_HEADS, HEAD_DIM)
    attn = _block_attention(q, k, v)
    ssm = _s5(u, a_re_f, a_im_f, log_step_f, a_re_b, a_im_b, log_step_b,
              b_re, b_im, c_re, c_im, d, w_glu, b_glu)
    return jnp.concatenate([attn, ssm], axis=-1) @ w_out


def _mixer_c(x, w_in, ln_g, ln_b, w_s, b_s, w_out):
    bsz, seq_len = x.shape[0], x.shape[1]
    h = jax.nn.gelu(x @ w_in)
    u, v = jnp.split(h, 2, axis=-1)
    v = _layer_norm(v, ln_g, ln_b)
    vb = v.reshape(bsz, seq_len // SGU_CHUNK, SGU_CHUNK, SGU_HEADS, SGU_HEAD_DIM)
    mixed = jnp.einsum('gij,bcjgd->bcigd', w_s, vb) + b_s.T[None, None, :, :, None]
    return (u * mixed.reshape(bsz, seq_len, SGU_WIDTH)) @ w_out


def _ffn(x, w1, w2):
    h = jax.nn.relu(x @ w1)
    return (h * h) @ w2


def _trunk(x, p):
    cos, sin = _axial_rope(x.shape[1])
    for layer in range(DEPTH):
        i = layer // 2
        if layer % 2 == 0:
            mix = _mixer_ab(x, cos, sin, p['ab_w_in'][i], p['ab_q_norm'][i], p['ab_k_norm'][i],
                            p['ssm_a_re_f'][i], p['ssm_a_im_f'][i], p['ssm_log_step_f'][i],
                            p['ssm_a_re_b'][i], p['ssm_a_im_b'][i], p['ssm_log_step_b'][i],
                            p['ssm_b_re'][i], p['ssm_b_im'][i], p['ssm_c_re'][i], p['ssm_c_im'][i],
                            p['ssm_d'][i], p['ssm_w_glu'][i], p['ssm_b_glu'][i], p['ab_w_out'][i])
        else:
            mix = _mixer_c(x, p['c_w_in'][i], p['c_ln_g'][i], p['c_ln_b'][i],
                           p['c_w_s'][i], p['c_b_s'][i], p['c_w_out'][i])
        x = _layer_norm(ALPHA * x + mix, p['ln_mix_g'][layer], p['ln_mix_b'][layer])
        x = _layer_norm(ALPHA * x + _ffn(x, p['ff_w1'][layer], p['ff_w2'][layer]),
                        p['ln_ff_g'][layer], p['ln_ff_b'][layer])
    return x


def setup_inputs(seed: int = 0) -> dict:
    key = jax.random.key(seed)
    ks = jax.random.split(key, 32)
    f32 = jnp.float32

    def nrm(k, shape, scale):
        return jax.random.normal(k, shape, f32) * scale

    n_idx = jnp.arange(SSM_STATE, dtype=f32)
    a_re0 = jnp.full((N_AB, SSM_GROUPS, SSM_STATE), -0.5, f32)
    a_im0 = jnp.broadcast_to(math.pi * n_idx, (N_AB, SSM_GROUPS, SSM_STATE))
    log_lo, log_hi = math.log(DT_MIN), math.log(DT_MAX)
    return {
        'x_prompt': jax.random.normal(ks[0], (BATCH, SEQ, D_MODEL), f32),
        'x_sample': jax.random.normal(ks[1], (DEC_BATCH, DEC_SEQ, D_MODEL), f32),
        'ab_w_in': nrm(ks[2], (N_AB, D_MODEL, AB_IN), D_MODEL ** -0.5),
        'ab_q_norm': 1.0 + nrm(ks[3], (N_AB, HEAD_DIM), 0.02),
        'ab_k_norm': 1.0 + nrm(ks[4], (N_AB, HEAD_DIM), 0.02),
        'ssm_a_re_f': a_re0 * (1.0 + nrm(ks[5], a_re0.shape, 0.01)),
        'ssm_a_im_f': a_im0 + nrm(ks[6], a_re0.shape, 0.01),
        'ssm_log_step_f': jax.random.uniform(ks[7], (N_AB, SSM_GROUPS), f32, log_lo, log_hi),
        'ssm_a_re_b': a_re0 * (1.0 + nrm(ks[8], a_re0.shape, 0.01)),
        'ssm_a_im_b': a_im0 + nrm(ks[9], a_re0.shape, 0.01),
        'ssm_log_step_b': jax.random.uniform(ks[10], (N_AB, SSM_GROUPS), f32, log_lo, log_hi),
        'ssm_b_re': nrm(ks[11], (N_AB, SSM_GROUPS, SSM_STATE, SSM_GROUP), (2 * SSM_GROUP) ** -0.5),
        'ssm_b_im': nrm(ks[12], (N_AB, SSM_GROUPS, SSM_STATE, SSM_GROUP), (2 * SSM_GROUP) ** -0.5),
        'ssm_c_re': nrm(ks[13], (N_AB, SSM_GROUPS, SSM_GROUP, SSM_STATE), (2 * SSM_STATE) ** -0.5),
        'ssm_c_im': nrm(ks[14], (N_AB, SSM_GROUPS, SSM_GROUP, SSM_STATE), (2 * SSM_STATE) ** -0.5),
        'ssm_d': nrm(ks[15], (N_AB, SSM_WIDTH), 1.0),
        'ssm_w_glu': nrm(ks[16], (N_AB, SSM_WIDTH, SSM_WIDTH), SSM_WIDTH ** -0.5),
        'ssm_b_glu': nrm(ks[17], (N_AB, SSM_WIDTH), 0.01),
        'ab_w_out': nrm(ks[18], (N_AB, D_MODEL, D_MODEL), BETA * D_MODEL ** -0.5),
        'c_w_in': nrm(ks[19], (N_C, D_MODEL, 2 * SGU_WIDTH), D_MODEL ** -0.5),
        'c_ln_g': 1.0 + nrm(ks[20], (N_C, SGU_WIDTH), 0.02),
        'c_ln_b': nrm(ks[21], (N_C, SGU_WIDTH), 0.01),
        'c_w_s': nrm(ks[22], (N_C, SGU_HEADS, SGU_CHUNK, SGU_CHUNK), SGU_CHUNK ** -0.5),
        'c_b_s': 1.0 + nrm(ks[23], (N_C, SGU_HEADS, SGU_CHUNK), 0.01),
        'c_w_out': nrm(ks[24], (N_C, SGU_WIDTH, D_MODEL), BETA * SGU_WIDTH ** -0.5),
        'ln_mix_g': 1.0 + nrm(ks[25], (DEPTH, D_MODEL), 0.02),
        'ln_mix_b': nrm(ks[26], (DEPTH, D_MODEL), 0.01),
        'ff_w1': nrm(ks[27], (DEPTH, D_MODEL, D_FF), D_MODEL ** -0.5),
        'ff_w2': nrm(ks[28], (DEPTH, D_FF, D_MODEL), BETA * D_FF ** -0.5),
        'ln_ff_g': 1.0 + nrm(ks[29], (DEPTH, D_MODEL), 0.02),
        'ln_ff_b': nrm(ks[30], (DEPTH, D_MODEL), 0.01),
    }


def reference(x_prompt, x_sample, ab_w_in, ab_q_norm, ab_k_norm,
              ssm_a_re_f, ssm_a_im_f, ssm_log_step_f, ssm_a_re_b, ssm_a_im_b, ssm_log_step_b,
              ssm_b_re, ssm_b_im, ssm_c_re, ssm_c_im, ssm_d, ssm_w_glu, ssm_b_glu, ab_w_out,
              c_w_in, c_ln_g, c_ln_b, c_w_s, c_b_s, c_w_out,
              ln_mix_g, ln_mix_b, ff_w1, ff_w2, ln_ff_g, ln_ff_b):
    p = {
        'ab_w_in': ab_w_in, 'ab_q_norm': ab_q_norm, 'ab_k_norm': ab_k_norm,
        'ssm_a_re_f': ssm_a_re_f, 'ssm_a_im_f': ssm_a_im_f, 'ssm_log_step_f': ssm_log_step_f,
        'ssm_a_re_b': ssm_a_re_b, 'ssm_a_im_b': ssm_a_im_b, 'ssm_log_step_b': ssm_log_step_b,
        'ssm_b_re': ssm_b_re, 'ssm_b_im': ssm_b_im, 'ssm_c_re': ssm_c_re, 'ssm_c_im': ssm_c_im,
        'ssm_d': ssm_d, 'ssm_w_glu': ssm_w_glu, 'ssm_b_glu': ssm_b_glu, 'ab_w_out': ab_w_out,
        'c_w_in': c_w_in, 'c_ln_g': c_ln_g, 'c_ln_b': c_ln_b, 'c_w_s': c_w_s, 'c_b_s': c_b_s,
        'c_w_out': c_w_out,
        'ln_mix_g': ln_mix_g, 'ln_mix_b': ln_mix_b, 'ff_w1': ff_w1, 'ff_w2': ff_w2,
        'ln_ff_g': ln_ff_g, 'ln_ff_b': ln_ff_b,
    }
    y_prompt = _trunk(x_prompt, p)
    y_sample = _trunk(x_sample, p)
    return (y_prompt, y_sample)
```

```python
import functools
import math

import numpy as np
import jax
import jax.numpy as jnp
from jax import lax
from jax.experimental import pallas as pl
from jax.experimental.pallas import tpu as pltpu

F32 = jnp.float32
BF16 = jnp.bfloat16

D_MODEL = 1024
GRID_W = 64
N_HEADS = 8
N_KV_HEADS = 2
HEAD_DIM = 64
ATTN_WIDTH = N_HEADS * HEAD_DIM
KV_WIDTH = N_KV_HEADS * HEAD_DIM
QK_WIDTH = ATTN_WIDTH + KV_WIDTH
Q_BLOCK = 128
ROPE_THETA = 10000.0
SSM_WIDTH = D_MODEL - ATTN_WIDTH
SSM_GROUP = 16
SSM_GROUPS = SSM_WIDTH // SSM_GROUP
SSM_STATE = 64
SGU_HEADS = 8
SGU_CHUNK = 128
D_FF = 4 * D_MODEL
DEPTH = 2
AB_IN = ATTN_WIDTH + 2 * KV_WIDTH + SSM_WIDTH
ALPHA = (2 * DEPTH) ** 0.25
EPS = 1e-6

LANES = 128
VMEM_LIMIT_BYTES = 56 * 1024 * 1024

SSM_CHUNK = 64
SSM_ROW = SSM_CHUNK * SSM_GROUP
SSM_LAGS = 2 * SSM_CHUNK
SSM_TAPS = SSM_LAGS * SSM_GROUP
SSM_SDIM = 4 * SSM_STATE

TOKEN_TILE = 512


def _cparams(sem):
    return pltpu.CompilerParams(dimension_semantics=sem, vmem_limit_bytes=VMEM_LIMIT_BYTES)


def _const_spec(shape):
    nd = len(shape)
    return pl.BlockSpec(shape, lambda *_: (0,) * nd, pipeline_mode=pl.Buffered(1))


def _layer_norm(r, g, b):
    mu = jnp.mean(r, axis=-1, keepdims=True)
    d = r - mu
    var = jnp.mean(d * d, axis=-1, keepdims=True)
    return d * lax.rsqrt(var + EPS) * g + b


def _split_bf16(x):
    hi = x.astype(BF16)
    lo = (x - hi.astype(F32)).astype(BF16)
    return hi, lo


def _select_cols(x, e):
    hi, lo = _split_bf16(x)
    return (jnp.dot(hi, e, preferred_element_type=F32) + jnp.dot(lo, e, preferred_element_type=F32))


def _select_rows(e, x):
    hi, lo = _split_bf16(x)
    return (jnp.dot(e, hi, preferred_element_type=F32) + jnp.dot(e, lo, preferred_element_type=F32))


def _inproj_kernel(x_ref, w_ref, gain_ref, cos_ref, sin_ref, hmean_ref, q_ref, k_ref, v_ref, u_ref):
    xb = x_ref[...].astype(BF16)
    h = jnp.dot(xb, w_ref[...], preferred_element_type=F32)
    qk = h[:, :QK_WIDTH]
    ms = jnp.dot((qk * qk).astype(BF16), hmean_ref[...], preferred_element_type=F32)
    qkn = qk * lax.rsqrt(ms + EPS) * gain_ref[...]
    c = cos_ref[...]
    s = sin_ref[...]
    even = (lax.broadcasted_iota(jnp.int32, c.shape, 1) % 2) == 0
    outs = []
    for j in range(QK_WIDTH // LANES):
        xs = qkn[:, j * LANES:(j + 1) * LANES]
        nxt = pltpu.roll(xs, LANES - 1, 1)
        prv = pltpu.roll(xs, 1, 1)
        outs.append(xs * c + jnp.where(even, nxt, prv) * s)
    for j in range(ATTN_WIDTH // LANES):
        q_ref[:, j * LANES:(j + 1) * LANES] = outs[j].astype(BF16)
    k_ref[...] = outs[ATTN_WIDTH // LANES].astype(BF16)
    v_ref[...] = h[:, QK_WIDTH:QK_WIDTH + KV_WIDTH].astype(BF16)
    u_ref[...] = h[:, QK_WIDTH + KV_WIDTH:].astype(BF16)


def _inproj(x2, w_in, gain, cos_t, sin_t, hmean, seq_len):
    n = x2.shape[0]
    tm = TOKEN_TILE
    nper = seq_len // tm
    return pl.pallas_call(
        _inproj_kernel,
        grid=(n // tm,),
        in_specs=[
            pl.BlockSpec((tm, D_MODEL), lambda i: (i, 0)),
            _const_spec((D_MODEL, AB_IN)),
            _const_spec((1, QK_WIDTH)),
            pl.BlockSpec((tm, LANES), lambda i: (i % nper, 0)),
            pl.BlockSpec((tm, LANES), lambda i: (i % nper, 0)),
            _const_spec((QK_WIDTH, QK_WIDTH)),
        ],
        out_specs=[
            pl.BlockSpec((tm, ATTN_WIDTH), lambda i: (i, 0)),
            pl.BlockSpec((tm, KV_WIDTH), lambda i: (i, 0)),
            pl.BlockSpec((tm, KV_WIDTH), lambda i: (i, 0)),
            pl.BlockSpec((tm, SSM_WIDTH), lambda i: (i, 0)),
        ],
        out_shape=[
            jax.ShapeDtypeStruct((n, ATTN_WIDTH), BF16),
            jax.ShapeDtypeStruct((n, KV_WIDTH), BF16),
            jax.ShapeDtypeStruct((n, KV_WIDTH), BF16),
            jax.ShapeDtypeStruct((n, SSM_WIDTH), BF16),
        ],
        compiler_params=_cparams(("parallel",)),
        name="inproj",
    )(x2, w_in, gain, cos_t, sin_t, hmean)


def _attn_kernel(q_ref, k_ref, v_ref, o_ref):
    k = k_ref[0]
    v = v_ref[0]
    q = q_ref[0]
    tq = q.shape[0]
    rep = N_HEADS // N_KV_HEADS
    lane = lax.broadcasted_iota(jnp.int32, (tq, LANES), 1)
    halves = []
    for g in range(N_KV_HEADS):
        in_group = (lane >= g * HEAD_DIM) & (lane < (g + 1) * HEAD_DIM)
        zero = jnp.zeros((tq, LANES), BF16)
        qs = jnp.concatenate(
            [jnp.where(in_group, q[:, m * LANES:(m + 1) * LANES], zero) for m in range(rep)], axis=0)
        s = lax.dot_general(qs, k, (((1,), (1,)), ((), ())), preferred_element_type=F32)
        m_row = jnp.max(s, axis=-1, keepdims=True)
        p = jnp.exp(s - m_row)
        l_row = jnp.sum(p, axis=-1, keepdims=True)
        o = jnp.dot(p.astype(BF16), v, preferred_element_type=F32)
        halves.append(o / l_row)
    lo_half = lane < HEAD_DIM
    for m in range(rep):
        blk = jnp.where(lo_half, halves[0][m * tq:(m + 1) * tq], halves[1][m * tq:(m + 1) * tq])
        o_ref[0, :, m * LANES:(m + 1) * LANES] = blk.astype(BF16)


def _attention(q, k, v):
    b, seq_len, _ = q.shape
    tq = Q_BLOCK
    return pl.pallas_call(
        _attn_kernel,
        grid=(b, seq_len // tq),
        in_specs=[
            pl.BlockSpec((1, tq, ATTN_WIDTH), lambda i, j: (i, j, 0)),
            pl.BlockSpec((1, seq_len, KV_WIDTH), lambda i, j: (i, 0, 0)),
            pl.BlockSpec((1, seq_len, KV_WIDTH), lambda i, j: (i, 0, 0)),
        ],
        out_specs=pl.BlockSpec((1, tq, ATTN_WIDTH), lambda i, j: (i, j, 0)),
        out_shape=jax.ShapeDtypeStruct((b, seq_len, ATTN_WIDTH), BF16),
        compiler_params=_cparams(("parallel", "arbitrary")),
        name="attention",
    )(q, k, v)


def _ssm_table_kernel(pc_ref, pr_ref, bret_ref, bimt_ref, ctr_ref, cti_ref, dt_ref,
                      e16_ref, t16_ref, x16_ref, tt_ref,
                      m_ref, p_ref, q_ref, sca_ref, scb_ref):
    ns = SSM_STATE
    pc = pc_ref[0]
    are_c, aim_c, ls_c = pc[:, 0:1], pc[:, 1:2], pc[:, 2:3]
    dt_c = jnp.exp(ls_c)
    row = lax.broadcasted_iota(jnp.int32, (SSM_SDIM, LANES), 0)
    lan = lax.broadcasted_iota(jnp.int32, (SSM_SDIM, LANES), 1)
    b_row = row >= 2 * ns
    im_row = ((row // ns) % 2) == 1

    def col_tables(n_int, mask):
        nf = jnp.where(mask, n_int, 0).astype(F32)
        rho = jnp.exp(are_c * dt_c * nf)
        ang = aim_c * dt_c * nf
        c = rho * jnp.cos(ang)
        s = rho * jnp.sin(ang)
        a1 = jnp.where(mask, jnp.where(im_row, -s, c), 0.0)
        a2 = jnp.where(mask, jnp.where(im_row, -c, -s), 0.0)
        return a1, a2

    e16 = e16_ref[...]
    t16 = t16_ref[...]
    ctr = _select_cols(ctr_ref[0], t16)
    cti = _select_cols(cti_ref[0], t16)

    n_g = jnp.where(b_row, (SSM_CHUNK - 1) - lan, lan - (SSM_CHUNK - 1))
    a1, a2 = col_tables(n_g, n_g >= 0)
    g_tab = _select_cols(a1, e16) * ctr + _select_cols(a2, e16) * cti

    n_q = jnp.where(b_row, SSM_CHUNK - lan, lan + 1)
    a1, a2 = col_tables(n_q, lan < SSM_CHUNK)
    q_tab = (_select_cols(a1, e16[:, :SSM_ROW]) * ctr[:, :SSM_ROW]
             + _select_cols(a2, e16[:, :SSM_ROW]) * cti[:, :SSM_ROW])
    q_ref[0] = q_tab.astype(BF16)

    pr = pr_ref[0]
    are_r, aim_r, ls_r = pr[0:1, :], pr[1:2, :], pr[2:3, :]
    dt_r = jnp.exp(ls_r)
    rho1 = jnp.exp(are_r * dt_r)
    th1 = aim_r * dt_r
    abr = rho1 * jnp.cos(th1) - 1.0
    abi = rho1 * jnp.sin(th1)
    den = are_r * are_r + aim_r * aim_r
    cr = (abr * are_r + abi * aim_r) / den
    ci = (abi * are_r - abr * aim_r) / den
    bret = bret_ref[0]
    bimt = bimt_ref[0]
    bbr = cr * bret - ci * bimt
    bbi = cr * bimt + ci * bret
    lan_r = lax.broadcasted_iota(jnp.int32, (LANES, SSM_SDIM), 1)
    im_lane = ((lan_r // ns) % 2) == 1
    bcat = jnp.where(im_lane, bbi, bbr)[:SSM_GROUP]

    bh, bl = _split_bf16(bcat)
    gh, gl = _split_bf16(g_tab)
    w = (jnp.dot(bh, gh, preferred_element_type=F32) + jnp.dot(bh, gl, preferred_element_type=F32)
         + jnp.dot(bl, gh, preferred_element_type=F32))
    wrow = lax.broadcasted_iota(jnp.int32, (SSM_GROUP, SSM_TAPS), 0)
    wcol = lax.broadcasted_iota(jnp.int32, (SSM_GROUP, SSM_TAPS), 1)
    on_diag = (wcol // SSM_GROUP == SSM_CHUNK - 1) & (wcol % SSM_GROUP == wrow)
    w = w + jnp.where(on_diag, dt_ref[0], 0.0)

    per_vreg = LANES // SSM_GROUP
    rolled = [w] + [pltpu.roll(w, SSM_TAPS - SSM_GROUP * kk, 1) for kk in range(1, per_vreg)]
    for i in range(SSM_CHUNK):
        r = SSM_CHUNK - 1 - i
        blk = rolled[r % per_vreg][:, (r // per_vreg) * LANES:(r // per_vreg) * LANES + SSM_ROW]
        m_ref[0, i * SSM_GROUP:(i + 1) * SSM_GROUP, :] = blk.astype(BF16)

    i_r = lax.broadcasted_iota(jnp.int32, (SSM_CHUNK, SSM_SDIM), 0)
    l_r = lax.broadcasted_iota(jnp.int32, (SSM_CHUNK, SSM_SDIM), 1)
    n_p = jnp.where(l_r >= 2 * ns, i_r, (SSM_CHUNK - 1) - i_r).astype(F32)
    rho = jnp.exp(are_r * dt_r * n_p)
    ang = aim_r * dt_r * n_p
    c = rho * jnp.cos(ang)
    s = rho * jnp.sin(ang)
    im_l = ((l_r // ns) % 2) == 1
    a1 = jnp.where(im_l, s, c)
    a2 = jnp.where(im_l, c, -s)
    x16 = x16_ref[...]
    tt = tt_ref[...]
    p_tab = (_select_rows(x16, a1) * _select_rows(tt, bbr) + _select_rows(x16, a2) * _select_rows(tt, bbi))
    p_ref[0] = p_tab.astype(BF16)

    k_r = lax.broadcasted_iota(jnp.int32, (8, SSM_SDIM), 0)
    l8 = lax.broadcasted_iota(jnp.int32, (8, SSM_SDIM), 1)
    n_s = (SSM_CHUNK * jnp.left_shift(1, k_r)).astype(F32)
    rho = jnp.exp(are_r * dt_r * n_s)
    ang = aim_r * dt_r * n_s
    sca_ref[0] = rho * jnp.cos(ang)
    scb_ref[0] = rho * jnp.sin(ang) * jnp.where(((l8 // ns) % 2) == 1, 1.0, -1.0)


def _ssm_tables(pc, pr, bret, bimt, ctr, cti, dtile, e16, t16, x16, tt):
    g = SSM_GROUPS
    grp = lambda shape: pl.BlockSpec((1,) + shape, lambda i: (i, 0, 0))
    return pl.pallas_call(
        _ssm_table_kernel,
        grid=(g,),
        in_specs=[
            grp((SSM_SDIM, 8)), grp((8, SSM_SDIM)), grp((LANES, SSM_SDIM)), grp((LANES, SSM_SDIM)),
            grp((SSM_SDIM, LANES)), grp((SSM_SDIM, LANES)), grp((1, SSM_TAPS)),
            _const_spec((LANES, SSM_TAPS)), _const_spec((LANES, SSM_TAPS)),
            _const_spec((SSM_ROW, SSM_CHUNK)), _const_spec((SSM_ROW, LANES)),
        ],
        out_specs=[
            grp((SSM_ROW, SSM_ROW)), grp((SSM_ROW, SSM_SDIM)), grp((SSM_SDIM, SSM_ROW)),
            grp((8, SSM_SDIM)), grp((8, SSM_SDIM)),
        ],
        out_shape=[
            jax.ShapeDtypeStruct((g, SSM_ROW, SSM_ROW), BF16),
            jax.ShapeDtypeStruct((g, SSM_ROW, SSM_SDIM), BF16),
            jax.ShapeDtypeStruct((g, SSM_SDIM, SSM_ROW), BF16),
            jax.ShapeDtypeStruct((g, 8, SSM_SDIM), F32),
            jax.ShapeDtypeStruct((g, 8, SSM_SDIM), F32),
        ],
        compiler_params=_cparams(("parallel",)),
        name="ssm_tables",
    )(pc, pr, bret, bimt, ctr, cti, dtile, e16, t16, x16, tt)


def _ssm_apply_kernel(z_ref, m_ref, p_ref, q_ref, sca_ref, scb_ref, y_ref, *, segments):
    z = z_ref[0]
    y = jnp.dot(z, m_ref[0], preferred_element_type=F32)
    st = jnp.dot(z, p_ref[0], preferred_element_type=F32)
    sca = sca_ref[0]
    scb = scb_ref[0]
    carried = []
    for (r0, rows, nc) in segments:
        c_idx = lax.broadcasted_iota(jnp.int32, (rows, LANES), 0) % nc
        xf = st[r0:r0 + rows, :LANES]
        xb = st[r0:r0 + rows, LANES:]
        for kk in range(int(math.log2(nc))):
            d = 1 << kk
            sh = jnp.where(c_idx >= d, pltpu.roll(xf, d, 0), 0.0)
            xf = xf + sh * sca[kk:kk + 1, :LANES] + pltpu.roll(sh, SSM_STATE, 1) * scb[kk:kk + 1, :LANES]
            sh = jnp.where(c_idx < nc - d, pltpu.roll(xb, rows - d, 0), 0.0)
            xb = xb + sh * sca[kk:kk + 1, LANES:] + pltpu.roll(sh, SSM_STATE, 1) * scb[kk:kk + 1, LANES:]
        prev_f = jnp.where(c_idx >= 1, pltpu.roll(xf, 1, 0), 0.0)
        next_b = jnp.where(c_idx < nc - 1, pltpu.roll(xb, rows - 1, 0), 0.0)
        carried.append(jnp.concatenate([prev_f, next_b], axis=1))
    xin = jnp.concatenate(carried, axis=0).astype(BF16)
    y_ref[0] = y + jnp.dot(xin, q_ref[0], preferred_element_type=F32)


def _ssm_apply(z, m_tab, p_tab, q_tab, sca, scb, segments):
    g, rows, _ = z.shape
    grp = lambda shape: pl.BlockSpec((1,) + shape, lambda i: (i, 0, 0))
    return pl.pallas_call(
        functools.partial(_ssm_apply_kernel, segments=segments),
        grid=(g,),
        in_specs=[grp((rows, SSM_ROW)), grp((SSM_ROW, SSM_ROW)), grp((SSM_ROW, SSM_SDIM)),
                  grp((SSM_SDIM, SSM_ROW)), grp((8, SSM_SDIM)), grp((8, SSM_SDIM))],
        out_specs=grp((rows, SSM_ROW)),
        out_shape=jax.ShapeDtypeStruct((g, rows, SSM_ROW), F32),
        compiler_params=_cparams(("parallel",)),
        name="ssm_apply",
    )(z, m_tab, p_tab, q_tab, sca, scb)


def _mix0_kernel(x_ref, a_ref, y_ref, wg_ref, bg_ref, woa_ref, wos_ref, g_ref, b_ref, o_ref):
    y = jax.nn.gelu(y_ref[...])
    gate = jnp.dot(y.astype(BF16), wg_ref[...], preferred_element_type=F32) + bg_ref[...]
    zs = (y * jax.nn.sigmoid(gate)).astype(BF16)
    mix = (jnp.dot(a_ref[...], woa_ref[...], preferred_element_type=F32)
           + jnp.dot(zs, wos_ref[...], preferred_element_type=F32))
    o_ref[...] = _layer_norm(ALPHA * x_ref[...] + mix, g_ref[...], b_ref[...])


def _mix0(x2, attn, y, w_glu, b_glu, w_out_a, w_out_s, ln_g, ln_b):
    n = x2.shape[0]
    tm = TOKEN_TILE
    tok = lambda w: pl.BlockSpec((tm, w), lambda i: (i, 0))
    return pl.pallas_call(
        _mix0_kernel,
        grid=(n // tm,),
        in_specs=[tok(D_MODEL), tok(ATTN_WIDTH), tok(SSM_WIDTH),
                  _const_spec((SSM_WIDTH, SSM_WIDTH)), _const_spec((1, SSM_WIDTH)),
                  _const_spec((ATTN_WIDTH, D_MODEL)), _const_spec((SSM_WIDTH, D_MODEL)),
                  _const_spec((1, D_MODEL)), _const_spec((1, D_MODEL))],
        out_specs=tok(D_MODEL),
        out_shape=jax.ShapeDtypeStruct((n, D_MODEL), F32),
        compiler_params=_cparams(("parallel",)),
        name="mix0",
    )(x2, attn, y, w_glu, b_glu, w_out_a, w_out_s, ln_g, ln_b)


FF_CHUNK = 1024


def _ffn_kernel(x_ref, w1_ref, w2_ref, g_ref, b_ref, o_ref):
    x = x_ref[...]
    xb = x.astype(BF16)
    acc = ALPHA * x
    for c in range(D_FF // FF_CHUNK):
        h = jnp.dot(xb, w1_ref[:, c * FF_CHUNK:(c + 1) * FF_CHUNK], preferred_element_type=F32)
        h = jnp.maximum(h, 0.0)
        acc = acc + jnp.dot((h * h).astype(BF16), w2_ref[c * FF_CHUNK:(c + 1) * FF_CHUNK, :],
                            preferred_element_type=F32)
    o_ref[...] = _layer_norm(acc, g_ref[...], b_ref[...])


def _ffn(x2, w1, w2, ln_g, ln_b):
    n = x2.shape[0]
    tm = TOKEN_TILE
    tok = pl.BlockSpec((tm, D_MODEL), lambda i: (i, 0))
    return pl.pallas_call(
        _ffn_kernel,
        grid=(n // tm,),
        in_specs=[tok, _const_spec((D_MODEL, D_FF)), _const_spec((D_FF, D_MODEL)),
                  _const_spec((1, D_MODEL)), _const_spec((1, D_MODEL))],
        out_specs=tok,
        out_shape=jax.ShapeDtypeStruct((n, D_MODEL), F32),
        compiler_params=_cparams(("parallel",)),
        name="ffn",
    )(x2, w1, w2, ln_g, ln_b)


def _sgu_kernel(x_ref, win_ref, lg_ref, lb_ref, ws_ref, bs_ref, wout_ref, g_ref, b_ref, o_ref):
    x = x_ref[...]
    tm = x.shape[0]
    h = jax.nn.gelu(jnp.dot(x.astype(BF16), win_ref[...], preferred_element_type=F32))
    u = h[:, :D_MODEL]
    vb = _layer_norm(h[:, D_MODEL:], lg_ref[...], lb_ref[...]).astype(BF16)
    rows = []
    for c in range(tm // SGU_CHUNK):
        cols = []
        for hd in range(SGU_HEADS):
            blk = vb[c * SGU_CHUNK:(c + 1) * SGU_CHUNK, hd * LANES:(hd + 1) * LANES]
            cols.append(jnp.dot(ws_ref[hd], blk, preferred_element_type=F32) + bs_ref[hd])
        rows.append(jnp.concatenate(cols, axis=1))
    mixed = jnp.concatenate(rows, axis=0)
    out = jnp.dot((u * mixed).astype(BF16), wout_ref[...], preferred_element_type=F32)
    o_ref[...] = _layer_norm(ALPHA * x + out, g_ref[...], b_ref[...])


def _sgu(x2, w_in, ln_g, ln_b, w_s, b_s_full, w_out, mg, mb):
    n = x2.shape[0]
    tm = TOKEN_TILE
    tok = pl.BlockSpec((tm, D_MODEL), lambda i: (i, 0))
    return pl.pallas_call(
        _sgu_kernel,
        grid=(n // tm,),
        in_specs=[tok, _const_spec((D_MODEL, 2 * D_MODEL)), _const_spec((1, D_MODEL)), _const_spec((1, D_MODEL)),
                  _const_spec((SGU_HEADS, SGU_CHUNK, SGU_CHUNK)), _const_spec((SGU_HEADS, SGU_CHUNK, LANES)),
                  _const_spec((D_MODEL, D_MODEL)), _const_spec((1, D_MODEL)), _const_spec((1, D_MODEL))],
        out_specs=tok,
        out_shape=jax.ShapeDtypeStruct((n, D_MODEL), F32),
        compiler_params=_cparams(("parallel",)),
        name="sgu",
    )(x2, w_in, ln_g, ln_b, w_s, b_s_full, w_out, mg, mb)


_HEAD_OF_SLOT = np.array([0, 4, 1, 5, 2, 6, 3, 7])


def _rope_tables(seq_len):
    t = np.arange(seq_len)
    row = (t // GRID_W).astype(np.float32)
    col = (t % GRID_W).astype(np.float32)
    axis_dim = HEAD_DIM // 2
    inv_freq = ROPE_THETA ** (-jnp.arange(0, axis_dim, 2, dtype=F32) / axis_dim)
    ang = jnp.concatenate([jnp.asarray(row)[:, None] * inv_freq, jnp.asarray(col)[:, None] * inv_freq], axis=-1)
    cos = jnp.repeat(jnp.cos(ang), 2, axis=-1)
    sin = jnp.repeat(jnp.sin(ang), 2, axis=-1)
    sign = jnp.tile(jnp.asarray([-1.0, 1.0], F32), HEAD_DIM // 2)
    reps = LANES // HEAD_DIM
    return jnp.tile(cos, (1, reps)), jnp.tile(sin * sign, (1, reps))


def _selectors():
    col = np.arange(SSM_TAPS)
    row = np.arange(LANES)
    e16 = (col[None, :] // SSM_GROUP == row[:, None])
    t16 = (col[None, :] % SSM_GROUP == row[:, None])
    r2 = np.arange(SSM_ROW)
    x16 = (r2[:, None] // SSM_GROUP == np.arange(SSM_CHUNK)[None, :])
    tt = (r2[:, None] % SSM_GROUP == np.arange(LANES)[None, :])
    as_bf16 = lambda a: jnp.asarray(a.astype(np.float32), BF16)
    return as_bf16(e16), as_bf16(t16), as_bf16(x16), as_bf16(tt)


def _ssm_param_layout(a_re_f, a_im_f, ls_f, a_re_b, a_im_b, ls_b, b_re, b_im, c_re, c_im, d):
    g, ns = SSM_GROUPS, SSM_STATE
    lsf = jnp.broadcast_to(ls_f[:, None], (g, ns))
    lsb = jnp.broadcast_to(ls_b[:, None], (g, ns))
    zeros = jnp.zeros((g, ns), F32)
    per_f = jnp.stack([a_re_f, a_im_f, lsf] + [zeros] * 5, axis=1)
    per_b = jnp.stack([a_re_b, a_im_b, lsb] + [zeros] * 5, axis=1)
    pr = jnp.concatenate([per_f, per_f, per_b, per_b], axis=2)
    pc = jnp.transpose(pr, (0, 2, 1))
    def rows_t(b):
        bt = jnp.transpose(b, (0, 2, 1))
        bt = jnp.concatenate([bt, bt, bt, bt], axis=2)
        return jnp.pad(bt, ((0, 0), (0, LANES - SSM_GROUP), (0, 0)))
    def cols_t(c):
        ct = jnp.transpose(c, (0, 2, 1))
        ct = jnp.concatenate([ct, ct, ct, ct], axis=1)
        return jnp.pad(ct, ((0, 0), (0, 0), (0, LANES - SSM_GROUP)))
    dtile = jnp.tile(d.reshape(g, 1, SSM_GROUP), (1, 1, SSM_LAGS))
    return pc, pr, rows_t(b_re), rows_t(b_im), cols_t(c_re), cols_t(c_im), dtile


def kernel(x_prompt, x_sample, ab_w_in, ab_q_norm, ab_k_norm, ssm_a_re_f, ssm_a_im_f, ssm_log_step_f,
           ssm_a_re_b, ssm_a_im_b, ssm_log_step_b, ssm_b_re, ssm_b_im, ssm_c_re, ssm_c_im, ssm_d, ssm_w_glu,
           ssm_b_glu, ab_w_out, c_w_in, c_ln_g, c_ln_b, c_w_s, c_b_s, c_w_out, ln_mix_g, ln_mix_b, ff_w1, ff_w2,
           ln_ff_g, ln_ff_b):
    xs = [x_prompt, x_sample]
    row2 = lambda v: v.reshape(1, -1)

    slot_cols = (_HEAD_OF_SLOT[:, None] * HEAD_DIM + np.arange(HEAD_DIM)[None, :]).reshape(-1)
    w_in = ab_w_in[0]
    w_in = jnp.concatenate([w_in[:, slot_cols], w_in[:, ATTN_WIDTH:]], axis=1).astype(BF16)
    scale = HEAD_DIM ** -0.5
    gain = jnp.concatenate([jnp.tile(ab_q_norm[0] * scale, N_HEADS), jnp.tile(ab_k_norm[0], N_KV_HEADS)])
    hid = np.arange(QK_WIDTH) // HEAD_DIM
    hmean = jnp.asarray((hid[:, None] == hid[None, :]).astype(np.float32) / HEAD_DIM, BF16)
    w_out = ab_w_out[0]
    w_out_a = w_out[:ATTN_WIDTH][slot_cols].astype(BF16)
    w_out_s = w_out[ATTN_WIDTH:].astype(BF16)
    w_glu = ssm_w_glu[0].astype(BF16)
    e16, t16, x16, tt = _selectors()
    tabs = _ssm_tables(*_ssm_param_layout(
        ssm_a_re_f[0], ssm_a_im_f[0], ssm_log_step_f[0], ssm_a_re_b[0], ssm_a_im_b[0], ssm_log_step_b[0],
        ssm_b_re[0], ssm_b_im[0], ssm_c_re[0], ssm_c_im[0], ssm_d[0]), e16, t16, x16, tt)

    attn, zs, segments = [], [], []
    r0 = 0
    for x in xs:
        b, seq_len, _ = x.shape
        cos_t, sin_t = _rope_tables(seq_len)
        q, k, v, u = _inproj(x.reshape(b * seq_len, D_MODEL), w_in, row2(gain), cos_t, sin_t, hmean, seq_len)
        attn.append(_attention(q.reshape(b, seq_len, ATTN_WIDTH), k.reshape(b, seq_len, KV_WIDTH),
                               v.reshape(b, seq_len, KV_WIDTH)).reshape(b * seq_len, ATTN_WIDTH))
        nc = seq_len // SSM_CHUNK
        z = u.reshape(b * nc, SSM_CHUNK, SSM_GROUPS, SSM_GROUP)
        zs.append(jnp.transpose(z, (2, 0, 1, 3)).reshape(SSM_GROUPS, b * nc, SSM_ROW))
        segments.append((r0, b * nc, nc))
        r0 += b * nc
    y_all = _ssm_apply(jnp.concatenate(zs, axis=1), *tabs, tuple(segments))

    outs = []
    for x, a, (r0, rows, nc) in zip(xs, attn, segments):
        b, seq_len, _ = x.shape
        n = b * seq_len
        y = y_all[:, r0:r0 + rows].reshape(SSM_GROUPS, rows, SSM_CHUNK, SSM_GROUP)
        y = jnp.transpose(y, (1, 2, 0, 3)).reshape(n, SSM_WIDTH)
        h = _mix0(x.reshape(n, D_MODEL), a, y, w_glu, row2(ssm_b_glu[0]), w_out_a, w_out_s,
                  row2(ln_mix_g[0]), row2(ln_mix_b[0]))
        h = _ffn(h, ff_w1[0].astype(BF16), ff_w2[0].astype(BF16), row2(ln_ff_g[0]), row2(ln_ff_b[0]))
        bs_full = jnp.broadcast_to(c_b_s[0][:, :, None], (SGU_HEADS, SGU_CHUNK, LANES))
        h = _sgu(h, c_w_in[0].astype(BF16), row2(c_ln_g[0]), row2(c_ln_b[0]), c_w_s[0].astype(BF16), bs_full,
                 c_w_out[0].astype(BF16), row2(ln_mix_g[1]), row2(ln_mix_b[1]))
        h = _ffn(h, ff_w1[1].astype(BF16), ff_w2[1].astype(BF16), row2(ln_ff_g[1]), row2(ln_ff_b[1]))
        outs.append(h.reshape(b, seq_len, D_MODEL))
    return tuple(outs)
```

```python
import functools
import math

import numpy as np
import jax
import jax.numpy as jnp
from jax import lax
from jax.experimental import pallas as pl
from jax.experimental.pallas import tpu as pltpu

F32 = jnp.float32
BF16 = jnp.bfloat16

D_MODEL = 1024
GRID_W = 64
N_HEADS = 8
N_KV_HEADS = 2
HEAD_DIM = 64
ATTN_WIDTH = N_HEADS * HEAD_DIM
KV_WIDTH = N_KV_HEADS * HEAD_DIM
QK_WIDTH = ATTN_WIDTH + KV_WIDTH
Q_BLOCK = 128
ROPE_THETA = 10000.0
SSM_WIDTH = D_MODEL - ATTN_WIDTH
SSM_GROUP = 16
SSM_GROUPS = SSM_WIDTH // SSM_GROUP
SSM_STATE = 64
SGU_HEADS = 8
SGU_CHUNK = 128
D_FF = 4 * D_MODEL
DEPTH = 2
AB_IN = ATTN_WIDTH + 2 * KV_WIDTH + SSM_WIDTH
ALPHA = (2 * DEPTH) ** 0.25
EPS = 1e-6

LANES = 128
VMEM_LIMIT_BYTES = 56 * 1024 * 1024

SSM_CHUNK = 64
SSM_ROW = SSM_CHUNK * SSM_GROUP
SSM_LAGS = 2 * SSM_CHUNK
SSM_TAPS = SSM_LAGS * SSM_GROUP
SSM_SDIM = 4 * SSM_STATE

TOKEN_TILE = 512


def _cparams(sem):
    return pltpu.CompilerParams(dimension_semantics=sem, vmem_limit_bytes=VMEM_LIMIT_BYTES)


def _const_spec(shape):
    nd = len(shape)
    return pl.BlockSpec(shape, lambda *_: (0,) * nd, pipeline_mode=pl.Buffered(1))


def _layer_norm(r, g, b):
    mu = jnp.mean(r, axis=-1, keepdims=True)
    d = r - mu
    var = jnp.mean(d * d, axis=-1, keepdims=True)
    return d * lax.rsqrt(var + EPS) * g + b


def _split_bf16(x):
    hi = x.astype(BF16)
    lo = (x - hi.astype(F32)).astype(BF16)
    return hi, lo


def _select_cols(x, e):
    hi, lo = _split_bf16(x)
    y = jnp.dot(jnp.concatenate([hi, lo], axis=0), e, preferred_element_type=F32)
    return y[:x.shape[0]] + y[x.shape[0]:]


def _select_rows(e, x):
    hi, lo = _split_bf16(x)
    y = jnp.dot(e, jnp.concatenate([hi, lo], axis=1), preferred_element_type=F32)
    return y[:, :x.shape[1]] + y[:, x.shape[1]:]


def _inproj_kernel(x_ref, w_ref, gain_ref, cos_ref, sin_ref, hmean_ref, q_ref, k_ref, v_ref, u_ref):
    xb = x_ref[...].astype(BF16)
    h = jnp.dot(xb, w_ref[...], preferred_element_type=F32)
    qk = h[:, :QK_WIDTH]
    ms = jnp.dot((qk * qk).astype(BF16), hmean_ref[...], preferred_element_type=F32)
    qkn = qk * lax.rsqrt(ms + EPS) * gain_ref[...]
    c = cos_ref[...]
    s = sin_ref[...]
    even = (lax.broadcasted_iota(jnp.int32, c.shape, 1) % 2) == 0
    outs = []
    for j in range(QK_WIDTH // LANES):
        xs = qkn[:, j * LANES:(j + 1) * LANES]
        nxt = pltpu.roll(xs, LANES - 1, 1)
        prv = pltpu.roll(xs, 1, 1)
        outs.append(xs * c + jnp.where(even, nxt, prv) * s)
    for j in range(ATTN_WIDTH // LANES):
        q_ref[:, j * LANES:(j + 1) * LANES] = outs[j].astype(BF16)
    k_ref[...] = outs[ATTN_WIDTH // LANES].astype(BF16)
    v_ref[...] = h[:, QK_WIDTH:QK_WIDTH + KV_WIDTH].astype(BF16)
    u_ref[...] = h[:, QK_WIDTH + KV_WIDTH:].astype(BF16)


def _inproj(x2, w_in, gain, cos_t, sin_t, hmean, seq_len):
    n = x2.shape[0]
    tm = TOKEN_TILE
    nper = seq_len // tm
    return pl.pallas_call(
        _inproj_kernel,
        grid=(n // tm,),
        in_specs=[
            pl.BlockSpec((tm, D_MODEL), lambda i: (i, 0)),
            _const_spec((D_MODEL, AB_IN)),
            _const_spec((1, QK_WIDTH)),
            pl.BlockSpec((tm, LANES), lambda i: (i % nper, 0)),
            pl.BlockSpec((tm, LANES), lambda i: (i % nper, 0)),
            _const_spec((QK_WIDTH, QK_WIDTH)),
        ],
        out_specs=[
            pl.BlockSpec((tm, ATTN_WIDTH), lambda i: (i, 0)),
            pl.BlockSpec((tm, KV_WIDTH), lambda i: (i, 0)),
            pl.BlockSpec((tm, KV_WIDTH), lambda i: (i, 0)),
            pl.BlockSpec((tm, SSM_WIDTH), lambda i: (i, 0)),
        ],
        out_shape=[
            jax.ShapeDtypeStruct((n, ATTN_WIDTH), BF16),
            jax.ShapeDtypeStruct((n, KV_WIDTH), BF16),
            jax.ShapeDtypeStruct((n, KV_WIDTH), BF16),
            jax.ShapeDtypeStruct((n, SSM_WIDTH), BF16),
        ],
        compiler_params=_cparams(("parallel",)),
        name="inproj",
    )(x2, w_in, gain, cos_t, sin_t, hmean)


ATT_UNIT_Q = 64


def _attn_kernel(q_ref, k_ref, v_ref, o_ref, s0_ref, s1_ref, p0_ref, p1_ref, m0_ref, m1_ref, l0_ref, l1_ref):
    seq_len = k_ref.shape[1]
    n_units = seq_len // ATT_UNIT_Q
    rep = N_HEADS // N_KV_HEADS
    lane = lax.broadcasted_iota(jnp.int32, (ATT_UNIT_Q, LANES), 1)
    lo_half = lane < HEAD_DIM
    zero = jnp.zeros((ATT_UNIT_Q, LANES), BF16)

    def rows_of(t):
        return pl.ds(pl.multiple_of(t * ATT_UNIT_Q, ATT_UNIT_Q), ATT_UNIT_Q)

    def scores(t, s_ref, m_ref):
        q = q_ref[0, rows_of(t), :]
        slots = [q[:, m * LANES:(m + 1) * LANES] for m in range(rep)]
        qs = jnp.concatenate([jnp.where(lo_half, x, zero) for x in slots]
                             + [jnp.where(lo_half, zero, x) for x in slots], axis=0)
        s = lax.dot_general(qs, k_ref[0], (((1,), (1,)), ((), ())), preferred_element_type=F32)
        s_ref[...] = s
        m_ref[...] = jnp.max(s, axis=-1, keepdims=True)

    def probs(s_ref, m_ref, p_ref, l_ref):
        p = jnp.exp2(s_ref[...] - m_ref[...])
        l_ref[...] = jnp.sum(p, axis=-1, keepdims=True)
        p_ref[...] = p.astype(BF16)

    def output(t, p_ref, l_ref):
        o = jnp.dot(p_ref[...], v_ref[0], preferred_element_type=F32) / l_ref[...]
        for m in range(rep):
            g0 = o[m * ATT_UNIT_Q:(m + 1) * ATT_UNIT_Q]
            g1 = o[(rep + m) * ATT_UNIT_Q:(rep + m + 1) * ATT_UNIT_Q]
            o_ref[0, rows_of(t), m * LANES:(m + 1) * LANES] = jnp.where(lo_half, g0, g1).astype(BF16)

    scores(0, s0_ref, m0_ref)
    scores(1, s1_ref, m1_ref)
    probs(s0_ref, m0_ref, p0_ref, l0_ref)

    def body(i, carry):
        t = 2 * i + 2
        scores(t, s0_ref, m0_ref)
        probs(s1_ref, m1_ref, p1_ref, l1_ref)
        output(t - 2, p0_ref, l0_ref)
        scores(t + 1, s1_ref, m1_ref)
        probs(s0_ref, m0_ref, p0_ref, l0_ref)
        output(t - 1, p1_ref, l1_ref)
        return carry

    lax.fori_loop(0, (n_units - 2) // 2, body, 0)
    probs(s1_ref, m1_ref, p1_ref, l1_ref)
    output(n_units - 2, p0_ref, l0_ref)
    output(n_units - 1, p1_ref, l1_ref)


def _attention(q, k, v):
    b, seq_len, _ = q.shape
    rows = ATT_UNIT_Q * N_HEADS
    seq = lambda w, **kw: pl.BlockSpec((1, seq_len, w), lambda i: (i, 0, 0), **kw)
    once = dict(pipeline_mode=pl.Buffered(1))
    return pl.pallas_call(
        _attn_kernel,
        grid=(b,),
        in_specs=[seq(ATTN_WIDTH, **once), seq(KV_WIDTH, **once), seq(KV_WIDTH, **once)],
        out_specs=seq(ATTN_WIDTH),
        out_shape=jax.ShapeDtypeStruct((b, seq_len, ATTN_WIDTH), BF16),
        scratch_shapes=[pltpu.VMEM((rows, seq_len), F32), pltpu.VMEM((rows, seq_len), F32),
                        pltpu.VMEM((rows, seq_len), BF16), pltpu.VMEM((rows, seq_len), BF16),
                        pltpu.VMEM((rows, 1), F32), pltpu.VMEM((rows, 1), F32),
                        pltpu.VMEM((rows, 1), F32), pltpu.VMEM((rows, 1), F32)],
        compiler_params=_cparams(("parallel",)),
        name="attention",
    )(q, k, v)


def _ssm_table_kernel(pc_ref, pr_ref, pr2_ref, bret_ref, bimt_ref, ctr_ref, cti_ref, dt_ref, e16_ref, x16_ref,
                      m_ref, p_ref, q_ref, sca_ref, scb_ref):
    ns = SSM_STATE
    pc = pc_ref[0]
    are_c, aim_c, dt_c = pc[:, 0:1], pc[:, 1:2], jnp.exp(pc[:, 2:3])
    row = lax.broadcasted_iota(jnp.int32, (2 * ns, LANES), 0)
    lan = lax.broadcasted_iota(jnp.int32, (2 * ns, LANES), 1)
    n_g = jnp.where(row >= ns, (SSM_CHUNK - 1) - lan, lan - (SSM_CHUNK - 1))
    mask = n_g >= 0
    nf = jnp.where(mask, n_g, 0).astype(F32)
    rho = jnp.exp(are_c * dt_c * nf)
    ang = aim_c * dt_c * nf
    cs = jnp.where(mask, rho * jnp.cos(ang), 0.0)
    sn = jnp.where(mask, rho * jnp.sin(ang), 0.0)
    base = jnp.concatenate([cs[:ns], sn[:ns], cs[ns:], sn[ns:]], axis=0)
    ex = _select_cols(base, e16_ref[...])
    ecf, esf, ecb, esb = ex[:ns], ex[ns:2 * ns], ex[2 * ns:3 * ns], ex[3 * ns:]

    def tile_lanes(x, width):
        for sh in (SSM_GROUP, 2 * SSM_GROUP, 4 * SSM_GROUP):
            x = x + pltpu.roll(x, sh, 1)
        return jnp.concatenate([x] * (width // LANES), axis=1)

    crt = tile_lanes(ctr_ref[0], SSM_TAPS)
    cit = tile_lanes(cti_ref[0], SSM_TAPS)
    g_fre = ecf * crt - esf * cit
    g_fim = -(esf * crt + ecf * cit)
    g_bre = ecb * crt - esb * cit
    g_bim = -(esb * crt + ecb * cit)
    g_tab = jnp.concatenate([g_fre, g_fim, g_bre, g_bim], axis=0)

    ab_r = cs[ns:, SSM_CHUNK - 2:SSM_CHUNK - 1]
    ab_i = sn[ns:, SSM_CHUNK - 2:SSM_CHUNK - 1]
    q_tab = jnp.concatenate([
        g_fre[:, SSM_ROW:], g_fim[:, SSM_ROW:],
        ab_r * g_bre[:, :SSM_ROW] + ab_i * g_bim[:, :SSM_ROW],
        ab_r * g_bim[:, :SSM_ROW] - ab_i * g_bre[:, :SSM_ROW]], axis=0)
    q_ref[0] = q_tab.astype(BF16)

    pr = pr_ref[0]
    are_r, aim_r, ls_r = pr[0:1, :], pr[1:2, :], pr[2:3, :]
    dt_r = jnp.exp(ls_r)
    rho1 = jnp.exp(are_r * dt_r)
    th1 = aim_r * dt_r
    abr = rho1 * jnp.cos(th1) - 1.0
    abi = rho1 * jnp.sin(th1)
    den = are_r * are_r + aim_r * aim_r
    cr = (abr * are_r + abi * aim_r) / den
    ci = (abi * are_r - abr * aim_r) / den
    bret = bret_ref[0]
    bimt = bimt_ref[0]
    bbr = cr * bret - ci * bimt
    bbi = cr * bimt + ci * bret
    lan_r = lax.broadcasted_iota(jnp.int32, (SSM_GROUP, SSM_SDIM), 1)
    im_lane = ((lan_r // ns) % 2) == 1
    bcat = jnp.where(im_lane, bbi, bbr)

    bh, bl = _split_bf16(bcat)
    gh, gl = _split_bf16(g_tab)
    w = (jnp.dot(bh, gh, preferred_element_type=F32) + jnp.dot(bh, gl, preferred_element_type=F32)
         + jnp.dot(bl, gh, preferred_element_type=F32))
    wrow = lax.broadcasted_iota(jnp.int32, (SSM_GROUP, SSM_TAPS), 0)
    wcol = lax.broadcasted_iota(jnp.int32, (SSM_GROUP, SSM_TAPS), 1)
    on_diag = (wcol // SSM_GROUP == SSM_CHUNK - 1) & (wcol % SSM_GROUP == wrow)
    w = w + jnp.where(on_diag, dt_ref[0], 0.0)

    per_vreg = LANES // SSM_GROUP
    rolled = [w] + [pltpu.roll(w, SSM_TAPS - SSM_GROUP * kk, 1) for kk in range(1, per_vreg)]
    for i in range(SSM_CHUNK):
        r = SSM_CHUNK - 1 - i
        blk = rolled[r % per_vreg][:, (r // per_vreg) * LANES:(r // per_vreg) * LANES + SSM_ROW]
        m_ref[0, i * SSM_GROUP:(i + 1) * SSM_GROUP, :] = blk.astype(BF16)

    pr2 = pr2_ref[0]
    i_r = lax.broadcasted_iota(jnp.int32, (SSM_CHUNK, LANES), 0)
    l_r = lax.broadcasted_iota(jnp.int32, (SSM_CHUNK, LANES), 1)
    lo_r = l_r < ns
    n_p = jnp.where(lo_r, (SSM_CHUNK - 1) - i_r, i_r).astype(F32)
    dt2 = jnp.exp(pr2[2:3, :])
    rho = jnp.exp(pr2[0:1, :] * dt2 * n_p)
    ang = pr2[1:2, :] * dt2 * n_p
    c = rho * jnp.cos(ang)
    s = rho * jnp.sin(ang)
    base_p = jnp.concatenate([jnp.where(lo_r, c, pltpu.roll(s, ns, 1)),
                              jnp.where(lo_r, pltpu.roll(c, ns, 1), s)], axis=1)
    px = _select_rows(x16_ref[...], base_p)
    sign = jnp.where(lax.broadcasted_iota(jnp.int32, (SSM_ROW, LANES), 1) < ns, -1.0, 1.0)
    px2 = jnp.concatenate([pltpu.roll(px[:, :LANES], ns, 1) * sign,
                           pltpu.roll(px[:, LANES:], ns, 1) * sign], axis=1)
    bt_r = jnp.concatenate([bbr] * SSM_CHUNK, axis=0)
    bt_i = jnp.concatenate([bbi] * SSM_CHUNK, axis=0)
    p_ref[0] = (px * bt_r + px2 * bt_i).astype(BF16)

    k_r = lax.broadcasted_iota(jnp.int32, (8, SSM_SDIM), 0)
    l8 = lax.broadcasted_iota(jnp.int32, (8, SSM_SDIM), 1)
    n_s = (SSM_CHUNK * jnp.left_shift(1, k_r)).astype(F32)
    rho = jnp.exp(are_r * dt_r * n_s)
    ang = aim_r * dt_r * n_s
    sca_ref[0] = rho * jnp.cos(ang)
    scb_ref[0] = rho * jnp.sin(ang) * jnp.where(((l8 // ns) % 2) == 1, 1.0, -1.0)


def _ssm_tables(pc, pr, pr2, bret, bimt, ctr, cti, dtile, e16, x16):
    g = SSM_GROUPS
    grp = lambda shape: pl.BlockSpec((1,) + shape, lambda i: (i, 0, 0))
    return pl.pallas_call(
        _ssm_table_kernel,
        grid=(g,),
        in_specs=[
            grp((2 * SSM_STATE, 8)), grp((8, SSM_SDIM)), grp((8, LANES)),
            grp((SSM_GROUP, SSM_SDIM)), grp((SSM_GROUP, SSM_SDIM)),
            grp((SSM_STATE, LANES)), grp((SSM_STATE, LANES)), grp((1, SSM_TAPS)),
            _const_spec((LANES, SSM_TAPS)), _const_spec((SSM_ROW, SSM_CHUNK)),
        ],
        out_specs=[
            grp((SSM_ROW, SSM_ROW)), grp((SSM_ROW, SSM_SDIM)), grp((SSM_SDIM, SSM_ROW)),
            grp((8, SSM_SDIM)), grp((8, SSM_SDIM)),
        ],
        out_shape=[
            jax.ShapeDtypeStruct((g, SSM_ROW, SSM_ROW), BF16),
            jax.ShapeDtypeStruct((g, SSM_ROW, SSM_SDIM), BF16),
            jax.ShapeDtypeStruct((g, SSM_SDIM, SSM_ROW), BF16),
            jax.ShapeDtypeStruct((g, 8, SSM_SDIM), F32),
            jax.ShapeDtypeStruct((g, 8, SSM_SDIM), F32),
        ],
        compiler_params=_cparams(("parallel",)),
        name="ssm_tables",
    )(pc, pr, pr2, bret, bimt, ctr, cti, dtile, e16, x16)


def _ssm_apply_kernel(z_ref, m_ref, p_ref, q_ref, sca_ref, scb_ref, y_ref, *, segments):
    z = z_ref[0]
    y = jnp.dot(z, m_ref[0], preferred_element_type=F32)
    st = jnp.dot(z, p_ref[0], preferred_element_type=F32)
    sca = sca_ref[0]
    scb = scb_ref[0]
    carried = []
    for (r0, rows, nc) in segments:
        c_idx = lax.broadcasted_iota(jnp.int32, (rows, LANES), 0) % nc
        xf = st[r0:r0 + rows, :LANES]
        xb = st[r0:r0 + rows, LANES:]
        for kk in range(int(math.log2(nc))):
            d = 1 << kk
            sh = jnp.where(c_idx >= d, pltpu.roll(xf, d, 0), 0.0)
            xf = xf + sh * sca[kk:kk + 1, :LANES] + pltpu.roll(sh, SSM_STATE, 1) * scb[kk:kk + 1, :LANES]
            sh = jnp.where(c_idx < nc - d, pltpu.roll(xb, rows - d, 0), 0.0)
            xb = xb + sh * sca[kk:kk + 1, LANES:] + pltpu.roll(sh, SSM_STATE, 1) * scb[kk:kk + 1, LANES:]
        prev_f = jnp.where(c_idx >= 1, pltpu.roll(xf, 1, 0), 0.0)
        next_b = jnp.where(c_idx < nc - 1, pltpu.roll(xb, rows - 1, 0), 0.0)
        carried.append(jnp.concatenate([prev_f, next_b], axis=1))
    xin = jnp.concatenate(carried, axis=0).astype(BF16)
    y_ref[0] = (y + jnp.dot(xin, q_ref[0], preferred_element_type=F32)).astype(y_ref.dtype)


def _ssm_apply(z, m_tab, p_tab, q_tab, sca, scb, segments):
    g, rows, _ = z.shape
    grp = lambda shape: pl.BlockSpec((1,) + shape, lambda i: (i, 0, 0))
    return pl.pallas_call(
        functools.partial(_ssm_apply_kernel, segments=segments),
        grid=(g,),
        in_specs=[grp((rows, SSM_ROW)), grp((SSM_ROW, SSM_ROW)), grp((SSM_ROW, SSM_SDIM)),
                  grp((SSM_SDIM, SSM_ROW)), grp((8, SSM_SDIM)), grp((8, SSM_SDIM))],
        out_specs=grp((rows, SSM_ROW)),
        out_shape=jax.ShapeDtypeStruct((g, rows, SSM_ROW), BF16),
        compiler_params=_cparams(("parallel",)),
        name="ssm_apply",
    )(z, m_tab, p_tab, q_tab, sca, scb)


def _mix0_kernel(x_ref, a_ref, y_ref, wg_ref, bg_ref, woa_ref, wos_ref, g_ref, b_ref, o_ref):
    y = jax.nn.gelu(y_ref[...].astype(F32))
    gate = jnp.dot(y.astype(BF16), wg_ref[...], preferred_element_type=F32) + bg_ref[...]
    zs = (y * jax.nn.sigmoid(gate)).astype(BF16)
    mix = (jnp.dot(a_ref[...], woa_ref[...], preferred_element_type=F32)
           + jnp.dot(zs, wos_ref[...], preferred_element_type=F32))
    o_ref[...] = _layer_norm(ALPHA * x_ref[...] + mix, g_ref[...], b_ref[...])


def _mix0(x2, attn, y, w_glu, b_glu, w_out_a, w_out_s, ln_g, ln_b):
    n = x2.shape[0]
    tm = TOKEN_TILE
    tok = lambda w: pl.BlockSpec((tm, w), lambda i: (i, 0))
    return pl.pallas_call(
        _mix0_kernel,
        grid=(n // tm,),
        in_specs=[tok(D_MODEL), tok(ATTN_WIDTH), tok(SSM_WIDTH),
                  _const_spec((SSM_WIDTH, SSM_WIDTH)), _const_spec((1, SSM_WIDTH)),
                  _const_spec((ATTN_WIDTH, D_MODEL)), _const_spec((SSM_WIDTH, D_MODEL)),
                  _const_spec((1, D_MODEL)), _const_spec((1, D_MODEL))],
        out_specs=tok(D_MODEL),
        out_shape=jax.ShapeDtypeStruct((n, D_MODEL), F32),
        compiler_params=_cparams(("parallel",)),
        name="mix0",
    )(x2, attn, y, w_glu, b_glu, w_out_a, w_out_s, ln_g, ln_b)


FF_CHUNK = 1024


def _ffn_kernel(x_ref, w1_ref, w2_ref, g_ref, b_ref, o_ref):
    x = x_ref[...]
    xb = x.astype(BF16)
    acc = ALPHA * x
    for c in range(D_FF // FF_CHUNK):
        h = jnp.dot(xb, w1_ref[:, c * FF_CHUNK:(c + 1) * FF_CHUNK], preferred_element_type=F32)
        h = jnp.maximum(h, 0.0)
        acc = acc + jnp.dot((h * h).astype(BF16), w2_ref[c * FF_CHUNK:(c + 1) * FF_CHUNK, :],
                            preferred_element_type=F32)
    o_ref[...] = _layer_norm(acc, g_ref[...], b_ref[...])


def _ffn(x2, w1, w2, ln_g, ln_b):
    n = x2.shape[0]
    tm = TOKEN_TILE
    tok = pl.BlockSpec((tm, D_MODEL), lambda i: (i, 0))
    return pl.pallas_call(
        _ffn_kernel,
        grid=(n // tm,),
        in_specs=[tok, _const_spec((D_MODEL, D_FF)), _const_spec((D_FF, D_MODEL)),
                  _const_spec((1, D_MODEL)), _const_spec((1, D_MODEL))],
        out_specs=tok,
        out_shape=jax.ShapeDtypeStruct((n, D_MODEL), F32),
        compiler_params=_cparams(("parallel",)),
        name="ffn",
    )(x2, w1, w2, ln_g, ln_b)


def _sgu_kernel(x_ref, win_ref, lg_ref, lb_ref, ws_ref, bs_ref, wout_ref, g_ref, b_ref, o_ref):
    x = x_ref[...]
    tm = x.shape[0]
    h = jax.nn.gelu(jnp.dot(x.astype(BF16), win_ref[...], preferred_element_type=F32))
    u = h[:, :D_MODEL]
    vb = _layer_norm(h[:, D_MODEL:], lg_ref[...], lb_ref[...]).astype(BF16)
    rows = []
    for c in range(tm // SGU_CHUNK):
        cols = []
        for hd in range(SGU_HEADS):
            blk = vb[c * SGU_CHUNK:(c + 1) * SGU_CHUNK, hd * LANES:(hd + 1) * LANES]
            cols.append(jnp.dot(ws_ref[hd], blk, preferred_element_type=F32) + bs_ref[hd])
        rows.append(jnp.concatenate(cols, axis=1))
    mixed = jnp.concatenate(rows, axis=0)
    out = jnp.dot((u * mixed).astype(BF16), wout_ref[...], preferred_element_type=F32)
    o_ref[...] = _layer_norm(ALPHA * x + out, g_ref[...], b_ref[...])


def _sgu(x2, w_in, ln_g, ln_b, w_s, b_s_full, w_out, mg, mb):
    n = x2.shape[0]
    tm = TOKEN_TILE
    tok = pl.BlockSpec((tm, D_MODEL), lambda i: (i, 0))
    return pl.pallas_call(
        _sgu_kernel,
        grid=(n // tm,),
        in_specs=[tok, _const_spec((D_MODEL, 2 * D_MODEL)), _const_spec((1, D_MODEL)), _const_spec((1, D_MODEL)),
                  _const_spec((SGU_HEADS, SGU_CHUNK, SGU_CHUNK)), _const_spec((SGU_HEADS, SGU_CHUNK, LANES)),
                  _const_spec((D_MODEL, D_MODEL)), _const_spec((1, D_MODEL)), _const_spec((1, D_MODEL))],
        out_specs=tok,
        out_shape=jax.ShapeDtypeStruct((n, D_MODEL), F32),
        compiler_params=_cparams(("parallel",)),
        name="sgu",
    )(x2, w_in, ln_g, ln_b, w_s, b_s_full, w_out, mg, mb)


_HEAD_OF_SLOT = np.array([0, 4, 1, 5, 2, 6, 3, 7])


def _rope_tables(seq_len):
    t = np.arange(seq_len)
    row = (t // GRID_W).astype(np.float32)
    col = (t % GRID_W).astype(np.float32)
    axis_dim = HEAD_DIM // 2
    inv_freq = ROPE_THETA ** (-jnp.arange(0, axis_dim, 2, dtype=F32) / axis_dim)
    ang = jnp.concatenate([jnp.asarray(row)[:, None] * inv_freq, jnp.asarray(col)[:, None] * inv_freq], axis=-1)
    cos = jnp.repeat(jnp.cos(ang), 2, axis=-1)
    sin = jnp.repeat(jnp.sin(ang), 2, axis=-1)
    sign = jnp.tile(jnp.asarray([-1.0, 1.0], F32), HEAD_DIM // 2)
    reps = LANES // HEAD_DIM
    return jnp.tile(cos, (1, reps)), jnp.tile(sin * sign, (1, reps))


def _selectors():
    col = np.arange(SSM_TAPS)
    row = np.arange(LANES)
    e16 = (col[None, :] // SSM_GROUP == row[:, None])
    r2 = np.arange(SSM_ROW)
    x16 = (r2[:, None] // SSM_GROUP == np.arange(SSM_CHUNK)[None, :])
    as_bf16 = lambda a: jnp.asarray(a.astype(np.float32), BF16)
    return as_bf16(e16), as_bf16(x16)


def _ssm_param_layout(a_re_f, a_im_f, ls_f, a_re_b, a_im_b, ls_b, b_re, b_im, c_re, c_im, d):
    g, ns = SSM_GROUPS, SSM_STATE
    lsf = jnp.broadcast_to(ls_f[:, None], (g, ns))
    lsb = jnp.broadcast_to(ls_b[:, None], (g, ns))
    zeros = jnp.zeros((g, ns), F32)
    per_f = jnp.stack([a_re_f, a_im_f, lsf] + [zeros] * 5, axis=1)
    per_b = jnp.stack([a_re_b, a_im_b, lsb] + [zeros] * 5, axis=1)
    pr = jnp.concatenate([per_f, per_f, per_b, per_b], axis=2)
    pr2 = jnp.concatenate([per_f, per_b], axis=2)
    pc = jnp.transpose(pr2, (0, 2, 1))
    def rows_t(b):
        bt = jnp.transpose(b, (0, 2, 1))
        return jnp.concatenate([bt, bt, bt, bt], axis=2)
    def cols_t(c):
        return jnp.pad(jnp.transpose(c, (0, 2, 1)), ((0, 0), (0, 0), (0, LANES - SSM_GROUP)))
    dtile = jnp.tile(d.reshape(g, 1, SSM_GROUP), (1, 1, SSM_LAGS))
    return pc, pr, pr2, rows_t(b_re), rows_t(b_im), cols_t(c_re), cols_t(c_im), dtile


def kernel(x_prompt, x_sample, ab_w_in, ab_q_norm, ab_k_norm, ssm_a_re_f, ssm_a_im_f, ssm_log_step_f,
           ssm_a_re_b, ssm_a_im_b, ssm_log_step_b, ssm_b_re, ssm_b_im, ssm_c_re, ssm_c_im, ssm_d, ssm_w_glu,
           ssm_b_glu, ab_w_out, c_w_in, c_ln_g, c_ln_b, c_w_s, c_b_s, c_w_out, ln_mix_g, ln_mix_b, ff_w1, ff_w2,
           ln_ff_g, ln_ff_b):
    xs = [x_prompt, x_sample]
    row2 = lambda v: v.reshape(1, -1)

    slot_cols = (_HEAD_OF_SLOT[:, None] * HEAD_DIM + np.arange(HEAD_DIM)[None, :]).reshape(-1)
    w_in = ab_w_in[0]
    w_in = jnp.concatenate([w_in[:, slot_cols], w_in[:, ATTN_WIDTH:]], axis=1).astype(BF16)
    scale = HEAD_DIM ** -0.5 * math.log2(math.e)
    gain = jnp.concatenate([jnp.tile(ab_q_norm[0] * scale, N_HEADS), jnp.tile(ab_k_norm[0], N_KV_HEADS)])
    hid = np.arange(QK_WIDTH) // HEAD_DIM
    hmean = jnp.asarray((hid[:, None] == hid[None, :]).astype(np.float32) / HEAD_DIM, BF16)
    w_out = ab_w_out[0]
    w_out_a = w_out[:ATTN_WIDTH][slot_cols].astype(BF16)
    w_out_s = w_out[ATTN_WIDTH:].astype(BF16)
    w_glu = ssm_w_glu[0].astype(BF16)
    tabs = _ssm_tables(*_ssm_param_layout(
        ssm_a_re_f[0], ssm_a_im_f[0], ssm_log_step_f[0], ssm_a_re_b[0], ssm_a_im_b[0], ssm_log_step_b[0],
        ssm_b_re[0], ssm_b_im[0], ssm_c_re[0], ssm_c_im[0], ssm_d[0]), *_selectors())

    attn, zs, segments = [], [], []
    r0 = 0
    for x in xs:
        b, seq_len, _ = x.shape
        cos_t, sin_t = _rope_tables(seq_len)
        q, k, v, u = _inproj(x.reshape(b * seq_len, D_MODEL), w_in, row2(gain), cos_t, sin_t, hmean, seq_len)
        attn.append(_attention(q.reshape(b, seq_len, ATTN_WIDTH), k.reshape(b, seq_len, KV_WIDTH),
                               v.reshape(b, seq_len, KV_WIDTH)).reshape(b * seq_len, ATTN_WIDTH))
        nc = seq_len // SSM_CHUNK
        z = u.reshape(b * nc, SSM_CHUNK, SSM_GROUPS, SSM_GROUP)
        zs.append(jnp.transpose(z, (2, 0, 1, 3)).reshape(SSM_GROUPS, b * nc, SSM_ROW))
        segments.append((r0, b * nc, nc))
        r0 += b * nc
    y_all = _ssm_apply(jnp.concatenate(zs, axis=1), *tabs, tuple(segments))

    outs = []
    for x, a, (r0, rows, nc) in zip(xs, attn, segments):
        b, seq_len, _ = x.shape
        n = b * seq_len
        y = y_all[:, r0:r0 + rows].reshape(SSM_GROUPS, rows, SSM_CHUNK, SSM_GROUP)
        y = jnp.transpose(y, (1, 2, 0, 3)).reshape(n, SSM_WIDTH)
        h = _mix0(x.reshape(n, D_MODEL), a, y, w_glu, row2(ssm_b_glu[0]), w_out_a, w_out_s,
                  row2(ln_mix_g[0]), row2(ln_mix_b[0]))
        h = _ffn(h, ff_w1[0].astype(BF16), ff_w2[0].astype(BF16), row2(ln_ff_g[0]), row2(ln_ff_b[0]))
        bs_full = jnp.broadcast_to(c_b_s[0][:, :, None], (SGU_HEADS, SGU_CHUNK, LANES))
        h = _sgu(h, c_w_in[0].astype(BF16), row2(c_ln_g[0]), row2(c_ln_b[0]), c_w_s[0].astype(BF16), bs_full,
                 c_w_out[0].astype(BF16), row2(ln_mix_g[1]), row2(ln_mix_b[1]))
        h = _ffn(h, ff_w1[1].astype(BF16), ff_w2[1].astype(BF16), row2(ln_ff_g[1]), row2(ln_ff_b[1]))
        outs.append(h.reshape(b, seq_len, D_MODEL))
    return tuple(outs)
```

```python
import functools
import math

import numpy as np
import jax
import jax.numpy as jnp
from jax import lax
from jax.experimental import pallas as pl
from jax.experimental.pallas import tpu as pltpu

F32 = jnp.float32
BF16 = jnp.bfloat16

D_MODEL = 1024
GRID_W = 64
N_HEADS = 8
N_KV_HEADS = 2
HEAD_DIM = 64
ATTN_WIDTH = N_HEADS * HEAD_DIM
KV_WIDTH = N_KV_HEADS * HEAD_DIM
QK_WIDTH = ATTN_WIDTH + KV_WIDTH
Q_BLOCK = 128
ROPE_THETA = 10000.0
SSM_WIDTH = D_MODEL - ATTN_WIDTH
SSM_GROUP = 16
SSM_GROUPS = SSM_WIDTH // SSM_GROUP
SSM_STATE = 64
SGU_HEADS = 8
SGU_CHUNK = 128
D_FF = 4 * D_MODEL
DEPTH = 2
AB_IN = ATTN_WIDTH + 2 * KV_WIDTH + SSM_WIDTH
ALPHA = (2 * DEPTH) ** 0.25
EPS = 1e-6

LANES = 128
VMEM_LIMIT_BYTES = 56 * 1024 * 1024

SSM_CHUNK = 64
SSM_ROW = SSM_CHUNK * SSM_GROUP
SSM_LAGS = 2 * SSM_CHUNK
SSM_TAPS = SSM_LAGS * SSM_GROUP
SSM_SDIM = 4 * SSM_STATE

TOKEN_TILE = 512


def _cparams(sem):
    return pltpu.CompilerParams(dimension_semantics=sem, vmem_limit_bytes=VMEM_LIMIT_BYTES)


def _const_spec(shape):
    nd = len(shape)
    return pl.BlockSpec(shape, lambda *_: (0,) * nd, pipeline_mode=pl.Buffered(1))


def _layer_norm(r, g, b):
    mu = jnp.mean(r, axis=-1, keepdims=True)
    d = r - mu
    var = jnp.mean(d * d, axis=-1, keepdims=True)
    return d * lax.rsqrt(var + EPS) * g + b


def _split_bf16(x):
    hi = x.astype(BF16)
    lo = (x - hi.astype(F32)).astype(BF16)
    return hi, lo


def _select_cols(x, e):
    hi, lo = _split_bf16(x)
    y = jnp.dot(jnp.concatenate([hi, lo], axis=0), e, preferred_element_type=F32)
    return y[:x.shape[0]] + y[x.shape[0]:]


def _select_rows(e, x):
    hi, lo = _split_bf16(x)
    y = jnp.dot(e, jnp.concatenate([hi, lo], axis=1), preferred_element_type=F32)
    return y[:, :x.shape[1]] + y[:, x.shape[1]:]


def _inproj_kernel(x_ref, w_ref, gain_ref, cos_ref, sin_ref, hmean_ref, q_ref, k_ref, v_ref, u_ref):
    xb = x_ref[...].astype(BF16)
    h = jnp.dot(xb, w_ref[...], preferred_element_type=F32)
    qk = h[:, :QK_WIDTH]
    ms = jnp.dot((qk * qk).astype(BF16), hmean_ref[...], preferred_element_type=F32)
    qkn = qk * lax.rsqrt(ms + EPS) * gain_ref[...]
    c = cos_ref[...]
    s = sin_ref[...]
    even = (lax.broadcasted_iota(jnp.int32, c.shape, 1) % 2) == 0
    outs = []
    for j in range(QK_WIDTH // LANES):
        xs = qkn[:, j * LANES:(j + 1) * LANES]
        nxt = pltpu.roll(xs, LANES - 1, 1)
        prv = pltpu.roll(xs, 1, 1)
        outs.append(xs * c + jnp.where(even, nxt, prv) * s)
    for j in range(ATTN_WIDTH // LANES):
        q_ref[:, j * LANES:(j + 1) * LANES] = outs[j].astype(BF16)
    k_ref[...] = outs[ATTN_WIDTH // LANES].astype(BF16)
    v_ref[...] = h[:, QK_WIDTH:QK_WIDTH + KV_WIDTH].astype(BF16)
    u_ref[...] = h[:, QK_WIDTH + KV_WIDTH:].astype(BF16)


def _inproj(x2, w_in, gain, cos_t, sin_t, hmean, seq_len):
    n = x2.shape[0]
    tm = TOKEN_TILE
    nper = seq_len // tm
    return pl.pallas_call(
        _inproj_kernel,
        grid=(n // tm,),
        in_specs=[
            pl.BlockSpec((tm, D_MODEL), lambda i: (i, 0)),
            _const_spec((D_MODEL, AB_IN)),
            _const_spec((1, QK_WIDTH)),
            pl.BlockSpec((tm, LANES), lambda i: (i % nper, 0)),
            pl.BlockSpec((tm, LANES), lambda i: (i % nper, 0)),
            _const_spec((QK_WIDTH, QK_WIDTH)),
        ],
        out_specs=[
            pl.BlockSpec((tm, ATTN_WIDTH), lambda i: (i, 0)),
            pl.BlockSpec((tm, KV_WIDTH), lambda i: (i, 0)),
            pl.BlockSpec((tm, KV_WIDTH), lambda i: (i, 0)),
            pl.BlockSpec((tm, SSM_WIDTH), lambda i: (i, 0)),
        ],
        out_shape=[
            jax.ShapeDtypeStruct((n, ATTN_WIDTH), BF16),
            jax.ShapeDtypeStruct((n, KV_WIDTH), BF16),
            jax.ShapeDtypeStruct((n, KV_WIDTH), BF16),
            jax.ShapeDtypeStruct((n, SSM_WIDTH), BF16),
        ],
        compiler_params=_cparams(("parallel",)),
        name="inproj",
    )(x2, w_in, gain, cos_t, sin_t, hmean)


ATT_UNIT_Q = 64


ATT_SHIFT_LIMIT = 60.0


def _attn_kernel(bound_ref, q_ref, k_ref, v_ref, o_ref,
                 s0_ref, s1_ref, p0_ref, p1_ref, m0_ref, m1_ref, l0_ref, l1_ref):
    seq_len = k_ref.shape[1]
    n_units = seq_len // ATT_UNIT_Q
    rep = N_HEADS // N_KV_HEADS
    lane = lax.broadcasted_iota(jnp.int32, (ATT_UNIT_Q, LANES), 1)
    lo_half = lane < HEAD_DIM
    zero = jnp.zeros((ATT_UNIT_Q, LANES), BF16)
    bound = bound_ref[0, 0]

    def rows_of(t):
        return pl.ds(pl.multiple_of(t * ATT_UNIT_Q, ATT_UNIT_Q), ATT_UNIT_Q)

    def raw_scores(t):
        q = q_ref[0, rows_of(t), :]
        slots = [q[:, m * LANES:(m + 1) * LANES] for m in range(rep)]
        qs = jnp.concatenate([jnp.where(lo_half, x, zero) for x in slots]
                             + [jnp.where(lo_half, zero, x) for x in slots], axis=0)
        return lax.dot_general(qs, k_ref[0], (((1,), (1,)), ((), ())), preferred_element_type=F32)

    def scores(t, s_ref, m_ref):
        s = raw_scores(t)
        s_ref[...] = s
        m_ref[...] = jnp.max(s, axis=-1, keepdims=True)

    def probs(s_ref, m_ref, p_ref, l_ref):
        p = jnp.exp2(s_ref[...] - m_ref[...])
        l_ref[...] = jnp.sum(p, axis=-1, keepdims=True)
        p_ref[...] = p.astype(BF16)

    def bounded_probs(t, p_ref, l_ref):
        p = jnp.exp2(raw_scores(t) - bound)
        l_ref[...] = jnp.sum(p, axis=-1, keepdims=True)
        p_ref[...] = p.astype(BF16)

    def output(t, p_ref, l_ref):
        o = jnp.dot(p_ref[...], v_ref[0], preferred_element_type=F32) / l_ref[...]
        for m in range(rep):
            g0 = o[m * ATT_UNIT_Q:(m + 1) * ATT_UNIT_Q]
            g1 = o[(rep + m) * ATT_UNIT_Q:(rep + m + 1) * ATT_UNIT_Q]
            o_ref[0, rows_of(t), m * LANES:(m + 1) * LANES] = jnp.where(lo_half, g0, g1).astype(BF16)

    @pl.when(bound <= ATT_SHIFT_LIMIT)
    def _bounded_shift():
        bounded_probs(0, p0_ref, l0_ref)

        def body(i, carry):
            t = 2 * i + 1
            bounded_probs(t, p1_ref, l1_ref)
            output(t - 1, p0_ref, l0_ref)
            bounded_probs(t + 1, p0_ref, l0_ref)
            output(t, p1_ref, l1_ref)
            return carry

        lax.fori_loop(0, (n_units - 2) // 2, body, 0)
        bounded_probs(n_units - 1, p1_ref, l1_ref)
        output(n_units - 2, p0_ref, l0_ref)
        output(n_units - 1, p1_ref, l1_ref)

    @pl.when(bound > ATT_SHIFT_LIMIT)
    def _row_max_shift():
        scores(0, s0_ref, m0_ref)
        scores(1, s1_ref, m1_ref)
        probs(s0_ref, m0_ref, p0_ref, l0_ref)

        def body(i, carry):
            t = 2 * i + 2
            scores(t, s0_ref, m0_ref)
            probs(s1_ref, m1_ref, p1_ref, l1_ref)
            output(t - 2, p0_ref, l0_ref)
            scores(t + 1, s1_ref, m1_ref)
            probs(s0_ref, m0_ref, p0_ref, l0_ref)
            output(t - 1, p1_ref, l1_ref)
            return carry

        lax.fori_loop(0, (n_units - 2) // 2, body, 0)
        probs(s1_ref, m1_ref, p1_ref, l1_ref)
        output(n_units - 2, p0_ref, l0_ref)
        output(n_units - 1, p1_ref, l1_ref)


def _attention(bound, q, k, v):
    b, seq_len, _ = q.shape
    rows = ATT_UNIT_Q * N_HEADS
    seq = lambda w, **kw: pl.BlockSpec((1, seq_len, w), lambda i: (i, 0, 0), **kw)
    once = dict(pipeline_mode=pl.Buffered(1))
    return pl.pallas_call(
        _attn_kernel,
        grid=(b,),
        in_specs=[pl.BlockSpec(memory_space=pltpu.SMEM),
                  seq(ATTN_WIDTH, **once), seq(KV_WIDTH, **once), seq(KV_WIDTH, **once)],
        out_specs=seq(ATTN_WIDTH),
        out_shape=jax.ShapeDtypeStruct((b, seq_len, ATTN_WIDTH), BF16),
        scratch_shapes=[pltpu.VMEM((rows, seq_len), F32), pltpu.VMEM((rows, seq_len), F32),
                        pltpu.VMEM((rows, seq_len), BF16), pltpu.VMEM((rows, seq_len), BF16),
                        pltpu.VMEM((rows, 1), F32), pltpu.VMEM((rows, 1), F32),
                        pltpu.VMEM((rows, 1), F32), pltpu.VMEM((rows, 1), F32)],
        compiler_params=_cparams(("parallel",)),
        name="attention",
    )(bound, q, k, v)


def _ssm_table_kernel(pc_ref, pr_ref, pr2_ref, bret_ref, bimt_ref, ctr_ref, cti_ref, dt_ref, e16_ref, x16_ref,
                      m_ref, p_ref, q_ref, sca_ref, scb_ref):
    ns = SSM_STATE
    pc = pc_ref[0]
    are_c, aim_c, dt_c = pc[:, 0:1], pc[:, 1:2], jnp.exp(pc[:, 2:3])
    row = lax.broadcasted_iota(jnp.int32, (2 * ns, LANES), 0)
    lan = lax.broadcasted_iota(jnp.int32, (2 * ns, LANES), 1)
    n_g = jnp.where(row >= ns, (SSM_CHUNK - 1) - lan, lan - (SSM_CHUNK - 1))
    mask = n_g >= 0
    nf = jnp.where(mask, n_g, 0).astype(F32)
    rho = jnp.exp(are_c * dt_c * nf)
    ang = aim_c * dt_c * nf
    cs = jnp.where(mask, rho * jnp.cos(ang), 0.0)
    sn = jnp.where(mask, rho * jnp.sin(ang), 0.0)
    base = jnp.concatenate([cs[:ns], sn[:ns], cs[ns:], sn[ns:]], axis=0)
    ex = _select_cols(base, e16_ref[...])
    ecf, esf, ecb, esb = ex[:ns], ex[ns:2 * ns], ex[2 * ns:3 * ns], ex[3 * ns:]

    def tile_lanes(x, width):
        for sh in (SSM_GROUP, 2 * SSM_GROUP, 4 * SSM_GROUP):
            x = x + pltpu.roll(x, sh, 1)
        return jnp.concatenate([x] * (width // LANES), axis=1)

    crt = tile_lanes(ctr_ref[0], SSM_TAPS)
    cit = tile_lanes(cti_ref[0], SSM_TAPS)
    g_fre = ecf * crt - esf * cit
    g_fim = -(esf * crt + ecf * cit)
    g_bre = ecb * crt - esb * cit
    g_bim = -(esb * crt + ecb * cit)
    g_tab = jnp.concatenate([g_fre, g_fim, g_bre, g_bim], axis=0)

    ab_r = cs[ns:, SSM_CHUNK - 2:SSM_CHUNK - 1]
    ab_i = sn[ns:, SSM_CHUNK - 2:SSM_CHUNK - 1]
    q_tab = jnp.concatenate([
        g_fre[:, SSM_ROW:], g_fim[:, SSM_ROW:],
        ab_r * g_bre[:, :SSM_ROW] + ab_i * g_bim[:, :SSM_ROW],
        ab_r * g_bim[:, :SSM_ROW] - ab_i * g_bre[:, :SSM_ROW]], axis=0)
    q_ref[0] = q_tab.astype(BF16)

    pr = pr_ref[0]
    are_r, aim_r, ls_r = pr[0:1, :], pr[1:2, :], pr[2:3, :]
    dt_r = jnp.exp(ls_r)
    rho1 = jnp.exp(are_r * dt_r)
    th1 = aim_r * dt_r
    abr = rho1 * jnp.cos(th1) - 1.0
    abi = rho1 * jnp.sin(th1)
    den = are_r * are_r + aim_r * aim_r
    cr = (abr * are_r + abi * aim_r) / den
    ci = (abi * are_r - abr * aim_r) / den
    bret = bret_ref[0]
    bimt = bimt_ref[0]
    bbr = cr * bret - ci * bimt
    bbi = cr * bimt + ci * bret
    lan_r = lax.broadcasted_iota(jnp.int32, (SSM_GROUP, SSM_SDIM), 1)
    im_lane = ((lan_r // ns) % 2) == 1
    bcat = jnp.where(im_lane, bbi, bbr)

    bh, bl = _split_bf16(bcat)
    gh, gl = _split_bf16(g_tab)
    w = (jnp.dot(bh, gh, preferred_element_type=F32) + jnp.dot(bh, gl, preferred_element_type=F32)
         + jnp.dot(bl, gh, preferred_element_type=F32))
    wrow = lax.broadcasted_iota(jnp.int32, (SSM_GROUP, SSM_TAPS), 0)
    wcol = lax.broadcasted_iota(jnp.int32, (SSM_GROUP, SSM_TAPS), 1)
    on_diag = (wcol // SSM_GROUP == SSM_CHUNK - 1) & (wcol % SSM_GROUP == wrow)
    w = w + jnp.where(on_diag, dt_ref[0], 0.0)

    per_vreg = LANES // SSM_GROUP
    rolled = [w] + [pltpu.roll(w, SSM_TAPS - SSM_GROUP * kk, 1) for kk in range(1, per_vreg)]
    for i in range(SSM_CHUNK):
        r = SSM_CHUNK - 1 - i
        blk = rolled[r % per_vreg][:, (r // per_vreg) * LANES:(r // per_vreg) * LANES + SSM_ROW]
        m_ref[0, i * SSM_GROUP:(i + 1) * SSM_GROUP, :] = blk.astype(BF16)

    pr2 = pr2_ref[0]
    i_r = lax.broadcasted_iota(jnp.int32, (SSM_CHUNK, LANES), 0)
    l_r = lax.broadcasted_iota(jnp.int32, (SSM_CHUNK, LANES), 1)
    lo_r = l_r < ns
    n_p = jnp.where(lo_r, (SSM_CHUNK - 1) - i_r, i_r).astype(F32)
    dt2 = jnp.exp(pr2[2:3, :])
    rho = jnp.exp(pr2[0:1, :] * dt2 * n_p)
    ang = pr2[1:2, :] * dt2 * n_p
    c = rho * jnp.cos(ang)
    s = rho * jnp.sin(ang)
    base_p = jnp.concatenate([jnp.where(lo_r, c, pltpu.roll(s, ns, 1)),
                              jnp.where(lo_r, pltpu.roll(c, ns, 1), s)], axis=1)
    px = _select_rows(x16_ref[...], base_p)
    sign = jnp.where(lax.broadcasted_iota(jnp.int32, (SSM_ROW, LANES), 1) < ns, -1.0, 1.0)
    px2 = jnp.concatenate([pltpu.roll(px[:, :LANES], ns, 1) * sign,
                           pltpu.roll(px[:, LANES:], ns, 1) * sign], axis=1)
    bt_r = jnp.concatenate([bbr] * SSM_CHUNK, axis=0)
    bt_i = jnp.concatenate([bbi] * SSM_CHUNK, axis=0)
    p_ref[0] = (px * bt_r + px2 * bt_i).astype(BF16)

    k_r = lax.broadcasted_iota(jnp.int32, (8, SSM_SDIM), 0)
    l8 = lax.broadcasted_iota(jnp.int32, (8, SSM_SDIM), 1)
    n_s = (SSM_CHUNK * jnp.left_shift(1, k_r)).astype(F32)
    rho = jnp.exp(are_r * dt_r * n_s)
    ang = aim_r * dt_r * n_s
    sca_ref[0] = rho * jnp.cos(ang)
    scb_ref[0] = rho * jnp.sin(ang) * jnp.where(((l8 // ns) % 2) == 1, 1.0, -1.0)


def _ssm_tables(pc, pr, pr2, bret, bimt, ctr, cti, dtile, e16, x16):
    g = SSM_GROUPS
    grp = lambda shape: pl.BlockSpec((1,) + shape, lambda i: (i, 0, 0))
    return pl.pallas_call(
        _ssm_table_kernel,
        grid=(g,),
        in_specs=[
            grp((2 * SSM_STATE, 8)), grp((8, SSM_SDIM)), grp((8, LANES)),
            grp((SSM_GROUP, SSM_SDIM)), grp((SSM_GROUP, SSM_SDIM)),
            grp((SSM_STATE, LANES)), grp((SSM_STATE, LANES)), grp((1, SSM_TAPS)),
            _const_spec((LANES, SSM_TAPS)), _const_spec((SSM_ROW, SSM_CHUNK)),
        ],
        out_specs=[
            grp((SSM_ROW, SSM_ROW)), grp((SSM_ROW, SSM_SDIM)), grp((SSM_SDIM, SSM_ROW)),
            grp((8, SSM_SDIM)), grp((8, SSM_SDIM)),
        ],
        out_shape=[
            jax.ShapeDtypeStruct((g, SSM_ROW, SSM_ROW), BF16),
            jax.ShapeDtypeStruct((g, SSM_ROW, SSM_SDIM), BF16),
            jax.ShapeDtypeStruct((g, SSM_SDIM, SSM_ROW), BF16),
            jax.ShapeDtypeStruct((g, 8, SSM_SDIM), F32),
            jax.ShapeDtypeStruct((g, 8, SSM_SDIM), F32),
        ],
        compiler_params=_cparams(("parallel",)),
        name="ssm_tables",
    )(pc, pr, pr2, bret, bimt, ctr, cti, dtile, e16, x16)


def _ssm_apply_kernel(z_ref, m_ref, p_ref, q_ref, sca_ref, scb_ref, y_ref, *, segments):
    z = z_ref[0]
    y = jnp.dot(z, m_ref[0], preferred_element_type=F32)
    st = jnp.dot(z, p_ref[0], preferred_element_type=F32)
    sca = sca_ref[0]
    scb = scb_ref[0]
    carried = []
    for (r0, rows, nc) in segments:
        c_idx = lax.broadcasted_iota(jnp.int32, (rows, LANES), 0) % nc
        xf = st[r0:r0 + rows, :LANES]
        xb = st[r0:r0 + rows, LANES:]
        for kk in range(int(math.log2(nc))):
            d = 1 << kk
            sh = jnp.where(c_idx >= d, pltpu.roll(xf, d, 0), 0.0)
            xf = xf + sh * sca[kk:kk + 1, :LANES] + pltpu.roll(sh, SSM_STATE, 1) * scb[kk:kk + 1, :LANES]
            sh = jnp.where(c_idx < nc - d, pltpu.roll(xb, rows - d, 0), 0.0)
            xb = xb + sh * sca[kk:kk + 1, LANES:] + pltpu.roll(sh, SSM_STATE, 1) * scb[kk:kk + 1, LANES:]
        prev_f = jnp.where(c_idx >= 1, pltpu.roll(xf, 1, 0), 0.0)
        next_b = jnp.where(c_idx < nc - 1, pltpu.roll(xb, rows - 1, 0), 0.0)
        carried.append(jnp.concatenate([prev_f, next_b], axis=1))
    xin = jnp.concatenate(carried, axis=0).astype(BF16)
    y_ref[0] = (y + jnp.dot(xin, q_ref[0], preferred_element_type=F32)).astype(y_ref.dtype)


def _ssm_apply(z, m_tab, p_tab, q_tab, sca, scb, segments):
    g, rows, _ = z.shape
    grp = lambda shape: pl.BlockSpec((1,) + shape, lambda i: (i, 0, 0))
    return pl.pallas_call(
        functools.partial(_ssm_apply_kernel, segments=segments),
        grid=(g,),
        in_specs=[grp((rows, SSM_ROW)), grp((SSM_ROW, SSM_ROW)), grp((SSM_ROW, SSM_SDIM)),
                  grp((SSM_SDIM, SSM_ROW)), grp((8, SSM_SDIM)), grp((8, SSM_SDIM))],
        out_specs=grp((rows, SSM_ROW)),
        out_shape=jax.ShapeDtypeStruct((g, rows, SSM_ROW), BF16),
        compiler_params=_cparams(("parallel",)),
        name="ssm_apply",
    )(z, m_tab, p_tab, q_tab, sca, scb)


def _mix0_kernel(x_ref, a_ref, y_ref, wg_ref, bg_ref, woa_ref, wos_ref, g_ref, b_ref, o_ref):
    y = jax.nn.gelu(y_ref[...].astype(F32))
    gate = jnp.dot(y.astype(BF16), wg_ref[...], preferred_element_type=F32) + bg_ref[...]
    zs = (y * jax.nn.sigmoid(gate)).astype(BF16)
    mix = (jnp.dot(a_ref[...], woa_ref[...], preferred_element_type=F32)
           + jnp.dot(zs, wos_ref[...], preferred_element_type=F32))
    o_ref[...] = _layer_norm(ALPHA * x_ref[...] + mix, g_ref[...], b_ref[...])


def _mix0(x2, attn, y, w_glu, b_glu, w_out_a, w_out_s, ln_g, ln_b):
    n = x2.shape[0]
    tm = TOKEN_TILE
    tok = lambda w: pl.BlockSpec((tm, w), lambda i: (i, 0))
    return pl.pallas_call(
        _mix0_kernel,
        grid=(n // tm,),
        in_specs=[tok(D_MODEL), tok(ATTN_WIDTH), tok(SSM_WIDTH),
                  _const_spec((SSM_WIDTH, SSM_WIDTH)), _const_spec((1, SSM_WIDTH)),
                  _const_spec((ATTN_WIDTH, D_MODEL)), _const_spec((SSM_WIDTH, D_MODEL)),
                  _const_spec((1, D_MODEL)), _const_spec((1, D_MODEL))],
        out_specs=tok(D_MODEL),
        out_shape=jax.ShapeDtypeStruct((n, D_MODEL), F32),
        compiler_params=_cparams(("parallel",)),
        name="mix0",
    )(x2, attn, y, w_glu, b_glu, w_out_a, w_out_s, ln_g, ln_b)


FF_CHUNK = 1024


def _ffn_kernel(x_ref, w1_ref, w2_ref, g_ref, b_ref, o_ref):
    x = x_ref[...]
    xb = x.astype(BF16)
    acc = ALPHA * x
    for c in range(D_FF // FF_CHUNK):
        h = jnp.dot(xb, w1_ref[:, c * FF_CHUNK:(c + 1) * FF_CHUNK], preferred_element_type=F32)
        h = jnp.maximum(h, 0.0)
        acc = acc + jnp.dot((h * h).astype(BF16), w2_ref[c * FF_CHUNK:(c + 1) * FF_CHUNK, :],
                            preferred_element_type=F32)
    o_ref[...] = _layer_norm(acc, g_ref[...], b_ref[...])


def _ffn(x2, w1, w2, ln_g, ln_b):
    n = x2.shape[0]
    tm = TOKEN_TILE
    tok = pl.BlockSpec((tm, D_MODEL), lambda i: (i, 0))
    return pl.pallas_call(
        _ffn_kernel,
        grid=(n // tm,),
        in_specs=[tok, _const_spec((D_MODEL, D_FF)), _const_spec((D_FF, D_MODEL)),
                  _const_spec((1, D_MODEL)), _const_spec((1, D_MODEL))],
        out_specs=tok,
        out_shape=jax.ShapeDtypeStruct((n, D_MODEL), F32),
        compiler_params=_cparams(("parallel",)),
        name="ffn",
    )(x2, w1, w2, ln_g, ln_b)


def _sgu_kernel(x_ref, win_ref, lg_ref, lb_ref, ws_ref, bs_ref, wout_ref, g_ref, b_ref, o_ref):
    x = x_ref[...]
    tm = x.shape[0]
    h = jax.nn.gelu(jnp.dot(x.astype(BF16), win_ref[...], preferred_element_type=F32))
    u = h[:, :D_MODEL]
    vb = _layer_norm(h[:, D_MODEL:], lg_ref[...], lb_ref[...]).astype(BF16)
    rows = []
    for c in range(tm // SGU_CHUNK):
        cols = []
        for hd in range(SGU_HEADS):
            blk = vb[c * SGU_CHUNK:(c + 1) * SGU_CHUNK, hd * LANES:(hd + 1) * LANES]
            cols.append(jnp.dot(ws_ref[hd], blk, preferred_element_type=F32) + bs_ref[hd])
        rows.append(jnp.concatenate(cols, axis=1))
    mixed = jnp.concatenate(rows, axis=0)
    out = jnp.dot((u * mixed).astype(BF16), wout_ref[...], preferred_element_type=F32)
    o_ref[...] = _layer_norm(ALPHA * x + out, g_ref[...], b_ref[...])


def _sgu(x2, w_in, ln_g, ln_b, w_s, b_s_full, w_out, mg, mb):
    n = x2.shape[0]
    tm = TOKEN_TILE
    tok = pl.BlockSpec((tm, D_MODEL), lambda i: (i, 0))
    return pl.pallas_call(
        _sgu_kernel,
        grid=(n // tm,),
        in_specs=[tok, _const_spec((D_MODEL, 2 * D_MODEL)), _const_spec((1, D_MODEL)), _const_spec((1, D_MODEL)),
                  _const_spec((SGU_HEADS, SGU_CHUNK, SGU_CHUNK)), _const_spec((SGU_HEADS, SGU_CHUNK, LANES)),
                  _const_spec((D_MODEL, D_MODEL)), _const_spec((1, D_MODEL)), _const_spec((1, D_MODEL))],
        out_specs=tok,
        out_shape=jax.ShapeDtypeStruct((n, D_MODEL), F32),
        compiler_params=_cparams(("parallel",)),
        name="sgu",
    )(x2, w_in, ln_g, ln_b, w_s, b_s_full, w_out, mg, mb)


_HEAD_OF_SLOT = np.array([0, 4, 1, 5, 2, 6, 3, 7])


def _rope_tables(seq_len):
    t = np.arange(seq_len)
    row = (t // GRID_W).astype(np.float32)
    col = (t % GRID_W).astype(np.float32)
    axis_dim = HEAD_DIM // 2
    inv_freq = ROPE_THETA ** (-jnp.arange(0, axis_dim, 2, dtype=F32) / axis_dim)
    ang = jnp.concatenate([jnp.asarray(row)[:, None] * inv_freq, jnp.asarray(col)[:, None] * inv_freq], axis=-1)
    cos = jnp.repeat(jnp.cos(ang), 2, axis=-1)
    sin = jnp.repeat(jnp.sin(ang), 2, axis=-1)
    sign = jnp.tile(jnp.asarray([-1.0, 1.0], F32), HEAD_DIM // 2)
    reps = LANES // HEAD_DIM
    return jnp.tile(cos, (1, reps)), jnp.tile(sin * sign, (1, reps))


def _selectors():
    col = np.arange(SSM_TAPS)
    row = np.arange(LANES)
    e16 = (col[None, :] // SSM_GROUP == row[:, None])
    r2 = np.arange(SSM_ROW)
    x16 = (r2[:, None] // SSM_GROUP == np.arange(SSM_CHUNK)[None, :])
    as_bf16 = lambda a: jnp.asarray(a.astype(np.float32), BF16)
    return as_bf16(e16), as_bf16(x16)


def _ssm_param_layout(a_re_f, a_im_f, ls_f, a_re_b, a_im_b, ls_b, b_re, b_im, c_re, c_im, d):
    g, ns = SSM_GROUPS, SSM_STATE
    lsf = jnp.broadcast_to(ls_f[:, None], (g, ns))
    lsb = jnp.broadcast_to(ls_b[:, None], (g, ns))
    zeros = jnp.zeros((g, ns), F32)
    per_f = jnp.stack([a_re_f, a_im_f, lsf] + [zeros] * 5, axis=1)
    per_b = jnp.stack([a_re_b, a_im_b, lsb] + [zeros] * 5, axis=1)
    pr = jnp.concatenate([per_f, per_f, per_b, per_b], axis=2)
    pr2 = jnp.concatenate([per_f, per_b], axis=2)
    pc = jnp.transpose(pr2, (0, 2, 1))
    def rows_t(b):
        bt = jnp.transpose(b, (0, 2, 1))
        return jnp.concatenate([bt, bt, bt, bt], axis=2)
    def cols_t(c):
        return jnp.pad(jnp.transpose(c, (0, 2, 1)), ((0, 0), (0, 0), (0, LANES - SSM_GROUP)))
    dtile = jnp.tile(d.reshape(g, 1, SSM_GROUP), (1, 1, SSM_LAGS))
    return pc, pr, pr2, rows_t(b_re), rows_t(b_im), cols_t(c_re), cols_t(c_im), dtile


def kernel(x_prompt, x_sample, ab_w_in, ab_q_norm, ab_k_norm, ssm_a_re_f, ssm_a_im_f, ssm_log_step_f,
           ssm_a_re_b, ssm_a_im_b, ssm_log_step_b, ssm_b_re, ssm_b_im, ssm_c_re, ssm_c_im, ssm_d, ssm_w_glu,
           ssm_b_glu, ab_w_out, c_w_in, c_ln_g, c_ln_b, c_w_s, c_b_s, c_w_out, ln_mix_g, ln_mix_b, ff_w1, ff_w2,
           ln_ff_g, ln_ff_b):
    xs = [x_prompt, x_sample]
    row2 = lambda v: v.reshape(1, -1)

    slot_cols = (_HEAD_OF_SLOT[:, None] * HEAD_DIM + np.arange(HEAD_DIM)[None, :]).reshape(-1)
    w_in = ab_w_in[0]
    w_in = jnp.concatenate([w_in[:, slot_cols], w_in[:, ATTN_WIDTH:]], axis=1).astype(BF16)
    scale = HEAD_DIM ** -0.5 * math.log2(math.e)
    gain = jnp.concatenate([jnp.tile(ab_q_norm[0] * scale, N_HEADS), jnp.tile(ab_k_norm[0], N_KV_HEADS)])
    score_bound = (1.05 * HEAD_DIM * scale * jnp.max(jnp.abs(ab_q_norm[0])) * jnp.max(jnp.abs(ab_k_norm[0])))
    score_bound = score_bound.reshape(1, 1).astype(F32)
    hid = np.arange(QK_WIDTH) // HEAD_DIM
    hmean = jnp.asarray((hid[:, None] == hid[None, :]).astype(np.float32) / HEAD_DIM, BF16)
    w_out = ab_w_out[0]
    w_out_a = w_out[:ATTN_WIDTH][slot_cols].astype(BF16)
    w_out_s = w_out[ATTN_WIDTH:].astype(BF16)
    w_glu = ssm_w_glu[0].astype(BF16)
    tabs = _ssm_tables(*_ssm_param_layout(
        ssm_a_re_f[0], ssm_a_im_f[0], ssm_log_step_f[0], ssm_a_re_b[0], ssm_a_im_b[0], ssm_log_step_b[0],
        ssm_b_re[0], ssm_b_im[0], ssm_c_re[0], ssm_c_im[0], ssm_d[0]), *_selectors())

    attn, zs, segments = [], [], []
    r0 = 0
    for x in xs:
        b, seq_len, _ = x.shape
        cos_t, sin_t = _rope_tables(seq_len)
        q, k, v, u = _inproj(x.reshape(b * seq_len, D_MODEL), w_in, row2(gain), cos_t, sin_t, hmean, seq_len)
        attn.append(_attention(score_bound, q.reshape(b, seq_len, ATTN_WIDTH), k.reshape(b, seq_len, KV_WIDTH),
                               v.reshape(b, seq_len, KV_WIDTH)).reshape(b * seq_len, ATTN_WIDTH))
        nc = seq_len // SSM_CHUNK
        z = u.reshape(b * nc, SSM_CHUNK, SSM_GROUPS, SSM_GROUP)
        zs.append(jnp.transpose(z, (2, 0, 1, 3)).reshape(SSM_GROUPS, b * nc, SSM_ROW))
        segments.append((r0, b * nc, nc))
        r0 += b * nc
    y_all = _ssm_apply(jnp.concatenate(zs, axis=1), *tabs, tuple(segments))

    outs = []
    for x, a, (r0, rows, nc) in zip(xs, attn, segments):
        b, seq_len, _ = x.shape
        n = b * seq_len
        y = y_all[:, r0:r0 + rows].reshape(SSM_GROUPS, rows, SSM_CHUNK, SSM_GROUP)
        y = jnp.transpose(y, (1, 2, 0, 3)).reshape(n, SSM_WIDTH)
        h = _mix0(x.reshape(n, D_MODEL), a, y, w_glu, row2(ssm_b_glu[0]), w_out_a, w_out_s,
                  row2(ln_mix_g[0]), row2(ln_mix_b[0]))
        h = _ffn(h, ff_w1[0].astype(BF16), ff_w2[0].astype(BF16), row2(ln_ff_g[0]), row2(ln_ff_b[0]))
        bs_full = jnp.broadcast_to(c_b_s[0][:, :, None], (SGU_HEADS, SGU_CHUNK, LANES))
        h = _sgu(h, c_w_in[0].astype(BF16), row2(c_ln_g[0]), row2(c_ln_b[0]), c_w_s[0].astype(BF16), bs_full,
                 c_w_out[0].astype(BF16), row2(ln_mix_g[1]), row2(ln_mix_b[1]))
        h = _ffn(h, ff_w1[1].astype(BF16), ff_w2[1].astype(BF16), row2(ln_ff_g[1]), row2(ln_ff_b[1]))
        outs.append(h.reshape(b, seq_len, D_MODEL))
    return tuple(outs)
```

```python
import functools
import math

import numpy as np
import jax
import jax.numpy as jnp
from jax import lax
from jax.experimental import pallas as pl
from jax.experimental.pallas import tpu as pltpu

F32 = jnp.float32
BF16 = jnp.bfloat16

D_MODEL = 1024
GRID_W = 64
N_HEADS = 8
N_KV_HEADS = 2
HEAD_DIM = 64
ATTN_WIDTH = N_HEADS * HEAD_DIM
KV_WIDTH = N_KV_HEADS * HEAD_DIM
QK_WIDTH = ATTN_WIDTH + KV_WIDTH
Q_BLOCK = 128
ROPE_THETA = 10000.0
SSM_WIDTH = D_MODEL - ATTN_WIDTH
SSM_GROUP = 16
SSM_GROUPS = SSM_WIDTH // SSM_GROUP
SSM_STATE = 64
SGU_HEADS = 8
SGU_CHUNK = 128
D_FF = 4 * D_MODEL
DEPTH = 2
AB_IN = ATTN_WIDTH + 2 * KV_WIDTH + SSM_WIDTH
ALPHA = (2 * DEPTH) ** 0.25
EPS = 1e-6

LANES = 128
VMEM_LIMIT_BYTES = 56 * 1024 * 1024

SSM_CHUNK = 64
SSM_ROW = SSM_CHUNK * SSM_GROUP
SSM_LAGS = 2 * SSM_CHUNK
SSM_TAPS = SSM_LAGS * SSM_GROUP
SSM_SDIM = 4 * SSM_STATE

TOKEN_TILE = 512


def _cparams(sem):
    return pltpu.CompilerParams(dimension_semantics=sem, vmem_limit_bytes=VMEM_LIMIT_BYTES)


def _const_spec(shape):
    nd = len(shape)
    return pl.BlockSpec(shape, lambda *_: (0,) * nd, pipeline_mode=pl.Buffered(1))


def _layer_norm(r, g, b):
    mu = jnp.mean(r, axis=-1, keepdims=True)
    d = r - mu
    var = jnp.mean(d * d, axis=-1, keepdims=True)
    return d * lax.rsqrt(var + EPS) * g + b


def _split_bf16(x):
    hi = x.astype(BF16)
    lo = (x - hi.astype(F32)).astype(BF16)
    return hi, lo


def _select_cols(x, e):
    hi, lo = _split_bf16(x)
    y = jnp.dot(jnp.concatenate([hi, lo], axis=0), e, preferred_element_type=F32)
    return y[:x.shape[0]] + y[x.shape[0]:]


def _select_rows(e, x):
    hi, lo = _split_bf16(x)
    y = jnp.dot(e, jnp.concatenate([hi, lo], axis=1), preferred_element_type=F32)
    return y[:, :x.shape[1]] + y[:, x.shape[1]:]


def _inproj_kernel(x_ref, w_ref, gain_ref, cos_ref, sin_ref, hmean_ref, q_ref, k_ref, v_ref, u_ref):
    xb = x_ref[...].astype(BF16)
    h = jnp.dot(xb, w_ref[...], preferred_element_type=F32)
    qk = h[:, :QK_WIDTH]
    ms = jnp.dot((qk * qk).astype(BF16), hmean_ref[...], preferred_element_type=F32)
    qkn = qk * lax.rsqrt(ms + EPS) * gain_ref[...]
    c = cos_ref[...]
    s = sin_ref[...]
    even = (lax.broadcasted_iota(jnp.int32, c.shape, 1) % 2) == 0
    outs = []
    for j in range(QK_WIDTH // LANES):
        xs = qkn[:, j * LANES:(j + 1) * LANES]
        nxt = pltpu.roll(xs, LANES - 1, 1)
        prv = pltpu.roll(xs, 1, 1)
        outs.append(xs * c + jnp.where(even, nxt, prv) * s)
    for j in range(ATTN_WIDTH // LANES):
        q_ref[:, j * LANES:(j + 1) * LANES] = outs[j].astype(BF16)
    k_ref[...] = outs[ATTN_WIDTH // LANES].astype(BF16)
    v_ref[...] = h[:, QK_WIDTH:QK_WIDTH + KV_WIDTH].astype(BF16)
    u_ref[...] = h[:, QK_WIDTH + KV_WIDTH:].astype(BF16)


def _inproj(x2, w_in, gain, cos_t, sin_t, hmean, seq_len):
    n = x2.shape[0]
    tm = TOKEN_TILE
    nper = seq_len // tm
    return pl.pallas_call(
        _inproj_kernel,
        grid=(n // tm,),
        in_specs=[
            pl.BlockSpec((tm, D_MODEL), lambda i: (i, 0)),
            _const_spec((D_MODEL, AB_IN)),
            _const_spec((1, QK_WIDTH)),
            pl.BlockSpec((tm, LANES), lambda i: (i % nper, 0)),
            pl.BlockSpec((tm, LANES), lambda i: (i % nper, 0)),
            _const_spec((QK_WIDTH, QK_WIDTH)),
        ],
        out_specs=[
            pl.BlockSpec((tm, ATTN_WIDTH), lambda i: (i, 0)),
            pl.BlockSpec((tm, KV_WIDTH), lambda i: (i, 0)),
            pl.BlockSpec((tm, KV_WIDTH), lambda i: (i, 0)),
            pl.BlockSpec((tm, SSM_WIDTH), lambda i: (i, 0)),
        ],
        out_shape=[
            jax.ShapeDtypeStruct((n, ATTN_WIDTH), BF16),
            jax.ShapeDtypeStruct((n, KV_WIDTH), BF16),
            jax.ShapeDtypeStruct((n, KV_WIDTH), BF16),
            jax.ShapeDtypeStruct((n, SSM_WIDTH), BF16),
        ],
        compiler_params=_cparams(("parallel",)),
        name="inproj",
    )(x2, w_in, gain, cos_t, sin_t, hmean)


ATT_UNIT_Q = 64


ATT_SHIFT_LIMIT = 60.0


def _attn_kernel(bound_ref, q_ref, k_ref, v_ref, o_ref,
                 s0_ref, s1_ref, p0_ref, p1_ref, m0_ref, m1_ref, l0_ref, l1_ref):
    seq_len = k_ref.shape[1]
    n_units = seq_len // ATT_UNIT_Q
    rep = N_HEADS // N_KV_HEADS
    lane = lax.broadcasted_iota(jnp.int32, (ATT_UNIT_Q, LANES), 1)
    lo_half = lane < HEAD_DIM
    zero = jnp.zeros((ATT_UNIT_Q, LANES), BF16)
    bound = bound_ref[0, 0]

    def rows_of(t):
        return pl.ds(pl.multiple_of(t * ATT_UNIT_Q, ATT_UNIT_Q), ATT_UNIT_Q)

    def raw_scores(t):
        q = q_ref[0, rows_of(t), :]
        slots = [q[:, m * LANES:(m + 1) * LANES] for m in range(rep)]
        qs = jnp.concatenate([jnp.where(lo_half, x, zero) for x in slots]
                             + [jnp.where(lo_half, zero, x) for x in slots], axis=0)
        return lax.dot_general(qs, k_ref[0], (((1,), (1,)), ((), ())), preferred_element_type=F32)

    def scores(t, s_ref, m_ref):
        s = raw_scores(t)
        s_ref[...] = s
        m_ref[...] = jnp.max(s, axis=-1, keepdims=True)

    def probs(s_ref, m_ref, p_ref, l_ref):
        p = jnp.exp2(s_ref[...] - m_ref[...])
        l_ref[...] = jnp.sum(p, axis=-1, keepdims=True)
        p_ref[...] = p.astype(BF16)

    def bounded_probs(t, p_ref, l_ref):
        p = jnp.exp2(raw_scores(t) - bound)
        l_ref[...] = jnp.sum(p, axis=-1, keepdims=True)
        p_ref[...] = p.astype(BF16)

    def output(t, p_ref, l_ref):
        o = jnp.dot(p_ref[...], v_ref[0], preferred_element_type=F32) / l_ref[...]
        for m in range(rep):
            g0 = o[m * ATT_UNIT_Q:(m + 1) * ATT_UNIT_Q]
            g1 = o[(rep + m) * ATT_UNIT_Q:(rep + m + 1) * ATT_UNIT_Q]
            o_ref[0, rows_of(t), m * LANES:(m + 1) * LANES] = jnp.where(lo_half, g0, g1).astype(BF16)

    @pl.when(bound <= ATT_SHIFT_LIMIT)
    def _bounded_shift():
        bounded_probs(0, p0_ref, l0_ref)

        def body(i, carry):
            t = 2 * i + 1
            bounded_probs(t, p1_ref, l1_ref)
            output(t - 1, p0_ref, l0_ref)
            bounded_probs(t + 1, p0_ref, l0_ref)
            output(t, p1_ref, l1_ref)
            return carry

        lax.fori_loop(0, (n_units - 2) // 2, body, 0)
        bounded_probs(n_units - 1, p1_ref, l1_ref)
        output(n_units - 2, p0_ref, l0_ref)
        output(n_units - 1, p1_ref, l1_ref)

    @pl.when(bound > ATT_SHIFT_LIMIT)
    def _row_max_shift():
        scores(0, s0_ref, m0_ref)
        scores(1, s1_ref, m1_ref)
        probs(s0_ref, m0_ref, p0_ref, l0_ref)

        def body(i, carry):
            t = 2 * i + 2
            scores(t, s0_ref, m0_ref)
            probs(s1_ref, m1_ref, p1_ref, l1_ref)
            output(t - 2, p0_ref, l0_ref)
            scores(t + 1, s1_ref, m1_ref)
            probs(s0_ref, m0_ref, p0_ref, l0_ref)
            output(t - 1, p1_ref, l1_ref)
            return carry

        lax.fori_loop(0, (n_units - 2) // 2, body, 0)
        probs(s1_ref, m1_ref, p1_ref, l1_ref)
        output(n_units - 2, p0_ref, l0_ref)
        output(n_units - 1, p1_ref, l1_ref)


def _attention(bound, q, k, v):
    b, seq_len, _ = q.shape
    rows = ATT_UNIT_Q * N_HEADS
    seq = lambda w, **kw: pl.BlockSpec((1, seq_len, w), lambda i: (i, 0, 0), **kw)
    once = dict(pipeline_mode=pl.Buffered(1))
    return pl.pallas_call(
        _attn_kernel,
        grid=(b,),
        in_specs=[pl.BlockSpec(memory_space=pltpu.SMEM),
                  seq(ATTN_WIDTH, **once), seq(KV_WIDTH, **once), seq(KV_WIDTH, **once)],
        out_specs=seq(ATTN_WIDTH),
        out_shape=jax.ShapeDtypeStruct((b, seq_len, ATTN_WIDTH), BF16),
        scratch_shapes=[pltpu.VMEM((rows, seq_len), F32), pltpu.VMEM((rows, seq_len), F32),
                        pltpu.VMEM((rows, seq_len), BF16), pltpu.VMEM((rows, seq_len), BF16),
                        pltpu.VMEM((rows, 1), F32), pltpu.VMEM((rows, 1), F32),
                        pltpu.VMEM((rows, 1), F32), pltpu.VMEM((rows, 1), F32)],
        compiler_params=_cparams(("parallel",)),
        name="attention",
    )(bound, q, k, v)


def _ssm_table_kernel(pc_ref, pr_ref, pr2_ref, bret_ref, bimt_ref, ctr_ref, cti_ref, dt_ref, e16_ref, x16_ref,
                      m_ref, p_ref, q_ref, sca_ref, scb_ref):
    ns = SSM_STATE
    pc = pc_ref[0]
    are_c, aim_c, dt_c = pc[:, 0:1], pc[:, 1:2], jnp.exp(pc[:, 2:3])
    row = lax.broadcasted_iota(jnp.int32, (2 * ns, LANES), 0)
    lan = lax.broadcasted_iota(jnp.int32, (2 * ns, LANES), 1)
    n_g = jnp.where(row >= ns, (SSM_CHUNK - 1) - lan, lan - (SSM_CHUNK - 1))
    mask = n_g >= 0
    nf = jnp.where(mask, n_g, 0).astype(F32)
    rho = jnp.exp(are_c * dt_c * nf)
    ang = aim_c * dt_c * nf
    cs = jnp.where(mask, rho * jnp.cos(ang), 0.0)
    sn = jnp.where(mask, rho * jnp.sin(ang), 0.0)
    base = jnp.concatenate([cs[:ns], sn[:ns], cs[ns:], sn[ns:]], axis=0)
    ex = _select_cols(base, e16_ref[...])
    ecf, esf, ecb, esb = ex[:ns], ex[ns:2 * ns], ex[2 * ns:3 * ns], ex[3 * ns:]

    def tile_lanes(x, width):
        for sh in (SSM_GROUP, 2 * SSM_GROUP, 4 * SSM_GROUP):
            x = x + pltpu.roll(x, sh, 1)
        return jnp.concatenate([x] * (width // LANES), axis=1)

    crt = tile_lanes(ctr_ref[0], SSM_TAPS)
    cit = tile_lanes(cti_ref[0], SSM_TAPS)
    g_fre = ecf * crt - esf * cit
    g_fim = -(esf * crt + ecf * cit)
    g_bre = ecb * crt - esb * cit
    g_bim = -(esb * crt + ecb * cit)
    g_tab = jnp.concatenate([g_fre, g_fim, g_bre, g_bim], axis=0)

    ab_r = cs[ns:, SSM_CHUNK - 2:SSM_CHUNK - 1]
    ab_i = sn[ns:, SSM_CHUNK - 2:SSM_CHUNK - 1]
    q_tab = jnp.concatenate([
        g_fre[:, SSM_ROW:], g_fim[:, SSM_ROW:],
        ab_r * g_bre[:, :SSM_ROW] + ab_i * g_bim[:, :SSM_ROW],
        ab_r * g_bim[:, :SSM_ROW] - ab_i * g_bre[:, :SSM_ROW]], axis=0)
    q_ref[0] = q_tab.astype(BF16)

    pr = pr_ref[0]
    are_r, aim_r, ls_r = pr[0:1, :], pr[1:2, :], pr[2:3, :]
    dt_r = jnp.exp(ls_r)
    rho1 = jnp.exp(are_r * dt_r)
    th1 = aim_r * dt_r
    abr = rho1 * jnp.cos(th1) - 1.0
    abi = rho1 * jnp.sin(th1)
    den = are_r * are_r + aim_r * aim_r
    cr = (abr * are_r + abi * aim_r) / den
    ci = (abi * are_r - abr * aim_r) / den
    bret = bret_ref[0]
    bimt = bimt_ref[0]
    bbr = cr * bret - ci * bimt
    bbi = cr * bimt + ci * bret
    lan_r = lax.broadcasted_iota(jnp.int32, (SSM_GROUP, SSM_SDIM), 1)
    im_lane = ((lan_r // ns) % 2) == 1
    bcat = jnp.where(im_lane, bbi, bbr)

    bh, bl = _split_bf16(bcat)
    gh, gl = _split_bf16(g_tab)
    w = (jnp.dot(bh, gh, preferred_element_type=F32) + jnp.dot(bh, gl, preferred_element_type=F32)
         + jnp.dot(bl, gh, preferred_element_type=F32))
    wrow = lax.broadcasted_iota(jnp.int32, (SSM_GROUP, SSM_TAPS), 0)
    wcol = lax.broadcasted_iota(jnp.int32, (SSM_GROUP, SSM_TAPS), 1)
    on_diag = (wcol // SSM_GROUP == SSM_CHUNK - 1) & (wcol % SSM_GROUP == wrow)
    w = w + jnp.where(on_diag, dt_ref[0], 0.0)

    per_vreg = LANES // SSM_GROUP
    rolled = [w] + [pltpu.roll(w, SSM_TAPS - SSM_GROUP * kk, 1) for kk in range(1, per_vreg)]
    for i in range(SSM_CHUNK):
        r = SSM_CHUNK - 1 - i
        blk = rolled[r % per_vreg][:, (r // per_vreg) * LANES:(r // per_vreg) * LANES + SSM_ROW]
        m_ref[0, i * SSM_GROUP:(i + 1) * SSM_GROUP, :] = blk.astype(BF16)

    pr2 = pr2_ref[0]
    i_r = lax.broadcasted_iota(jnp.int32, (SSM_CHUNK, LANES), 0)
    l_r = lax.broadcasted_iota(jnp.int32, (SSM_CHUNK, LANES), 1)
    lo_r = l_r < ns
    n_p = jnp.where(lo_r, (SSM_CHUNK - 1) - i_r, i_r).astype(F32)
    dt2 = jnp.exp(pr2[2:3, :])
    rho = jnp.exp(pr2[0:1, :] * dt2 * n_p)
    ang = pr2[1:2, :] * dt2 * n_p
    c = rho * jnp.cos(ang)
    s = rho * jnp.sin(ang)
    base_p = jnp.concatenate([jnp.where(lo_r, c, pltpu.roll(s, ns, 1)),
                              jnp.where(lo_r, pltpu.roll(c, ns, 1), s)], axis=1)
    px = _select_rows(x16_ref[...], base_p)
    sign = jnp.where(lax.broadcasted_iota(jnp.int32, (SSM_ROW, LANES), 1) < ns, -1.0, 1.0)
    px2 = jnp.concatenate([pltpu.roll(px[:, :LANES], ns, 1) * sign,
                           pltpu.roll(px[:, LANES:], ns, 1) * sign], axis=1)
    bt_r = jnp.concatenate([bbr] * SSM_CHUNK, axis=0)
    bt_i = jnp.concatenate([bbi] * SSM_CHUNK, axis=0)
    p_ref[0] = (px * bt_r + px2 * bt_i).astype(BF16)

    k_r = lax.broadcasted_iota(jnp.int32, (8, SSM_SDIM), 0)
    l8 = lax.broadcasted_iota(jnp.int32, (8, SSM_SDIM), 1)
    n_s = (SSM_CHUNK * jnp.left_shift(1, k_r)).astype(F32)
    rho = jnp.exp(are_r * dt_r * n_s)
    ang = aim_r * dt_r * n_s
    sca_ref[0] = rho * jnp.cos(ang)
    scb_ref[0] = rho * jnp.sin(ang) * jnp.where(((l8 // ns) % 2) == 1, 1.0, -1.0)


def _ssm_tables(pc, pr, pr2, bret, bimt, ctr, cti, dtile, e16, x16):
    g = SSM_GROUPS
    grp = lambda shape: pl.BlockSpec((1,) + shape, lambda i: (i, 0, 0))
    return pl.pallas_call(
        _ssm_table_kernel,
        grid=(g,),
        in_specs=[
            grp((2 * SSM_STATE, 8)), grp((8, SSM_SDIM)), grp((8, LANES)),
            grp((SSM_GROUP, SSM_SDIM)), grp((SSM_GROUP, SSM_SDIM)),
            grp((SSM_STATE, LANES)), grp((SSM_STATE, LANES)), grp((1, SSM_TAPS)),
            _const_spec((LANES, SSM_TAPS)), _const_spec((SSM_ROW, SSM_CHUNK)),
        ],
        out_specs=[
            grp((SSM_ROW, SSM_ROW)), grp((SSM_ROW, SSM_SDIM)), grp((SSM_SDIM, SSM_ROW)),
            grp((8, SSM_SDIM)), grp((8, SSM_SDIM)),
        ],
        out_shape=[
            jax.ShapeDtypeStruct((g, SSM_ROW, SSM_ROW), BF16),
            jax.ShapeDtypeStruct((g, SSM_ROW, SSM_SDIM), BF16),
            jax.ShapeDtypeStruct((g, SSM_SDIM, SSM_ROW), BF16),
            jax.ShapeDtypeStruct((g, 8, SSM_SDIM), F32),
            jax.ShapeDtypeStruct((g, 8, SSM_SDIM), F32),
        ],
        compiler_params=_cparams(("parallel",)),
        name="ssm_tables",
    )(pc, pr, pr2, bret, bimt, ctr, cti, dtile, e16, x16)


def _ssm_apply_kernel(z_ref, m_ref, p_ref, q_ref, sca_ref, scb_ref, y_ref, *, segments):
    z = z_ref[0]
    y = jnp.dot(z, m_ref[0], preferred_element_type=F32)
    st = jnp.dot(z, p_ref[0], preferred_element_type=F32)
    sca = sca_ref[0]
    scb = scb_ref[0]
    carried = []
    for (r0, rows, nc) in segments:
        c_idx = lax.broadcasted_iota(jnp.int32, (rows, LANES), 0) % nc
        xf = st[r0:r0 + rows, :LANES]
        xb = st[r0:r0 + rows, LANES:]
        for kk in range(int(math.log2(nc))):
            d = 1 << kk
            sh = jnp.where(c_idx >= d, pltpu.roll(xf, d, 0), 0.0)
            xf = xf + sh * sca[kk:kk + 1, :LANES] + pltpu.roll(sh, SSM_STATE, 1) * scb[kk:kk + 1, :LANES]
            sh = jnp.where(c_idx < nc - d, pltpu.roll(xb, rows - d, 0), 0.0)
            xb = xb + sh * sca[kk:kk + 1, LANES:] + pltpu.roll(sh, SSM_STATE, 1) * scb[kk:kk + 1, LANES:]
        prev_f = jnp.where(c_idx >= 1, pltpu.roll(xf, 1, 0), 0.0)
        next_b = jnp.where(c_idx < nc - 1, pltpu.roll(xb, rows - 1, 0), 0.0)
        carried.append(jnp.concatenate([prev_f, next_b], axis=1))
    xin = jnp.concatenate(carried, axis=0).astype(BF16)
    y_ref[0] = (y + jnp.dot(xin, q_ref[0], preferred_element_type=F32)).astype(y_ref.dtype)


def _ssm_apply(z, m_tab, p_tab, q_tab, sca, scb, segments):
    g, rows, _ = z.shape
    grp = lambda shape: pl.BlockSpec((1,) + shape, lambda i: (i, 0, 0))
    return pl.pallas_call(
        functools.partial(_ssm_apply_kernel, segments=segments),
        grid=(g,),
        in_specs=[grp((rows, SSM_ROW)), grp((SSM_ROW, SSM_ROW)), grp((SSM_ROW, SSM_SDIM)),
                  grp((SSM_SDIM, SSM_ROW)), grp((8, SSM_SDIM)), grp((8, SSM_SDIM))],
        out_specs=grp((rows, SSM_ROW)),
        out_shape=jax.ShapeDtypeStruct((g, rows, SSM_ROW), BF16),
        compiler_params=_cparams(("parallel",)),
        name="ssm_apply",
    )(z, m_tab, p_tab, q_tab, sca, scb)


FF_CHUNK = 1024


def _ffn_ln(x, w1_ref, w2_ref, g, b):
    xb = x.astype(BF16)
    acc = ALPHA * x
    for c in range(D_FF // FF_CHUNK):
        h = jnp.dot(xb, w1_ref[0, :, c * FF_CHUNK:(c + 1) * FF_CHUNK], preferred_element_type=F32)
        h = jnp.maximum(h, 0.0)
        acc = acc + jnp.dot((h * h).astype(BF16), w2_ref[0, c * FF_CHUNK:(c + 1) * FF_CHUNK, :],
                            preferred_element_type=F32)
    return _layer_norm(acc, g, b)


def _layer0_tail_kernel(x_ref, a_ref, y_ref, wg_ref, bg_ref, woa_ref, wos_ref, ln_ref, w1_ref, w2_ref, o_ref):
    y = jax.nn.gelu(y_ref[...].astype(F32))
    gate = jnp.dot(y.astype(BF16), wg_ref[...], preferred_element_type=F32) + bg_ref[...]
    zs = (y * jax.nn.sigmoid(gate)).astype(BF16)
    mix = (jnp.dot(a_ref[...], woa_ref[...], preferred_element_type=F32)
           + jnp.dot(zs, wos_ref[...], preferred_element_type=F32))
    ln = ln_ref[...]
    x1 = _layer_norm(ALPHA * x_ref[...] + mix, ln[0:1], ln[1:2])
    o_ref[...] = _ffn_ln(x1, w1_ref, w2_ref, ln[2:3], ln[3:4])


def _layer_weights_spec(shape, layer):
    return pl.BlockSpec((1,) + shape, lambda i: (layer, 0, 0), pipeline_mode=pl.Buffered(1))


def _layer0_tail(x2, attn, y, w_glu, b_glu, w_out_a, w_out_s, ln4, w1, w2):
    n = x2.shape[0]
    tm = TOKEN_TILE
    tok = lambda w: pl.BlockSpec((tm, w), lambda i: (i, 0))
    return pl.pallas_call(
        _layer0_tail_kernel,
        grid=(n // tm,),
        in_specs=[tok(D_MODEL), tok(ATTN_WIDTH), tok(SSM_WIDTH),
                  _const_spec((SSM_WIDTH, SSM_WIDTH)), _const_spec((1, SSM_WIDTH)),
                  _const_spec((ATTN_WIDTH, D_MODEL)), _const_spec((SSM_WIDTH, D_MODEL)),
                  _const_spec((4, D_MODEL)),
                  _layer_weights_spec((D_MODEL, D_FF), 0), _layer_weights_spec((D_FF, D_MODEL), 0)],
        out_specs=tok(D_MODEL),
        out_shape=jax.ShapeDtypeStruct((n, D_MODEL), F32),
        compiler_params=_cparams(("parallel",)),
        name="layer0_tail",
    )(x2, attn, y, w_glu, b_glu, w_out_a, w_out_s, ln4, w1, w2)


def _layer1_kernel(x_ref, win_ref, cln_ref, ws_ref, bs_ref, wout_ref, ln_ref, w1_ref, w2_ref, o_ref):
    x = x_ref[...]
    tm = x.shape[0]
    h = jax.nn.gelu(jnp.dot(x.astype(BF16), win_ref[...], preferred_element_type=F32))
    u = h[:, :D_MODEL]
    cln = cln_ref[...]
    vb = _layer_norm(h[:, D_MODEL:], cln[0:1], cln[1:2]).astype(BF16)
    rows = []
    for c in range(tm // SGU_CHUNK):
        cols = []
        for hd in range(SGU_HEADS):
            blk = vb[c * SGU_CHUNK:(c + 1) * SGU_CHUNK, hd * LANES:(hd + 1) * LANES]
            cols.append(jnp.dot(ws_ref[hd], blk, preferred_element_type=F32) + bs_ref[hd])
        rows.append(jnp.concatenate(cols, axis=1))
    mixed = jnp.concatenate(rows, axis=0)
    out = jnp.dot((u * mixed).astype(BF16), wout_ref[...], preferred_element_type=F32)
    ln = ln_ref[...]
    x1 = _layer_norm(ALPHA * x + out, ln[0:1], ln[1:2])
    o_ref[...] = _ffn_ln(x1, w1_ref, w2_ref, ln[2:3], ln[3:4])


def _layer1(x2, w_in, cln2, w_s, b_s_full, w_out, ln4, w1, w2):
    n = x2.shape[0]
    tm = TOKEN_TILE
    tok = pl.BlockSpec((tm, D_MODEL), lambda i: (i, 0))
    return pl.pallas_call(
        _layer1_kernel,
        grid=(n // tm,),
        in_specs=[tok, _const_spec((D_MODEL, 2 * D_MODEL)), _const_spec((2, D_MODEL)),
                  _const_spec((SGU_HEADS, SGU_CHUNK, SGU_CHUNK)), _const_spec((SGU_HEADS, SGU_CHUNK, LANES)),
                  _const_spec((D_MODEL, D_MODEL)), _const_spec((4, D_MODEL)),
                  _layer_weights_spec((D_MODEL, D_FF), 1), _layer_weights_spec((D_FF, D_MODEL), 1)],
        out_specs=tok,
        out_shape=jax.ShapeDtypeStruct((n, D_MODEL), F32),
        compiler_params=_cparams(("parallel",)),
        name="layer1",
    )(x2, w_in, cln2, w_s, b_s_full, w_out, ln4, w1, w2)


_HEAD_OF_SLOT = np.array([0, 4, 1, 5, 2, 6, 3, 7])


def _rope_tables(seq_len):
    t = np.arange(seq_len)
    row = (t // GRID_W).astype(np.float32)
    col = (t % GRID_W).astype(np.float32)
    axis_dim = HEAD_DIM // 2
    inv_freq = ROPE_THETA ** (-jnp.arange(0, axis_dim, 2, dtype=F32) / axis_dim)
    ang = jnp.concatenate([jnp.asarray(row)[:, None] * inv_freq, jnp.asarray(col)[:, None] * inv_freq], axis=-1)
    cos = jnp.repeat(jnp.cos(ang), 2, axis=-1)
    sin = jnp.repeat(jnp.sin(ang), 2, axis=-1)
    sign = jnp.tile(jnp.asarray([-1.0, 1.0], F32), HEAD_DIM // 2)
    reps = LANES // HEAD_DIM
    return jnp.tile(cos, (1, reps)), jnp.tile(sin * sign, (1, reps))


def _selectors():
    col = np.arange(SSM_TAPS)
    row = np.arange(LANES)
    e16 = (col[None, :] // SSM_GROUP == row[:, None])
    r2 = np.arange(SSM_ROW)
    x16 = (r2[:, None] // SSM_GROUP == np.arange(SSM_CHUNK)[None, :])
    as_bf16 = lambda a: jnp.asarray(a.astype(np.float32), BF16)
    return as_bf16(e16), as_bf16(x16)


def _ssm_param_layout(a_re_f, a_im_f, ls_f, a_re_b, a_im_b, ls_b, b_re, b_im, c_re, c_im, d):
    g, ns = SSM_GROUPS, SSM_STATE
    lsf = jnp.broadcast_to(ls_f[:, None], (g, ns))
    lsb = jnp.broadcast_to(ls_b[:, None], (g, ns))
    zeros = jnp.zeros((g, ns), F32)
    per_f = jnp.stack([a_re_f, a_im_f, lsf] + [zeros] * 5, axis=1)
    per_b = jnp.stack([a_re_b, a_im_b, lsb] + [zeros] * 5, axis=1)
    pr = jnp.concatenate([per_f, per_f, per_b, per_b], axis=2)
    pr2 = jnp.concatenate([per_f, per_b], axis=2)
    pc = jnp.transpose(pr2, (0, 2, 1))
    def rows_t(b):
        bt = jnp.transpose(b, (0, 2, 1))
        return jnp.concatenate([bt, bt, bt, bt], axis=2)
    def cols_t(c):
        return jnp.pad(jnp.transpose(c, (0, 2, 1)), ((0, 0), (0, 0), (0, LANES - SSM_GROUP)))
    dtile = jnp.tile(d.reshape(g, 1, SSM_GROUP), (1, 1, SSM_LAGS))
    return pc, pr, pr2, rows_t(b_re), rows_t(b_im), cols_t(c_re), cols_t(c_im), dtile


def kernel(x_prompt, x_sample, ab_w_in, ab_q_norm, ab_k_norm, ssm_a_re_f, ssm_a_im_f, ssm_log_step_f,
           ssm_a_re_b, ssm_a_im_b, ssm_log_step_b, ssm_b_re, ssm_b_im, ssm_c_re, ssm_c_im, ssm_d, ssm_w_glu,
           ssm_b_glu, ab_w_out, c_w_in, c_ln_g, c_ln_b, c_w_s, c_b_s, c_w_out, ln_mix_g, ln_mix_b, ff_w1, ff_w2,
           ln_ff_g, ln_ff_b):
    xs = [x_prompt, x_sample]
    row2 = lambda v: v.reshape(1, -1)

    slot_cols = (_HEAD_OF_SLOT[:, None] * HEAD_DIM + np.arange(HEAD_DIM)[None, :]).reshape(-1)
    w_in = ab_w_in[0]
    w_in = jnp.concatenate([w_in[:, slot_cols], w_in[:, ATTN_WIDTH:]], axis=1).astype(BF16)
    scale = HEAD_DIM ** -0.5 * math.log2(math.e)
    gain = jnp.concatenate([jnp.tile(ab_q_norm[0] * scale, N_HEADS), jnp.tile(ab_k_norm[0], N_KV_HEADS)])
    score_bound = (1.05 * HEAD_DIM * scale * jnp.max(jnp.abs(ab_q_norm[0])) * jnp.max(jnp.abs(ab_k_norm[0])))
    score_bound = score_bound.reshape(1, 1).astype(F32)
    hid = np.arange(QK_WIDTH) // HEAD_DIM
    hmean = jnp.asarray((hid[:, None] == hid[None, :]).astype(np.float32) / HEAD_DIM, BF16)
    w_out = ab_w_out[0]
    w_out_a = w_out[:ATTN_WIDTH][slot_cols].astype(BF16)
    w_out_s = w_out[ATTN_WIDTH:].astype(BF16)
    w_glu = ssm_w_glu[0].astype(BF16)
    tabs = _ssm_tables(*_ssm_param_layout(
        ssm_a_re_f[0], ssm_a_im_f[0], ssm_log_step_f[0], ssm_a_re_b[0], ssm_a_im_b[0], ssm_log_step_b[0],
        ssm_b_re[0], ssm_b_im[0], ssm_c_re[0], ssm_c_im[0], ssm_d[0]), *_selectors())

    attn, zs, segments = [], [], []
    r0 = 0
    for x in xs:
        b, seq_len, _ = x.shape
        cos_t, sin_t = _rope_tables(seq_len)
        q, k, v, u = _inproj(x.reshape(b * seq_len, D_MODEL), w_in, row2(gain), cos_t, sin_t, hmean, seq_len)
        attn.append(_attention(score_bound, q.reshape(b, seq_len, ATTN_WIDTH), k.reshape(b, seq_len, KV_WIDTH),
                               v.reshape(b, seq_len, KV_WIDTH)).reshape(b * seq_len, ATTN_WIDTH))
        nc = seq_len // SSM_CHUNK
        z = u.reshape(b * nc, SSM_CHUNK, SSM_GROUPS, SSM_GROUP)
        zs.append(jnp.transpose(z, (2, 0, 1, 3)).reshape(SSM_GROUPS, b * nc, SSM_ROW))
        segments.append((r0, b * nc, nc))
        r0 += b * nc
    y_all = _ssm_apply(jnp.concatenate(zs, axis=1), *tabs, tuple(segments))

    ff_w1b = ff_w1.astype(BF16)
    ff_w2b = ff_w2.astype(BF16)
    ln4 = [jnp.stack([ln_mix_g[i], ln_mix_b[i], ln_ff_g[i], ln_ff_b[i]]) for i in range(DEPTH)]
    cln2 = jnp.stack([c_ln_g[0], c_ln_b[0]])
    bs_full = jnp.broadcast_to(c_b_s[0][:, :, None], (SGU_HEADS, SGU_CHUNK, LANES))
    c_w_in_b, c_w_s_b, c_w_out_b = c_w_in[0].astype(BF16), c_w_s[0].astype(BF16), c_w_out[0].astype(BF16)
    outs = []
    for x, a, (r0, rows, nc) in zip(xs, attn, segments):
        b, seq_len, _ = x.shape
        n = b * seq_len
        y = y_all[:, r0:r0 + rows].reshape(SSM_GROUPS, rows, SSM_CHUNK, SSM_GROUP)
        y = jnp.transpose(y, (1, 2, 0, 3)).reshape(n, SSM_WIDTH)
        h = _layer0_tail(x.reshape(n, D_MODEL), a, y, w_glu, row2(ssm_b_glu[0]), w_out_a, w_out_s,
                         ln4[0], ff_w1b, ff_w2b)
        h = _layer1(h, c_w_in_b, cln2, c_w_s_b, bs_full, c_w_out_b, ln4[1], ff_w1b, ff_w2b)
        outs.append(h.reshape(b, seq_len, D_MODEL))
    return tuple(outs)
```

```python
import functools
import math

import numpy as np
import jax
import jax.numpy as jnp
from jax import lax
from jax.experimental import pallas as pl
from jax.experimental.pallas import tpu as pltpu

F32 = jnp.float32
BF16 = jnp.bfloat16

D_MODEL = 1024
GRID_W = 64
N_HEADS = 8
N_KV_HEADS = 2
HEAD_DIM = 64
ATTN_WIDTH = N_HEADS * HEAD_DIM
KV_WIDTH = N_KV_HEADS * HEAD_DIM
QK_WIDTH = ATTN_WIDTH + KV_WIDTH
Q_BLOCK = 128
ROPE_THETA = 10000.0
SSM_WIDTH = D_MODEL - ATTN_WIDTH
SSM_GROUP = 16
SSM_GROUPS = SSM_WIDTH // SSM_GROUP
SSM_STATE = 64
SGU_HEADS = 8
SGU_CHUNK = 128
D_FF = 4 * D_MODEL
DEPTH = 2
AB_IN = ATTN_WIDTH + 2 * KV_WIDTH + SSM_WIDTH
ALPHA = (2 * DEPTH) ** 0.25
EPS = 1e-6

LANES = 128
VMEM_LIMIT_BYTES = 56 * 1024 * 1024

SSM_CHUNK = 64
SSM_ROW = SSM_CHUNK * SSM_GROUP
SSM_LAGS = 2 * SSM_CHUNK
SSM_TAPS = SSM_LAGS * SSM_GROUP
SSM_SDIM = 4 * SSM_STATE

TOKEN_TILE = 512


def _cparams(sem):
    return pltpu.CompilerParams(dimension_semantics=sem, vmem_limit_bytes=VMEM_LIMIT_BYTES)


def _const_spec(shape):
    nd = len(shape)
    return pl.BlockSpec(shape, lambda *_: (0,) * nd, pipeline_mode=pl.Buffered(1))


def _layer_norm(r, g, b):
    mu = jnp.mean(r, axis=-1, keepdims=True)
    d = r - mu
    var = jnp.mean(d * d, axis=-1, keepdims=True)
    return d * lax.rsqrt(var + EPS) * g + b


def _split_bf16(x):
    hi = x.astype(BF16)
    lo = (x - hi.astype(F32)).astype(BF16)
    return hi, lo


def _select_cols(x, e):
    hi, lo = _split_bf16(x)
    y = jnp.dot(jnp.concatenate([hi, lo], axis=0), e, preferred_element_type=F32)
    return y[:x.shape[0]] + y[x.shape[0]:]


def _select_rows(e, x):
    hi, lo = _split_bf16(x)
    y = jnp.dot(e, jnp.concatenate([hi, lo], axis=1), preferred_element_type=F32)
    return y[:, :x.shape[1]] + y[:, x.shape[1]:]


def _inproj_kernel(x_ref, w_ref, gain_ref, cos_ref, sin_ref, hmean_ref, q_ref, k_ref, v_ref, u_ref):
    xb = x_ref[...].astype(BF16)
    h = jnp.dot(xb, w_ref[...], preferred_element_type=F32)
    qk = h[:, :QK_WIDTH]
    ms = jnp.dot((qk * qk).astype(BF16), hmean_ref[...], preferred_element_type=F32)
    qkn = qk * lax.rsqrt(ms + EPS) * gain_ref[...]
    c = cos_ref[...]
    s = sin_ref[...]
    even = (lax.broadcasted_iota(jnp.int32, c.shape, 1) % 2) == 0
    outs = []
    for j in range(QK_WIDTH // LANES):
        xs = qkn[:, j * LANES:(j + 1) * LANES]
        nxt = pltpu.roll(xs, LANES - 1, 1)
        prv = pltpu.roll(xs, 1, 1)
        outs.append(xs * c + jnp.where(even, nxt, prv) * s)
    for j in range(ATTN_WIDTH // LANES):
        q_ref[:, j * LANES:(j + 1) * LANES] = outs[j].astype(BF16)
    k_ref[...] = outs[ATTN_WIDTH // LANES].astype(BF16)
    v_ref[...] = h[:, QK_WIDTH:QK_WIDTH + KV_WIDTH].astype(BF16)
    for r in range(u_ref.shape[1]):
        blk = h[r * LANES:(r + 1) * LANES, QK_WIDTH + KV_WIDTH:]
        u_ref[:, r, :] = blk.T


INPROJ_TILE = 1024


def _inproj(x2, w_in, gain, cos_t, sin_t, hmean, seq_len):
    n = x2.shape[0]
    tm = INPROJ_TILE
    nper = seq_len // tm
    return pl.pallas_call(
        _inproj_kernel,
        grid=(n // tm,),
        in_specs=[
            pl.BlockSpec((tm, D_MODEL), lambda i: (i, 0)),
            _const_spec((D_MODEL, AB_IN)),
            _const_spec((1, QK_WIDTH)),
            pl.BlockSpec((tm, LANES), lambda i: (i % nper, 0)),
            pl.BlockSpec((tm, LANES), lambda i: (i % nper, 0)),
            _const_spec((QK_WIDTH, QK_WIDTH)),
        ],
        out_specs=[
            pl.BlockSpec((tm, ATTN_WIDTH), lambda i: (i, 0)),
            pl.BlockSpec((tm, KV_WIDTH), lambda i: (i, 0)),
            pl.BlockSpec((tm, KV_WIDTH), lambda i: (i, 0)),
            pl.BlockSpec((SSM_WIDTH, tm // LANES, LANES), lambda i: (0, i, 0)),
        ],
        out_shape=[
            jax.ShapeDtypeStruct((n, ATTN_WIDTH), BF16),
            jax.ShapeDtypeStruct((n, KV_WIDTH), BF16),
            jax.ShapeDtypeStruct((n, KV_WIDTH), BF16),
            jax.ShapeDtypeStruct((SSM_WIDTH, n // LANES, LANES), F32),
        ],
        compiler_params=_cparams(("parallel",)),
        name="inproj",
    )(x2, w_in, gain, cos_t, sin_t, hmean)


ATT_UNIT_Q = 64


ATT_SHIFT_LIMIT = 60.0


def _attn_kernel(bound_ref, q_ref, k_ref, v_ref, o_ref,
                 s0_ref, s1_ref, p0_ref, p1_ref, m0_ref, m1_ref, l0_ref, l1_ref):
    seq_len = k_ref.shape[1]
    n_units = seq_len // ATT_UNIT_Q
    rep = N_HEADS // N_KV_HEADS
    lane = lax.broadcasted_iota(jnp.int32, (ATT_UNIT_Q, LANES), 1)
    lo_half = lane < HEAD_DIM
    zero = jnp.zeros((ATT_UNIT_Q, LANES), BF16)
    bound = bound_ref[0, 0]

    def rows_of(t):
        return pl.ds(pl.multiple_of(t * ATT_UNIT_Q, ATT_UNIT_Q), ATT_UNIT_Q)

    def raw_scores(t):
        q = q_ref[0, rows_of(t), :]
        slots = [q[:, m * LANES:(m + 1) * LANES] for m in range(rep)]
        qs = jnp.concatenate([jnp.where(lo_half, x, zero) for x in slots]
                             + [jnp.where(lo_half, zero, x) for x in slots], axis=0)
        return lax.dot_general(qs, k_ref[0], (((1,), (1,)), ((), ())), preferred_element_type=F32)

    def scores(t, s_ref, m_ref):
        s = raw_scores(t)
        s_ref[...] = s
        m_ref[...] = jnp.max(s, axis=-1, keepdims=True)

    def probs(s_ref, m_ref, p_ref, l_ref):
        p = jnp.exp2(s_ref[...] - m_ref[...])
        l_ref[...] = jnp.sum(p, axis=-1, keepdims=True)
        p_ref[...] = p.astype(BF16)

    def bounded_probs(t, p_ref, l_ref):
        p = jnp.exp2(raw_scores(t) - bound)
        l_ref[...] = jnp.sum(p, axis=-1, keepdims=True)
        p_ref[...] = p.astype(BF16)

    def output(t, p_ref, l_ref):
        o = jnp.dot(p_ref[...], v_ref[0], preferred_element_type=F32) / l_ref[...]
        for m in range(rep):
            g0 = o[m * ATT_UNIT_Q:(m + 1) * ATT_UNIT_Q]
            g1 = o[(rep + m) * ATT_UNIT_Q:(rep + m + 1) * ATT_UNIT_Q]
            o_ref[0, rows_of(t), m * LANES:(m + 1) * LANES] = jnp.where(lo_half, g0, g1).astype(BF16)

    @pl.when(bound <= ATT_SHIFT_LIMIT)
    def _bounded_shift():
        bounded_probs(0, p0_ref, l0_ref)

        def body(i, carry):
            t = 2 * i + 1
            bounded_probs(t, p1_ref, l1_ref)
            output(t - 1, p0_ref, l0_ref)
            bounded_probs(t + 1, p0_ref, l0_ref)
            output(t, p1_ref, l1_ref)
            return carry

        lax.fori_loop(0, (n_units - 2) // 2, body, 0)
        bounded_probs(n_units - 1, p1_ref, l1_ref)
        output(n_units - 2, p0_ref, l0_ref)
        output(n_units - 1, p1_ref, l1_ref)

    @pl.when(bound > ATT_SHIFT_LIMIT)
    def _row_max_shift():
        scores(0, s0_ref, m0_ref)
        scores(1, s1_ref, m1_ref)
        probs(s0_ref, m0_ref, p0_ref, l0_ref)

        def body(i, carry):
            t = 2 * i + 2
            scores(t, s0_ref, m0_ref)
            probs(s1_ref, m1_ref, p1_ref, l1_ref)
            output(t - 2, p0_ref, l0_ref)
            scores(t + 1, s1_ref, m1_ref)
            probs(s0_ref, m0_ref, p0_ref, l0_ref)
            output(t - 1, p1_ref, l1_ref)
            return carry

        lax.fori_loop(0, (n_units - 2) // 2, body, 0)
        probs(s1_ref, m1_ref, p1_ref, l1_ref)
        output(n_units - 2, p0_ref, l0_ref)
        output(n_units - 1, p1_ref, l1_ref)


def _attention(bound, q, k, v):
    b, seq_len, _ = q.shape
    rows = ATT_UNIT_Q * N_HEADS
    seq = lambda w, **kw: pl.BlockSpec((1, seq_len, w), lambda i: (i, 0, 0), **kw)
    once = dict(pipeline_mode=pl.Buffered(1))
    return pl.pallas_call(
        _attn_kernel,
        grid=(b,),
        in_specs=[pl.BlockSpec(memory_space=pltpu.SMEM),
                  seq(ATTN_WIDTH, **once), seq(KV_WIDTH, **once), seq(KV_WIDTH, **once)],
        out_specs=seq(ATTN_WIDTH),
        out_shape=jax.ShapeDtypeStruct((b, seq_len, ATTN_WIDTH), BF16),
        scratch_shapes=[pltpu.VMEM((rows, seq_len), F32), pltpu.VMEM((rows, seq_len), F32),
                        pltpu.VMEM((rows, seq_len), BF16), pltpu.VMEM((rows, seq_len), BF16),
                        pltpu.VMEM((rows, 1), F32), pltpu.VMEM((rows, 1), F32),
                        pltpu.VMEM((rows, 1), F32), pltpu.VMEM((rows, 1), F32)],
        compiler_params=_cparams(("parallel",)),
        name="attention",
    )(bound, q, k, v)


def _ssm_table_kernel(pc_ref, pr_ref, pr2_ref, bret_ref, bimt_ref, ctr_ref, cti_ref, dt_ref, e16_ref, x16_ref,
                      m_ref, p_ref, q_ref, sca_ref, scb_ref):
    ns = SSM_STATE
    pc = pc_ref[0]
    are_c, aim_c, dt_c = pc[:, 0:1], pc[:, 1:2], jnp.exp(pc[:, 2:3])
    row = lax.broadcasted_iota(jnp.int32, (2 * ns, LANES), 0)
    lan = lax.broadcasted_iota(jnp.int32, (2 * ns, LANES), 1)
    n_g = jnp.where(row >= ns, (SSM_CHUNK - 1) - lan, lan - (SSM_CHUNK - 1))
    mask = n_g >= 0
    nf = jnp.where(mask, n_g, 0).astype(F32)
    rho = jnp.exp(are_c * dt_c * nf)
    ang = aim_c * dt_c * nf
    cs = jnp.where(mask, rho * jnp.cos(ang), 0.0)
    sn = jnp.where(mask, rho * jnp.sin(ang), 0.0)
    base = jnp.concatenate([cs[:ns], sn[:ns], cs[ns:], sn[ns:]], axis=0)
    ex = _select_cols(base, e16_ref[...])
    ecf, esf, ecb, esb = ex[:ns], ex[ns:2 * ns], ex[2 * ns:3 * ns], ex[3 * ns:]

    def tile_lanes(x, width):
        for sh in (SSM_GROUP, 2 * SSM_GROUP, 4 * SSM_GROUP):
            x = x + pltpu.roll(x, sh, 1)
        return jnp.concatenate([x] * (width // LANES), axis=1)

    crt = tile_lanes(ctr_ref[0], SSM_TAPS)
    cit = tile_lanes(cti_ref[0], SSM_TAPS)
    g_fre = ecf * crt - esf * cit
    g_fim = -(esf * crt + ecf * cit)
    g_bre = ecb * crt - esb * cit
    g_bim = -(esb * crt + ecb * cit)
    g_tab = jnp.concatenate([g_fre, g_fim, g_bre, g_bim], axis=0)

    ab_r = cs[ns:, SSM_CHUNK - 2:SSM_CHUNK - 1]
    ab_i = sn[ns:, SSM_CHUNK - 2:SSM_CHUNK - 1]
    q_tab = jnp.concatenate([
        g_fre[:, SSM_ROW:], g_fim[:, SSM_ROW:],
        ab_r * g_bre[:, :SSM_ROW] + ab_i * g_bim[:, :SSM_ROW],
        ab_r * g_bim[:, :SSM_ROW] - ab_i * g_bre[:, :SSM_ROW]], axis=0)
    q_ref[0] = q_tab.astype(BF16)

    pr = pr_ref[0]
    are_r, aim_r, ls_r = pr[0:1, :], pr[1:2, :], pr[2:3, :]
    dt_r = jnp.exp(ls_r)
    rho1 = jnp.exp(are_r * dt_r)
    th1 = aim_r * dt_r
    abr = rho1 * jnp.cos(th1) - 1.0
    abi = rho1 * jnp.sin(th1)
    den = are_r * are_r + aim_r * aim_r
    cr = (abr * are_r + abi * aim_r) / den
    ci = (abi * are_r - abr * aim_r) / den
    bret = bret_ref[0]
    bimt = bimt_ref[0]
    bbr = cr * bret - ci * bimt
    bbi = cr * bimt + ci * bret
    lan_r = lax.broadcasted_iota(jnp.int32, (SSM_GROUP, SSM_SDIM), 1)
    im_lane = ((lan_r // ns) % 2) == 1
    bcat = jnp.where(im_lane, bbi, bbr)

    bh, bl = _split_bf16(bcat)
    gh, gl = _split_bf16(g_tab)
    w = (jnp.dot(bh, gh, preferred_element_type=F32) + jnp.dot(bh, gl, preferred_element_type=F32)
         + jnp.dot(bl, gh, preferred_element_type=F32))
    wrow = lax.broadcasted_iota(jnp.int32, (SSM_GROUP, SSM_TAPS), 0)
    wcol = lax.broadcasted_iota(jnp.int32, (SSM_GROUP, SSM_TAPS), 1)
    on_diag = (wcol // SSM_GROUP == SSM_CHUNK - 1) & (wcol % SSM_GROUP == wrow)
    w = w + jnp.where(on_diag, dt_ref[0], 0.0)

    per_vreg = LANES // SSM_GROUP
    rolled = [w] + [pltpu.roll(w, SSM_TAPS - SSM_GROUP * kk, 1) for kk in range(1, per_vreg)]
    for i in range(SSM_CHUNK):
        r = SSM_CHUNK - 1 - i
        blk = rolled[r % per_vreg][:, (r // per_vreg) * LANES:(r // per_vreg) * LANES + SSM_ROW]
        m_ref[0, i * SSM_GROUP:(i + 1) * SSM_GROUP, :] = blk.astype(BF16)

    pr2 = pr2_ref[0]
    i_r = lax.broadcasted_iota(jnp.int32, (SSM_CHUNK, LANES), 0)
    l_r = lax.broadcasted_iota(jnp.int32, (SSM_CHUNK, LANES), 1)
    lo_r = l_r < ns
    n_p = jnp.where(lo_r, (SSM_CHUNK - 1) - i_r, i_r).astype(F32)
    dt2 = jnp.exp(pr2[2:3, :])
    rho = jnp.exp(pr2[0:1, :] * dt2 * n_p)
    ang = pr2[1:2, :] * dt2 * n_p
    c = rho * jnp.cos(ang)
    s = rho * jnp.sin(ang)
    base_p = jnp.concatenate([jnp.where(lo_r, c, pltpu.roll(s, ns, 1)),
                              jnp.where(lo_r, pltpu.roll(c, ns, 1), s)], axis=1)
    px = _select_rows(x16_ref[...], base_p)
    sign = jnp.where(lax.broadcasted_iota(jnp.int32, (SSM_ROW, LANES), 1) < ns, -1.0, 1.0)
    px2 = jnp.concatenate([pltpu.roll(px[:, :LANES], ns, 1) * sign,
                           pltpu.roll(px[:, LANES:], ns, 1) * sign], axis=1)
    bt_r = jnp.concatenate([bbr] * SSM_CHUNK, axis=0)
    bt_i = jnp.concatenate([bbi] * SSM_CHUNK, axis=0)
    p_ref[0] = (px * bt_r + px2 * bt_i).astype(BF16)

    k_r = lax.broadcasted_iota(jnp.int32, (8, SSM_SDIM), 0)
    l8 = lax.broadcasted_iota(jnp.int32, (8, SSM_SDIM), 1)
    n_s = (SSM_CHUNK * jnp.left_shift(1, k_r)).astype(F32)
    rho = jnp.exp(are_r * dt_r * n_s)
    ang = aim_r * dt_r * n_s
    sca_ref[0] = rho * jnp.cos(ang)
    scb_ref[0] = rho * jnp.sin(ang) * jnp.where(((l8 // ns) % 2) == 1, 1.0, -1.0)


def _ssm_tables(pc, pr, pr2, bret, bimt, ctr, cti, dtile, e16, x16):
    g = SSM_GROUPS
    grp = lambda shape: pl.BlockSpec((1,) + shape, lambda i: (i, 0, 0))
    return pl.pallas_call(
        _ssm_table_kernel,
        grid=(g,),
        in_specs=[
            grp((2 * SSM_STATE, 8)), grp((8, SSM_SDIM)), grp((8, LANES)),
            grp((SSM_GROUP, SSM_SDIM)), grp((SSM_GROUP, SSM_SDIM)),
            grp((SSM_STATE, LANES)), grp((SSM_STATE, LANES)), grp((1, SSM_TAPS)),
            _const_spec((LANES, SSM_TAPS)), _const_spec((SSM_ROW, SSM_CHUNK)),
        ],
        out_specs=[
            grp((SSM_ROW, SSM_ROW)), grp((SSM_ROW, SSM_SDIM)), grp((SSM_SDIM, SSM_ROW)),
            grp((8, SSM_SDIM)), grp((8, SSM_SDIM)),
        ],
        out_shape=[
            jax.ShapeDtypeStruct((g, SSM_ROW, SSM_ROW), BF16),
            jax.ShapeDtypeStruct((g, SSM_ROW, SSM_SDIM), BF16),
            jax.ShapeDtypeStruct((g, SSM_SDIM, SSM_ROW), BF16),
            jax.ShapeDtypeStruct((g, 8, SSM_SDIM), F32),
            jax.ShapeDtypeStruct((g, 8, SSM_SDIM), F32),
        ],
        compiler_params=_cparams(("parallel",)),
        name="ssm_tables",
    )(pc, pr, pr2, bret, bimt, ctr, cti, dtile, e16, x16)


def _ssm_apply_kernel(z_ref, m_ref, p_ref, q_ref, sca_ref, scb_ref, y_ref, *, segments):
    z = z_ref[0]
    y = jnp.dot(z, m_ref[0], preferred_element_type=F32)
    st = jnp.dot(z, p_ref[0], preferred_element_type=F32)
    sca = sca_ref[0]
    scb = scb_ref[0]
    carried = []
    for (r0, rows, nc) in segments:
        c_idx = lax.broadcasted_iota(jnp.int32, (rows, LANES), 0) % nc
        xf = st[r0:r0 + rows, :LANES]
        xb = st[r0:r0 + rows, LANES:]
        for kk in range(int(math.log2(nc))):
            d = 1 << kk
            sh = jnp.where(c_idx >= d, pltpu.roll(xf, d, 0), 0.0)
            xf = xf + sh * sca[kk:kk + 1, :LANES] + pltpu.roll(sh, SSM_STATE, 1) * scb[kk:kk + 1, :LANES]
            sh = jnp.where(c_idx < nc - d, pltpu.roll(xb, rows - d, 0), 0.0)
            xb = xb + sh * sca[kk:kk + 1, LANES:] + pltpu.roll(sh, SSM_STATE, 1) * scb[kk:kk + 1, LANES:]
        prev_f = jnp.where(c_idx >= 1, pltpu.roll(xf, 1, 0), 0.0)
        next_b = jnp.where(c_idx < nc - 1, pltpu.roll(xb, rows - 1, 0), 0.0)
        carried.append(jnp.concatenate([prev_f, next_b], axis=1))
    xin = jnp.concatenate(carried, axis=0).astype(BF16)
    y_ref[0] = (y + jnp.dot(xin, q_ref[0], preferred_element_type=F32)).astype(y_ref.dtype)


def _ssm_apply(z, m_tab, p_tab, q_tab, sca, scb, segments):
    g, rows, _ = z.shape
    grp = lambda shape: pl.BlockSpec((1,) + shape, lambda i: (i, 0, 0))
    return pl.pallas_call(
        functools.partial(_ssm_apply_kernel, segments=segments),
        grid=(g,),
        in_specs=[grp((rows, SSM_ROW)), grp((SSM_ROW, SSM_ROW)), grp((SSM_ROW, SSM_SDIM)),
                  grp((SSM_SDIM, SSM_ROW)), grp((8, SSM_SDIM)), grp((8, SSM_SDIM))],
        out_specs=grp((rows, SSM_ROW)),
        out_shape=jax.ShapeDtypeStruct((g, rows, SSM_ROW), BF16),
        compiler_params=_cparams(("parallel",)),
        name="ssm_apply",
    )(z, m_tab, p_tab, q_tab, sca, scb)


SSM_PITCH = 132


def _s5_kernel(xp_ref, xs_ref, pc_ref, pr2_ref, bcr_ref, bci_ref, ccat_ref, cr4_ref, ci4_ref, dt_ref,
               e16_ref, x16_ref, yp_ref, ys_ref, s_ref, zt_ref, m_ref, yt_ref, *, pairs):
    ns = SSM_STATE
    pc = pc_ref[0]
    are_c, aim_c, dt_c = pc[:, 0:1], pc[:, 1:2], jnp.exp(pc[:, 2:3])
    row = lax.broadcasted_iota(jnp.int32, (2 * ns, LANES), 0)
    lan = lax.broadcasted_iota(jnp.int32, (2 * ns, LANES), 1)
    b_rows = row >= ns
    n_g = jnp.where(b_rows, lan - (SSM_CHUNK - 1), (SSM_CHUNK - 1) - lan)
    mask = n_g >= 0
    nf = jnp.where(mask, n_g, 0).astype(F32)
    rho = jnp.exp(are_c * dt_c * nf)
    ang = aim_c * dt_c * nf
    cs = jnp.where(mask, rho * jnp.cos(ang), 0.0)
    sn = jnp.where(mask, rho * jnp.sin(ang), 0.0)
    b_col = b_rows[:, 0:1]
    a1r = jnp.where(b_col, cs[:, SSM_CHUNK:SSM_CHUNK + 1], cs[:, SSM_CHUNK - 2:SSM_CHUNK - 1]) - 1.0
    a1i = jnp.where(b_col, sn[:, SSM_CHUNK:SSM_CHUNK + 1], sn[:, SSM_CHUNK - 2:SSM_CHUNK - 1])
    den = are_c * are_c + aim_c * aim_c
    cr = (a1r * are_c + a1i * aim_c) / den
    ci = (a1i * are_c - a1r * aim_c) / den

    def tile_lanes(x):
        for sh in (SSM_GROUP, 2 * SSM_GROUP, 4 * SSM_GROUP):
            x = x + pltpu.roll(x, sh, 1)
        return jnp.concatenate([x] * (SSM_TAPS // LANES), axis=1)

    bre, bim = bcr_ref[0], bci_ref[0]
    bbr = tile_lanes(cr * bre - ci * bim)
    bbi = tile_lanes(cr * bim + ci * bre)
    base = jnp.concatenate([cs[:ns], sn[:ns], cs[ns:], sn[ns:]], axis=0)
    ex = _select_cols(base, e16_ref[...])
    ecf, esf, ecb, esb = ex[:ns], ex[ns:2 * ns], ex[2 * ns:3 * ns], ex[3 * ns:]
    z_fre = ecf * bbr[:ns] - esf * bbi[:ns]
    z_fim = esf * bbr[:ns] + ecf * bbi[:ns]
    z_bre = ecb * bbr[ns:] - esb * bbi[ns:]
    z_bim = esb * bbr[ns:] + ecb * bbi[ns:]

    g_tab = jnp.concatenate([z_fre, -z_fim, z_bre, -z_bim], axis=0)
    ch, cl = _split_bf16(ccat_ref[0])
    gh, gl = _split_bf16(g_tab)
    w = (jnp.dot(ch, gh, preferred_element_type=F32) + jnp.dot(ch, gl, preferred_element_type=F32)
         + jnp.dot(cl, gh, preferred_element_type=F32))
    wrow = lax.broadcasted_iota(jnp.int32, (SSM_GROUP, SSM_TAPS), 0)
    wcol = lax.broadcasted_iota(jnp.int32, (SSM_GROUP, SSM_TAPS), 1)
    on_diag = (wcol // SSM_GROUP == SSM_CHUNK - 1) & (wcol % SSM_GROUP == wrow)
    w = w + jnp.where(on_diag, dt_ref[0], 0.0)
    per_vreg = LANES // SSM_GROUP
    rolled = [w] + [pltpu.roll(w, SSM_TAPS - SSM_GROUP * kk, 1) for kk in range(1, per_vreg)]
    for j in range(SSM_CHUNK):
        r = SSM_CHUNK - 1 - j
        blk = rolled[r % per_vreg][:, (r // per_vreg) * LANES:(r // per_vreg) * LANES + SSM_ROW]
        m_ref[j * SSM_GROUP:(j + 1) * SSM_GROUP, :] = blk.astype(BF16)

    p2 = jnp.concatenate([z_fre[:, :SSM_ROW], z_fim[:, :SSM_ROW], z_bre[:, SSM_ROW:], z_bim[:, SSM_ROW:]],
                         axis=0).astype(BF16)

    pr2 = pr2_ref[0]
    j_r = lax.broadcasted_iota(jnp.int32, (SSM_CHUNK, LANES), 0)
    l_r = lax.broadcasted_iota(jnp.int32, (SSM_CHUNK, LANES), 1)
    lo_r = l_r < ns
    n_q = jnp.where(lo_r, j_r + 1, (SSM_CHUNK - 1) - j_r).astype(F32)
    dt2 = jnp.exp(pr2[2:3, :])
    rho = jnp.exp(pr2[0:1, :] * dt2 * n_q)
    ang = pr2[1:2, :] * dt2 * n_q
    c2 = rho * jnp.cos(ang)
    s2 = rho * jnp.sin(ang)
    base_q = jnp.concatenate([jnp.where(lo_r, c2, -pltpu.roll(s2, ns, 1)),
                              jnp.where(lo_r, pltpu.roll(c2, ns, 1), -s2)], axis=1)
    a1x = _select_rows(x16_ref[...], base_q)
    sign = jnp.where(lax.broadcasted_iota(jnp.int32, (SSM_ROW, LANES), 1) < ns, 1.0, -1.0)
    a2x = jnp.concatenate([pltpu.roll(a1x[:, :LANES], ns, 1) * sign,
                           pltpu.roll(a1x[:, LANES:], ns, 1) * sign], axis=1)
    q2 = (a1x * jnp.concatenate([cr4_ref[0]] * SSM_CHUNK, axis=0)
          + a2x * jnp.concatenate([ci4_ref[0]] * SSM_CHUNK, axis=0)).astype(BF16)

    n_s = (SSM_CHUNK * jnp.left_shift(1, jnp.minimum(lan, 8))).astype(F32)
    rho = jnp.exp(are_c * dt_c * n_s)
    ang = aim_c * dt_c * n_s
    sc_r = rho * jnp.cos(ang)
    sc_i = rho * jnp.sin(ang)

    for seg, x_ref in enumerate((xp_ref, xs_ref)):
        for hi in range(SSM_GROUP):
            s_ref[seg, hi * SSM_PITCH:hi * SSM_PITCH + LANES, :] = x_ref[hi].T
    for par in range(2):
        for i in range(SSM_CHUNK):
            for seg in range(2):
                rows16 = s_ref[seg, pl.ds(par * SSM_CHUNK + i, SSM_GROUP, stride=SSM_PITCH), :]
                blk = 2 * par + seg
                zt_ref[i * SSM_GROUP:(i + 1) * SSM_GROUP, blk * LANES:(blk + 1) * LANES] = rows16.astype(BF16)

    zt = zt_ref[...]
    yt = jnp.dot(m_ref[...], zt, preferred_element_type=F32)
    st = jnp.dot(p2, zt, preferred_element_type=F32)

    def cmul(kk, lo, v):
        ar, ai = sc_r[lo:lo + ns, kk:kk + 1], sc_i[lo:lo + ns, kk:kk + 1]
        return jnp.concatenate([ar * v[:ns] - ai * v[ns:], ar * v[ns:] + ai * v[:ns]], axis=0)

    carried = [None] * 4
    for seg, npairs in enumerate(pairs):
        c2i = lax.broadcasted_iota(jnp.int32, (2 * ns, LANES), 1) % npairs
        right = lambda v, d: jnp.where(c2i >= d, pltpu.roll(v, d, 1), 0.0)
        left = lambda v, d: jnp.where(c2i < npairs - d, pltpu.roll(v, LANES - d, 1), 0.0)
        ev, od = st[:, seg * LANES:(seg + 1) * LANES], st[:, (2 + seg) * LANES:(3 + seg) * LANES]
        ef, of, eb, ob = ev[:2 * ns], od[:2 * ns], ev[2 * ns:], od[2 * ns:]
        ef, of = ef + cmul(0, 0, right(of, 1)), of + cmul(0, 0, ef)
        eb, ob = eb + cmul(0, ns, ob), ob + cmul(0, ns, left(eb, 1))
        for kk in range(1, int(math.log2(npairs)) + 1):
            d = 1 << (kk - 1)
            ef, of = ef + cmul(kk, 0, right(ef, d)), of + cmul(kk, 0, right(of, d))
            eb, ob = eb + cmul(kk, ns, left(eb, d)), ob + cmul(kk, ns, left(ob, d))
        carried[seg] = jnp.concatenate([right(of, 1), ob], axis=0)
        carried[2 + seg] = jnp.concatenate([ef, left(eb, 1)], axis=0)
    xin = jnp.concatenate(carried, axis=1).astype(BF16)
    yt = yt + jnp.dot(q2, xin, preferred_element_type=F32)

    for blk in range(4):
        yt_ref[blk] = yt[:, blk * LANES:(blk + 1) * LANES]
    for seg, y_ref in enumerate((yp_ref, ys_ref)):
        for ho in range(SSM_GROUP):
            ev = yt_ref[seg, pl.ds(ho, SSM_CHUNK, stride=SSM_GROUP), :]
            od = yt_ref[2 + seg, pl.ds(ho, SSM_CHUNK, stride=SSM_GROUP), :]
            y_ref[ho] = jnp.concatenate([ev, od], axis=0).T


def _s5(u_p, u_s, pc, pr2, bcr, bci, ccat, cr4, ci4, dtile, e16, x16, pairs):
    g = SSM_GROUPS
    rows = u_p.shape[1]
    assert u_s.shape[1] == rows == LANES
    grp = lambda shape: pl.BlockSpec((1,) + shape, lambda i: (i, 0, 0))
    chan = pl.BlockSpec((SSM_GROUP, rows, LANES), lambda i: (i, 0, 0))
    return pl.pallas_call(
        functools.partial(_s5_kernel, pairs=pairs),
        grid=(g,),
        in_specs=[chan, chan, grp((2 * SSM_STATE, 8)), grp((8, LANES)),
                  grp((2 * SSM_STATE, LANES)), grp((2 * SSM_STATE, LANES)),
                  grp((SSM_GROUP, SSM_SDIM)), grp((SSM_GROUP, SSM_SDIM)), grp((SSM_GROUP, SSM_SDIM)),
                  grp((1, SSM_TAPS)), _const_spec((LANES, SSM_TAPS)), _const_spec((SSM_ROW, SSM_CHUNK))],
        out_specs=[chan, chan],
        out_shape=[jax.ShapeDtypeStruct(u_p.shape, F32), jax.ShapeDtypeStruct(u_s.shape, F32)],
        scratch_shapes=[pltpu.VMEM((2, SSM_GROUP * SSM_PITCH, LANES), F32),
                        pltpu.VMEM((SSM_ROW, 4 * LANES), BF16),
                        pltpu.VMEM((SSM_ROW, SSM_ROW), BF16),
                        pltpu.VMEM((4, SSM_ROW, LANES), F32)],
        compiler_params=_cparams(("parallel",)),
        name="s5",
    )(u_p, u_s, pc, pr2, bcr, bci, ccat, cr4, ci4, dtile, e16, x16)


FF_CHUNK = 1024


def _ffn_ln(x, w1_ref, w2_ref, g, b):
    xb = x.astype(BF16)
    acc = ALPHA * x
    for c in range(D_FF // FF_CHUNK):
        h = jnp.dot(xb, w1_ref[0, :, c * FF_CHUNK:(c + 1) * FF_CHUNK], preferred_element_type=F32)
        h = jnp.maximum(h, 0.0)
        acc = acc + jnp.dot((h * h).astype(BF16), w2_ref[0, c * FF_CHUNK:(c + 1) * FF_CHUNK, :],
                            preferred_element_type=F32)
    return _layer_norm(acc, g, b)


def _layer0_tail_kernel(x_ref, a_ref, y_ref, wg_ref, bg_ref, woa_ref, wos_ref, ln_ref, w1_ref, w2_ref, o_ref):
    first_half = pl.program_id(0) % 2 == 0
    rows_per_tile = x_ref.shape[0] // LANES
    y_tok = jnp.concatenate(
        [jnp.where(first_half, y_ref[:, r, :], y_ref[:, rows_per_tile + r, :]).T for r in range(rows_per_tile)],
        axis=0)
    y = jax.nn.gelu(y_tok)
    gate = jnp.dot(y.astype(BF16), wg_ref[...], preferred_element_type=F32) + bg_ref[...]
    zs = (y * jax.nn.sigmoid(gate)).astype(BF16)
    mix = (jnp.dot(a_ref[...], woa_ref[...], preferred_element_type=F32)
           + jnp.dot(zs, wos_ref[...], preferred_element_type=F32))
    ln = ln_ref[...]
    x1 = _layer_norm(ALPHA * x_ref[...] + mix, ln[0:1], ln[1:2])
    o_ref[...] = _ffn_ln(x1, w1_ref, w2_ref, ln[2:3], ln[3:4])


def _layer_weights_spec(shape, layer):
    return pl.BlockSpec((1,) + shape, lambda i: (layer, 0, 0), pipeline_mode=pl.Buffered(1))


def _layer0_tail(x2, attn, y, w_glu, b_glu, w_out_a, w_out_s, ln4, w1, w2):
    n = x2.shape[0]
    tm = TOKEN_TILE
    tok = lambda w: pl.BlockSpec((tm, w), lambda i: (i, 0))
    return pl.pallas_call(
        _layer0_tail_kernel,
        grid=(n // tm,),
        in_specs=[tok(D_MODEL), tok(ATTN_WIDTH),
                  pl.BlockSpec((SSM_WIDTH, 2 * tm // LANES, LANES), lambda i: (0, i // 2, 0)),
                  _const_spec((SSM_WIDTH, SSM_WIDTH)), _const_spec((1, SSM_WIDTH)),
                  _const_spec((ATTN_WIDTH, D_MODEL)), _const_spec((SSM_WIDTH, D_MODEL)),
                  _const_spec((4, D_MODEL)),
                  _layer_weights_spec((D_MODEL, D_FF), 0), _layer_weights_spec((D_FF, D_MODEL), 0)],
        out_specs=tok(D_MODEL),
        out_shape=jax.ShapeDtypeStruct((n, D_MODEL), F32),
        compiler_params=_cparams(("parallel",)),
        name="layer0_tail",
    )(x2, attn, y, w_glu, b_glu, w_out_a, w_out_s, ln4, w1, w2)


def _layer1_kernel(x_ref, win_ref, cln_ref, ws_ref, bs_ref, wout_ref, ln_ref, w1_ref, w2_ref, o_ref):
    x = x_ref[...]
    tm = x.shape[0]
    h = jax.nn.gelu(jnp.dot(x.astype(BF16), win_ref[...], preferred_element_type=F32))
    u = h[:, :D_MODEL]
    cln = cln_ref[...]
    vb = _layer_norm(h[:, D_MODEL:], cln[0:1], cln[1:2]).astype(BF16)
    rows = []
    for c in range(tm // SGU_CHUNK):
        cols = []
        for hd in range(SGU_HEADS):
            blk = vb[c * SGU_CHUNK:(c + 1) * SGU_CHUNK, hd * LANES:(hd + 1) * LANES]
            cols.append(jnp.dot(ws_ref[hd], blk, preferred_element_type=F32) + bs_ref[hd])
        rows.append(jnp.concatenate(cols, axis=1))
    mixed = jnp.concatenate(rows, axis=0)
    out = jnp.dot((u * mixed).astype(BF16), wout_ref[...], preferred_element_type=F32)
    ln = ln_ref[...]
    x1 = _layer_norm(ALPHA * x + out, ln[0:1], ln[1:2])
    o_ref[...] = _ffn_ln(x1, w1_ref, w2_ref, ln[2:3], ln[3:4])


def _layer1(x2, w_in, cln2, w_s, b_s_full, w_out, ln4, w1, w2):
    n = x2.shape[0]
    tm = TOKEN_TILE
    tok = pl.BlockSpec((tm, D_MODEL), lambda i: (i, 0))
    return pl.pallas_call(
        _layer1_kernel,
        grid=(n // tm,),
        in_specs=[tok, _const_spec((D_MODEL, 2 * D_MODEL)), _const_spec((2, D_MODEL)),
                  _const_spec((SGU_HEADS, SGU_CHUNK, SGU_CHUNK)), _const_spec((SGU_HEADS, SGU_CHUNK, LANES)),
                  _const_spec((D_MODEL, D_MODEL)), _const_spec((4, D_MODEL)),
                  _layer_weights_spec((D_MODEL, D_FF), 1), _layer_weights_spec((D_FF, D_MODEL), 1)],
        out_specs=tok,
        out_shape=jax.ShapeDtypeStruct((n, D_MODEL), F32),
        compiler_params=_cparams(("parallel",)),
        name="layer1",
    )(x2, w_in, cln2, w_s, b_s_full, w_out, ln4, w1, w2)


_HEAD_OF_SLOT = np.array([0, 4, 1, 5, 2, 6, 3, 7])


def _rope_tables(seq_len):
    t = np.arange(seq_len)
    row = (t // GRID_W).astype(np.float32)
    col = (t % GRID_W).astype(np.float32)
    axis_dim = HEAD_DIM // 2
    inv_freq = ROPE_THETA ** (-jnp.arange(0, axis_dim, 2, dtype=F32) / axis_dim)
    ang = jnp.concatenate([jnp.asarray(row)[:, None] * inv_freq, jnp.asarray(col)[:, None] * inv_freq], axis=-1)
    cos = jnp.repeat(jnp.cos(ang), 2, axis=-1)
    sin = jnp.repeat(jnp.sin(ang), 2, axis=-1)
    sign = jnp.tile(jnp.asarray([-1.0, 1.0], F32), HEAD_DIM // 2)
    reps = LANES // HEAD_DIM
    return jnp.tile(cos, (1, reps)), jnp.tile(sin * sign, (1, reps))


def _selectors():
    col = np.arange(SSM_TAPS)
    row = np.arange(LANES)
    e16 = (col[None, :] // SSM_GROUP == row[:, None])
    r2 = np.arange(SSM_ROW)
    x16 = (r2[:, None] // SSM_GROUP == np.arange(SSM_CHUNK)[None, :])
    as_bf16 = lambda a: jnp.asarray(a.astype(np.float32), BF16)
    return as_bf16(e16), as_bf16(x16)


def _ssm_param_layout(a_re_f, a_im_f, ls_f, a_re_b, a_im_b, ls_b, b_re, b_im, c_re, c_im, d):
    g, ns = SSM_GROUPS, SSM_STATE
    lsf = jnp.broadcast_to(ls_f[:, None], (g, ns))
    lsb = jnp.broadcast_to(ls_b[:, None], (g, ns))
    zeros = jnp.zeros((g, ns), F32)
    per_f = jnp.stack([a_re_f, a_im_f, lsf] + [zeros] * 5, axis=1)
    per_b = jnp.stack([a_re_b, a_im_b, lsb] + [zeros] * 5, axis=1)
    pr = jnp.concatenate([per_f, per_f, per_b, per_b], axis=2)
    pr2 = jnp.concatenate([per_f, per_b], axis=2)
    pc = jnp.transpose(pr2, (0, 2, 1))
    def rows_t(b):
        bt = jnp.transpose(b, (0, 2, 1))
        return jnp.concatenate([bt, bt, bt, bt], axis=2)
    def cols_t(c):
        return jnp.pad(jnp.transpose(c, (0, 2, 1)), ((0, 0), (0, 0), (0, LANES - SSM_GROUP)))
    dtile = jnp.tile(d.reshape(g, 1, SSM_GROUP), (1, 1, SSM_LAGS))
    return pc, pr, pr2, rows_t(b_re), rows_t(b_im), cols_t(c_re), cols_t(c_im), dtile


def _s5_param_layout(a_re_f, a_im_f, ls_f, a_re_b, a_im_b, ls_b, b_re, b_im, c_re, c_im, d):
    g, ns = SSM_GROUPS, SSM_STATE
    zeros = jnp.zeros((g, ns), F32)
    per = lambda a_re, a_im, ls: jnp.stack([a_re, a_im, jnp.broadcast_to(ls[:, None], (g, ns))] + [zeros] * 5,
                                           axis=1)
    pr2 = jnp.concatenate([per(a_re_f, a_im_f, ls_f), per(a_re_b, a_im_b, ls_b)], axis=2)
    pc = jnp.transpose(pr2, (0, 2, 1))
    pad_lanes = lambda b: jnp.pad(jnp.concatenate([b, b], axis=1), ((0, 0), (0, 0), (0, LANES - SSM_GROUP)))
    ccat = jnp.concatenate([c_re, c_im, c_re, c_im], axis=2)
    cr4 = jnp.concatenate([c_re] * 4, axis=2)
    ci4 = jnp.concatenate([c_im] * 4, axis=2)
    dtile = jnp.tile(d.reshape(g, 1, SSM_GROUP), (1, 1, SSM_LAGS))
    return pc, pr2, pad_lanes(b_re), pad_lanes(b_im), ccat, cr4, ci4, dtile


def kernel(x_prompt, x_sample, ab_w_in, ab_q_norm, ab_k_norm, ssm_a_re_f, ssm_a_im_f, ssm_log_step_f,
           ssm_a_re_b, ssm_a_im_b, ssm_log_step_b, ssm_b_re, ssm_b_im, ssm_c_re, ssm_c_im, ssm_d, ssm_w_glu,
           ssm_b_glu, ab_w_out, c_w_in, c_ln_g, c_ln_b, c_w_s, c_b_s, c_w_out, ln_mix_g, ln_mix_b, ff_w1, ff_w2,
           ln_ff_g, ln_ff_b):
    xs = [x_prompt, x_sample]
    row2 = lambda v: v.reshape(1, -1)

    slot_cols = (_HEAD_OF_SLOT[:, None] * HEAD_DIM + np.arange(HEAD_DIM)[None, :]).reshape(-1)
    w_in = ab_w_in[0]
    w_in = jnp.concatenate([w_in[:, slot_cols], w_in[:, ATTN_WIDTH:]], axis=1).astype(BF16)
    scale = HEAD_DIM ** -0.5 * math.log2(math.e)
    gain = jnp.concatenate([jnp.tile(ab_q_norm[0] * scale, N_HEADS), jnp.tile(ab_k_norm[0], N_KV_HEADS)])
    score_bound = (1.05 * HEAD_DIM * scale * jnp.max(jnp.abs(ab_q_norm[0])) * jnp.max(jnp.abs(ab_k_norm[0])))
    score_bound = score_bound.reshape(1, 1).astype(F32)
    hid = np.arange(QK_WIDTH) // HEAD_DIM
    hmean = jnp.asarray((hid[:, None] == hid[None, :]).astype(np.float32) / HEAD_DIM, BF16)
    w_out = ab_w_out[0]
    w_out_a = w_out[:ATTN_WIDTH][slot_cols].astype(BF16)
    w_out_s = w_out[ATTN_WIDTH:].astype(BF16)
    w_glu = ssm_w_glu[0].astype(BF16)
    attn, us, pairs = [], [], []
    for x in xs:
        b, seq_len, _ = x.shape
        cos_t, sin_t = _rope_tables(seq_len)
        q, k, v, u = _inproj(x.reshape(b * seq_len, D_MODEL), w_in, row2(gain), cos_t, sin_t, hmean, seq_len)
        attn.append(_attention(score_bound, q.reshape(b, seq_len, ATTN_WIDTH), k.reshape(b, seq_len, KV_WIDTH),
                               v.reshape(b, seq_len, KV_WIDTH)).reshape(b * seq_len, ATTN_WIDTH))
        us.append(u)
        pairs.append(seq_len // (2 * SSM_CHUNK))
    ys = _s5(us[0], us[1], *_s5_param_layout(
        ssm_a_re_f[0], ssm_a_im_f[0], ssm_log_step_f[0], ssm_a_re_b[0], ssm_a_im_b[0], ssm_log_step_b[0],
        ssm_b_re[0], ssm_b_im[0], ssm_c_re[0], ssm_c_im[0], ssm_d[0]), *_selectors(), tuple(pairs))

    ff_w1b = ff_w1.astype(BF16)
    ff_w2b = ff_w2.astype(BF16)
    ln4 = [jnp.stack([ln_mix_g[i], ln_mix_b[i], ln_ff_g[i], ln_ff_b[i]]) for i in range(DEPTH)]
    cln2 = jnp.stack([c_ln_g[0], c_ln_b[0]])
    bs_full = jnp.broadcast_to(c_b_s[0][:, :, None], (SGU_HEADS, SGU_CHUNK, LANES))
    c_w_in_b, c_w_s_b, c_w_out_b = c_w_in[0].astype(BF16), c_w_s[0].astype(BF16), c_w_out[0].astype(BF16)
    outs = []
    for x, a, y in zip(xs, attn, ys):
        b, seq_len, _ = x.shape
        n = b * seq_len
        h = _layer0_tail(x.reshape(n, D_MODEL), a, y, w_glu, row2(ssm_b_glu[0]), w_out_a, w_out_s,
                         ln4[0], ff_w1b, ff_w2b)
        h = _layer1(h, c_w_in_b, cln2, c_w_s_b, bs_full, c_w_out_b, ln4[1], ff_w1b, ff_w2b)
        outs.append(h.reshape(b, seq_len, D_MODEL))
    return tuple(outs)
```

```python
import functools
import math

import numpy as np
import jax
import jax.numpy as jnp
from jax import lax
from jax.experimental import pallas as pl
from jax.experimental.pallas import tpu as pltpu

F32 = jnp.float32
BF16 = jnp.bfloat16

D_MODEL = 1024
GRID_W = 64
N_HEADS = 8
N_KV_HEADS = 2
HEAD_DIM = 64
ATTN_WIDTH = N_HEADS * HEAD_DIM
KV_WIDTH = N_KV_HEADS * HEAD_DIM
QK_WIDTH = ATTN_WIDTH + KV_WIDTH
ROPE_THETA = 10000.0
SSM_WIDTH = D_MODEL - ATTN_WIDTH
SSM_GROUP = 16
SSM_GROUPS = SSM_WIDTH // SSM_GROUP
SSM_STATE = 64
SGU_HEADS = 8
SGU_CHUNK = 128
D_FF = 4 * D_MODEL
DEPTH = 2
AB_IN = ATTN_WIDTH + 2 * KV_WIDTH + SSM_WIDTH
ALPHA = (2 * DEPTH) ** 0.25
EPS = 1e-6

LANES = 128
VMEM_LIMIT_BYTES = 56 * 1024 * 1024

SSM_CHUNK = 64
SSM_ROW = SSM_CHUNK * SSM_GROUP
SSM_LAGS = 2 * SSM_CHUNK
SSM_TAPS = SSM_LAGS * SSM_GROUP
SSM_SDIM = 4 * SSM_STATE

TOKEN_TILE = 512


def _cparams(sem):
    return pltpu.CompilerParams(dimension_semantics=sem, vmem_limit_bytes=VMEM_LIMIT_BYTES)


def _const_spec(shape):
    nd = len(shape)
    return pl.BlockSpec(shape, lambda *_: (0,) * nd, pipeline_mode=pl.Buffered(1))


def _layer_norm(r, g, b):
    mu = jnp.mean(r, axis=-1, keepdims=True)
    d = r - mu
    var = jnp.mean(d * d, axis=-1, keepdims=True)
    return d * lax.rsqrt(var + EPS) * g + b


def _split_bf16(x):
    hi = x.astype(BF16)
    lo = (x - hi.astype(F32)).astype(BF16)
    return hi, lo


def _select_cols(x, e):
    hi, lo = _split_bf16(x)
    y = jnp.dot(jnp.concatenate([hi, lo], axis=0), e, preferred_element_type=F32)
    return y[:x.shape[0]] + y[x.shape[0]:]


def _select_rows(e, x):
    hi, lo = _split_bf16(x)
    y = jnp.dot(e, jnp.concatenate([hi, lo], axis=1), preferred_element_type=F32)
    return y[:, :x.shape[1]] + y[:, x.shape[1]:]


def _inproj_kernel(x_ref, w_ref, gain_ref, cos_ref, sin_ref, hmean_ref, q_ref, k_ref, v_ref, u_ref):
    xb = x_ref[...].astype(BF16)
    h = jnp.dot(xb, w_ref[...], preferred_element_type=F32)
    qk = h[:, :QK_WIDTH]
    ms = jnp.dot((qk * qk).astype(BF16), hmean_ref[...], preferred_element_type=F32)
    qkn = qk * lax.rsqrt(ms + EPS) * gain_ref[...]
    c = cos_ref[...]
    s = sin_ref[...]
    even = (lax.broadcasted_iota(jnp.int32, c.shape, 1) % 2) == 0
    outs = []
    for j in range(QK_WIDTH // LANES):
        xs = qkn[:, j * LANES:(j + 1) * LANES]
        nxt = pltpu.roll(xs, LANES - 1, 1)
        prv = pltpu.roll(xs, 1, 1)
        outs.append(xs * c + jnp.where(even, nxt, prv) * s)
    for j in range(ATTN_WIDTH // LANES):
        q_ref[:, j * LANES:(j + 1) * LANES] = outs[j].astype(BF16)
    k_ref[...] = outs[ATTN_WIDTH // LANES].astype(BF16)
    v_ref[...] = h[:, QK_WIDTH:QK_WIDTH + KV_WIDTH].T.astype(BF16)
    for r in range(u_ref.shape[1]):
        blk = h[r * LANES:(r + 1) * LANES, QK_WIDTH + KV_WIDTH:]
        u_ref[:, r, :] = blk.T


INPROJ_TILE = 1024


def _inproj(x2, w_in, gain, cos_t, sin_t, hmean, seq_len):
    n = x2.shape[0]
    tm = INPROJ_TILE
    nper = seq_len // tm
    return pl.pallas_call(
        _inproj_kernel,
        grid=(n // tm,),
        in_specs=[
            pl.BlockSpec((tm, D_MODEL), lambda i: (i, 0)),
            _const_spec((D_MODEL, AB_IN)),
            _const_spec((1, QK_WIDTH)),
            pl.BlockSpec((tm, LANES), lambda i: (i % nper, 0)),
            pl.BlockSpec((tm, LANES), lambda i: (i % nper, 0)),
            _const_spec((QK_WIDTH, QK_WIDTH)),
        ],
        out_specs=[
            pl.BlockSpec((tm, ATTN_WIDTH), lambda i: (i, 0)),
            pl.BlockSpec((tm, KV_WIDTH), lambda i: (i, 0)),
            pl.BlockSpec((KV_WIDTH, tm), lambda i: (0, i)),
            pl.BlockSpec((SSM_WIDTH, tm // LANES, LANES), lambda i: (0, i, 0)),
        ],
        out_shape=[
            jax.ShapeDtypeStruct((n, ATTN_WIDTH), BF16),
            jax.ShapeDtypeStruct((n, KV_WIDTH), BF16),
            jax.ShapeDtypeStruct((KV_WIDTH, n), BF16),
            jax.ShapeDtypeStruct((SSM_WIDTH, n // LANES, LANES), F32),
        ],
        compiler_params=_cparams(("parallel",)),
        name="inproj",
    )(x2, w_in, gain, cos_t, sin_t, hmean)


ATT_UNIT_Q = 64


ATT_SHIFT_LIMIT = 60.0


def _attn_kernel(bound_ref, q_ref, k_ref, vt_ref, o_ref,
                 s0_ref, s1_ref, p0_ref, p1_ref, m0_ref, m1_ref, l0_ref, l1_ref):
    seq_len = k_ref.shape[1]
    n_units = seq_len // ATT_UNIT_Q
    rep = N_HEADS // N_KV_HEADS
    lane = lax.broadcasted_iota(jnp.int32, (ATT_UNIT_Q, LANES), 1)
    lo_half = lane < HEAD_DIM
    zero = jnp.zeros((ATT_UNIT_Q, LANES), BF16)
    bound = bound_ref[0, 0]

    def rows_of(t):
        return pl.ds(pl.multiple_of(t * ATT_UNIT_Q, ATT_UNIT_Q), ATT_UNIT_Q)

    def raw_scores(t):
        q = q_ref[0, rows_of(t), :]
        slots = [q[:, m * LANES:(m + 1) * LANES] for m in range(rep)]
        qs = jnp.concatenate([jnp.where(lo_half, x, zero) for x in slots]
                             + [jnp.where(lo_half, zero, x) for x in slots], axis=0)
        return lax.dot_general(k_ref[0], qs, (((1,), (1,)), ((), ())), preferred_element_type=F32)

    def scores(t, s_ref, m_ref):
        s = raw_scores(t)
        s_ref[...] = s
        m_ref[...] = jnp.max(s, axis=0, keepdims=True)

    def probs(s_ref, m_ref, p_ref, l_ref):
        p = jnp.exp2(s_ref[...] - m_ref[...])
        l_ref[...] = jnp.sum(p, axis=0, keepdims=True)
        p_ref[...] = p.astype(BF16)

    def bounded_probs(t, p_ref, l_ref):
        p = jnp.exp2(raw_scores(t) - bound)
        l_ref[...] = jnp.sum(p, axis=0, keepdims=True)
        p_ref[...] = p.astype(BF16)

    def output(t, p_ref, l_ref):
        o_t = jnp.dot(vt_ref[...], p_ref[...], preferred_element_type=F32) / l_ref[...]
        o = o_t.T
        for m in range(rep):
            g0 = o[m * ATT_UNIT_Q:(m + 1) * ATT_UNIT_Q]
            g1 = o[(rep + m) * ATT_UNIT_Q:(rep + m + 1) * ATT_UNIT_Q]
            o_ref[0, rows_of(t), m * LANES:(m + 1) * LANES] = jnp.where(lo_half, g0, g1).astype(BF16)

    @pl.when(bound <= ATT_SHIFT_LIMIT)
    def _bounded_shift():
        bounded_probs(0, p0_ref, l0_ref)

        def body(i, carry):
            t = 2 * i + 1
            bounded_probs(t, p1_ref, l1_ref)
            output(t - 1, p0_ref, l0_ref)
            bounded_probs(t + 1, p0_ref, l0_ref)
            output(t, p1_ref, l1_ref)
            return carry

        lax.fori_loop(0, (n_units - 2) // 2, body, 0)
        bounded_probs(n_units - 1, p1_ref, l1_ref)
        output(n_units - 2, p0_ref, l0_ref)
        output(n_units - 1, p1_ref, l1_ref)

    @pl.when(bound > ATT_SHIFT_LIMIT)
    def _row_max_shift():
        scores(0, s0_ref, m0_ref)
        scores(1, s1_ref, m1_ref)
        probs(s0_ref, m0_ref, p0_ref, l0_ref)

        def body(i, carry):
            t = 2 * i + 2
            scores(t, s0_ref, m0_ref)
            probs(s1_ref, m1_ref, p1_ref, l1_ref)
            output(t - 2, p0_ref, l0_ref)
            scores(t + 1, s1_ref, m1_ref)
            probs(s0_ref, m0_ref, p0_ref, l0_ref)
            output(t - 1, p1_ref, l1_ref)
            return carry

        lax.fori_loop(0, (n_units - 2) // 2, body, 0)
        probs(s1_ref, m1_ref, p1_ref, l1_ref)
        output(n_units - 2, p0_ref, l0_ref)
        output(n_units - 1, p1_ref, l1_ref)


def _attention(bound, q, k, v_t):
    b, seq_len, _ = q.shape
    cols = ATT_UNIT_Q * N_HEADS
    seq = lambda w, **kw: pl.BlockSpec((1, seq_len, w), lambda i: (i, 0, 0), **kw)
    once = dict(pipeline_mode=pl.Buffered(1))
    return pl.pallas_call(
        _attn_kernel,
        grid=(b,),
        in_specs=[pl.BlockSpec(memory_space=pltpu.SMEM),
                  seq(ATTN_WIDTH, **once), seq(KV_WIDTH, **once),
                  pl.BlockSpec((KV_WIDTH, seq_len), lambda i: (0, i), **once)],
        out_specs=seq(ATTN_WIDTH),
        out_shape=jax.ShapeDtypeStruct((b, seq_len, ATTN_WIDTH), BF16),
        scratch_shapes=[pltpu.VMEM((seq_len, cols), F32), pltpu.VMEM((seq_len, cols), F32),
                        pltpu.VMEM((seq_len, cols), BF16), pltpu.VMEM((seq_len, cols), BF16),
                        pltpu.VMEM((1, cols), F32), pltpu.VMEM((1, cols), F32),
                        pltpu.VMEM((1, cols), F32), pltpu.VMEM((1, cols), F32)],
        compiler_params=_cparams(("parallel",)),
        name="attention",
    )(bound, q, k, v_t)


SSM_PITCH = 132


def _s5_kernel(xp_ref, xs_ref, pc_ref, pr2_ref, bcr_ref, bci_ref, ccat_ref, cr4_ref, ci4_ref, dt_ref,
               e16_ref, x16_ref, yp_ref, ys_ref, s_ref, zt_ref, m_ref, yt_ref, *, pairs):
    ns = SSM_STATE
    pc = pc_ref[0]
    are_c, aim_c, dt_c = pc[:, 0:1], pc[:, 1:2], jnp.exp(pc[:, 2:3])
    row = lax.broadcasted_iota(jnp.int32, (2 * ns, LANES), 0)
    lan = lax.broadcasted_iota(jnp.int32, (2 * ns, LANES), 1)
    b_rows = row >= ns
    n_g = jnp.where(b_rows, lan - (SSM_CHUNK - 1), (SSM_CHUNK - 1) - lan)
    mask = n_g >= 0
    nf = jnp.where(mask, n_g, 0).astype(F32)
    rho = jnp.exp(are_c * dt_c * nf)
    ang = aim_c * dt_c * nf
    cs = jnp.where(mask, rho * jnp.cos(ang), 0.0)
    sn = jnp.where(mask, rho * jnp.sin(ang), 0.0)
    b_col = b_rows[:, 0:1]
    a1r = jnp.where(b_col, cs[:, SSM_CHUNK:SSM_CHUNK + 1], cs[:, SSM_CHUNK - 2:SSM_CHUNK - 1]) - 1.0
    a1i = jnp.where(b_col, sn[:, SSM_CHUNK:SSM_CHUNK + 1], sn[:, SSM_CHUNK - 2:SSM_CHUNK - 1])
    den = are_c * are_c + aim_c * aim_c
    cr = (a1r * are_c + a1i * aim_c) / den
    ci = (a1i * are_c - a1r * aim_c) / den

    def tile_lanes(x):
        for sh in (SSM_GROUP, 2 * SSM_GROUP, 4 * SSM_GROUP):
            x = x + pltpu.roll(x, sh, 1)
        return jnp.concatenate([x] * (SSM_TAPS // LANES), axis=1)

    bre, bim = bcr_ref[0], bci_ref[0]
    bbr = tile_lanes(cr * bre - ci * bim)
    bbi = tile_lanes(cr * bim + ci * bre)
    base = jnp.concatenate([cs[:ns], sn[:ns], cs[ns:], sn[ns:]], axis=0)
    ex = _select_cols(base, e16_ref[...])
    ecf, esf, ecb, esb = ex[:ns], ex[ns:2 * ns], ex[2 * ns:3 * ns], ex[3 * ns:]
    z_fre = ecf * bbr[:ns] - esf * bbi[:ns]
    z_fim = esf * bbr[:ns] + ecf * bbi[:ns]
    z_bre = ecb * bbr[ns:] - esb * bbi[ns:]
    z_bim = esb * bbr[ns:] + ecb * bbi[ns:]

    g_tab = jnp.concatenate([z_fre, -z_fim, z_bre, -z_bim], axis=0)
    ch, cl = _split_bf16(ccat_ref[0])
    gh, gl = _split_bf16(g_tab)
    w = (jnp.dot(ch, gh, preferred_element_type=F32) + jnp.dot(ch, gl, preferred_element_type=F32)
         + jnp.dot(cl, gh, preferred_element_type=F32))
    wrow = lax.broadcasted_iota(jnp.int32, (SSM_GROUP, SSM_TAPS), 0)
    wcol = lax.broadcasted_iota(jnp.int32, (SSM_GROUP, SSM_TAPS), 1)
    on_diag = (wcol // SSM_GROUP == SSM_CHUNK - 1) & (wcol % SSM_GROUP == wrow)
    w = w + jnp.where(on_diag, dt_ref[0], 0.0)
    per_vreg = LANES // SSM_GROUP
    rolled = [w] + [pltpu.roll(w, SSM_TAPS - SSM_GROUP * kk, 1) for kk in range(1, per_vreg)]
    for j in range(SSM_CHUNK):
        r = SSM_CHUNK - 1 - j
        blk = rolled[r % per_vreg][:, (r // per_vreg) * LANES:(r // per_vreg) * LANES + SSM_ROW]
        m_ref[j * SSM_GROUP:(j + 1) * SSM_GROUP, :] = blk.astype(BF16)

    p2 = jnp.concatenate([z_fre[:, :SSM_ROW], z_fim[:, :SSM_ROW], z_bre[:, SSM_ROW:], z_bim[:, SSM_ROW:]],
                         axis=0).astype(BF16)

    pr2 = pr2_ref[0]
    j_r = lax.broadcasted_iota(jnp.int32, (SSM_CHUNK, LANES), 0)
    l_r = lax.broadcasted_iota(jnp.int32, (SSM_CHUNK, LANES), 1)
    lo_r = l_r < ns
    n_q = jnp.where(lo_r, j_r + 1, (SSM_CHUNK - 1) - j_r).astype(F32)
    dt2 = jnp.exp(pr2[2:3, :])
    rho = jnp.exp(pr2[0:1, :] * dt2 * n_q)
    ang = pr2[1:2, :] * dt2 * n_q
    c2 = rho * jnp.cos(ang)
    s2 = rho * jnp.sin(ang)
    base_q = jnp.concatenate([jnp.where(lo_r, c2, -pltpu.roll(s2, ns, 1)),
                              jnp.where(lo_r, pltpu.roll(c2, ns, 1), -s2)], axis=1)
    a1x = _select_rows(x16_ref[...], base_q)
    sign = jnp.where(lax.broadcasted_iota(jnp.int32, (SSM_ROW, LANES), 1) < ns, 1.0, -1.0)
    a2x = jnp.concatenate([pltpu.roll(a1x[:, :LANES], ns, 1) * sign,
                           pltpu.roll(a1x[:, LANES:], ns, 1) * sign], axis=1)
    q2 = (a1x * jnp.concatenate([cr4_ref[0]] * SSM_CHUNK, axis=0)
          + a2x * jnp.concatenate([ci4_ref[0]] * SSM_CHUNK, axis=0)).astype(BF16)

    n_s = (SSM_CHUNK * jnp.left_shift(1, jnp.minimum(lan, 8))).astype(F32)
    rho = jnp.exp(are_c * dt_c * n_s)
    ang = aim_c * dt_c * n_s
    sc_r = rho * jnp.cos(ang)
    sc_i = rho * jnp.sin(ang)

    for seg, x_ref in enumerate((xp_ref, xs_ref)):
        for hi in range(SSM_GROUP):
            s_ref[seg, hi * SSM_PITCH:hi * SSM_PITCH + LANES, :] = x_ref[hi].T
    for par in range(2):
        for i in range(SSM_CHUNK):
            for seg in range(2):
                rows16 = s_ref[seg, pl.ds(par * SSM_CHUNK + i, SSM_GROUP, stride=SSM_PITCH), :]
                blk = 2 * par + seg
                zt_ref[i * SSM_GROUP:(i + 1) * SSM_GROUP, blk * LANES:(blk + 1) * LANES] = rows16.astype(BF16)

    zt = zt_ref[...]
    yt = jnp.dot(m_ref[...], zt, preferred_element_type=F32)
    st = jnp.dot(p2, zt, preferred_element_type=F32)

    def cmul(kk, lo, v):
        ar, ai = sc_r[lo:lo + ns, kk:kk + 1], sc_i[lo:lo + ns, kk:kk + 1]
        return jnp.concatenate([ar * v[:ns] - ai * v[ns:], ar * v[ns:] + ai * v[:ns]], axis=0)

    carried = [None] * 4
    for seg, npairs in enumerate(pairs):
        c2i = lax.broadcasted_iota(jnp.int32, (2 * ns, LANES), 1) % npairs
        right = lambda v, d: jnp.where(c2i >= d, pltpu.roll(v, d, 1), 0.0)
        left = lambda v, d: jnp.where(c2i < npairs - d, pltpu.roll(v, LANES - d, 1), 0.0)
        ev, od = st[:, seg * LANES:(seg + 1) * LANES], st[:, (2 + seg) * LANES:(3 + seg) * LANES]
        ef, of, eb, ob = ev[:2 * ns], od[:2 * ns], ev[2 * ns:], od[2 * ns:]
        ef, of = ef + cmul(0, 0, right(of, 1)), of + cmul(0, 0, ef)
        eb, ob = eb + cmul(0, ns, ob), ob + cmul(0, ns, left(eb, 1))
        for kk in range(1, int(math.log2(npairs)) + 1):
            d = 1 << (kk - 1)
            ef, of = ef + cmul(kk, 0, right(ef, d)), of + cmul(kk, 0, right(of, d))
            eb, ob = eb + cmul(kk, ns, left(eb, d)), ob + cmul(kk, ns, left(ob, d))
        carried[seg] = jnp.concatenate([right(of, 1), ob], axis=0)
        carried[2 + seg] = jnp.concatenate([ef, left(eb, 1)], axis=0)
    xin = jnp.concatenate(carried, axis=1).astype(BF16)
    yt = yt + jnp.dot(q2, xin, preferred_element_type=F32)

    for blk in range(4):
        yt_ref[blk] = yt[:, blk * LANES:(blk + 1) * LANES]
    for seg, y_ref in enumerate((yp_ref, ys_ref)):
        for ho in range(SSM_GROUP):
            ev = yt_ref[seg, pl.ds(ho, SSM_CHUNK, stride=SSM_GROUP), :]
            od = yt_ref[2 + seg, pl.ds(ho, SSM_CHUNK, stride=SSM_GROUP), :]
            y_ref[ho] = jnp.concatenate([ev, od], axis=0).T


def _s5(u_p, u_s, pc, pr2, bcr, bci, ccat, cr4, ci4, dtile, e16, x16, pairs):
    g = SSM_GROUPS
    rows = u_p.shape[1]
    assert u_s.shape[1] == rows == LANES
    grp = lambda shape: pl.BlockSpec((1,) + shape, lambda i: (i, 0, 0))
    chan = pl.BlockSpec((SSM_GROUP, rows, LANES), lambda i: (i, 0, 0))
    return pl.pallas_call(
        functools.partial(_s5_kernel, pairs=pairs),
        grid=(g,),
        in_specs=[chan, chan, grp((2 * SSM_STATE, 8)), grp((8, LANES)),
                  grp((2 * SSM_STATE, LANES)), grp((2 * SSM_STATE, LANES)),
                  grp((SSM_GROUP, SSM_SDIM)), grp((SSM_GROUP, SSM_SDIM)), grp((SSM_GROUP, SSM_SDIM)),
                  grp((1, SSM_TAPS)), _const_spec((LANES, SSM_TAPS)), _const_spec((SSM_ROW, SSM_CHUNK))],
        out_specs=[chan, chan],
        out_shape=[jax.ShapeDtypeStruct(u_p.shape, F32), jax.ShapeDtypeStruct(u_s.shape, F32)],
        scratch_shapes=[pltpu.VMEM((2, SSM_GROUP * SSM_PITCH, LANES), F32),
                        pltpu.VMEM((SSM_ROW, 4 * LANES), BF16),
                        pltpu.VMEM((SSM_ROW, SSM_ROW), BF16),
                        pltpu.VMEM((4, SSM_ROW, LANES), F32)],
        compiler_params=_cparams(("parallel",)),
        name="s5",
    )(u_p, u_s, pc, pr2, bcr, bci, ccat, cr4, ci4, dtile, e16, x16)


FF_CHUNK = 1024


def _ffn_ln(x, w1_ref, w2_ref, g, b):
    xb = x.astype(BF16)
    acc = ALPHA * x
    for c in range(D_FF // FF_CHUNK):
        h = jnp.dot(xb, w1_ref[0, :, c * FF_CHUNK:(c + 1) * FF_CHUNK], preferred_element_type=F32)
        h = jnp.maximum(h, 0.0)
        acc = acc + jnp.dot((h * h).astype(BF16), w2_ref[0, c * FF_CHUNK:(c + 1) * FF_CHUNK, :],
                            preferred_element_type=F32)
    return _layer_norm(acc, g, b)


def _layer0_tail_kernel(x_ref, a_ref, y_ref, wg_ref, bg_ref, woa_ref, wos_ref, ln_ref, w1_ref, w2_ref, o_ref):
    rows_per_tile = x_ref.shape[0] // LANES
    row0 = (pl.program_id(0) % 2) * rows_per_tile
    y_tok = jnp.concatenate([y_ref[:, row0 + r, :].T for r in range(rows_per_tile)], axis=0)
    y = jax.nn.gelu(y_tok)
    gate = jnp.dot(y.astype(BF16), wg_ref[...], preferred_element_type=F32) + bg_ref[...]
    zs = (y * jax.nn.sigmoid(gate)).astype(BF16)
    mix = (jnp.dot(a_ref[...], woa_ref[...], preferred_element_type=F32)
           + jnp.dot(zs, wos_ref[...], preferred_element_type=F32))
    ln = ln_ref[...]
    x1 = _layer_norm(ALPHA * x_ref[...] + mix, ln[0:1], ln[1:2])
    o_ref[...] = _ffn_ln(x1, w1_ref, w2_ref, ln[2:3], ln[3:4])


def _layer_weights_spec(shape, layer):
    return pl.BlockSpec((1,) + shape, lambda i: (layer, 0, 0), pipeline_mode=pl.Buffered(1))


def _layer0_tail(x2, attn, y, w_glu, b_glu, w_out_a, w_out_s, ln4, w1, w2):
    n = x2.shape[0]
    tm = TOKEN_TILE
    tok = lambda w: pl.BlockSpec((tm, w), lambda i: (i, 0))
    return pl.pallas_call(
        _layer0_tail_kernel,
        grid=(n // tm,),
        in_specs=[tok(D_MODEL), tok(ATTN_WIDTH),
                  pl.BlockSpec((SSM_WIDTH, 2 * tm // LANES, LANES), lambda i: (0, i // 2, 0)),
                  _const_spec((SSM_WIDTH, SSM_WIDTH)), _const_spec((1, SSM_WIDTH)),
                  _const_spec((ATTN_WIDTH, D_MODEL)), _const_spec((SSM_WIDTH, D_MODEL)),
                  _const_spec((4, D_MODEL)),
                  _layer_weights_spec((D_MODEL, D_FF), 0), _layer_weights_spec((D_FF, D_MODEL), 0)],
        out_specs=tok(D_MODEL),
        out_shape=jax.ShapeDtypeStruct((n, D_MODEL), F32),
        compiler_params=_cparams(("parallel",)),
        name="layer0_tail",
    )(x2, attn, y, w_glu, b_glu, w_out_a, w_out_s, ln4, w1, w2)


def _layer1_kernel(x_ref, win_ref, cln_ref, ws_ref, bs_ref, wout_ref, ln_ref, w1_ref, w2_ref, o_ref):
    x = x_ref[...]
    tm = x.shape[0]
    h = jax.nn.gelu(jnp.dot(x.astype(BF16), win_ref[...], preferred_element_type=F32))
    u = h[:, :D_MODEL]
    cln = cln_ref[...]
    vb = _layer_norm(h[:, D_MODEL:], cln[0:1], cln[1:2]).astype(BF16)
    rows = []
    for c in range(tm // SGU_CHUNK):
        cols = []
        for hd in range(SGU_HEADS):
            blk = vb[c * SGU_CHUNK:(c + 1) * SGU_CHUNK, hd * LANES:(hd + 1) * LANES]
            cols.append(jnp.dot(ws_ref[hd], blk, preferred_element_type=F32) + bs_ref[hd])
        rows.append(jnp.concatenate(cols, axis=1))
    mixed = jnp.concatenate(rows, axis=0)
    out = jnp.dot((u * mixed).astype(BF16), wout_ref[...], preferred_element_type=F32)
    ln = ln_ref[...]
    x1 = _layer_norm(ALPHA * x + out, ln[0:1], ln[1:2])
    o_ref[...] = _ffn_ln(x1, w1_ref, w2_ref, ln[2:3], ln[3:4])


def _layer1(x2, w_in, cln2, w_s, b_s_full, w_out, ln4, w1, w2):
    n = x2.shape[0]
    tm = TOKEN_TILE
    tok = pl.BlockSpec((tm, D_MODEL), lambda i: (i, 0))
    return pl.pallas_call(
        _layer1_kernel,
        grid=(n // tm,),
        in_specs=[tok, _const_spec((D_MODEL, 2 * D_MODEL)), _const_spec((2, D_MODEL)),
                  _const_spec((SGU_HEADS, SGU_CHUNK, SGU_CHUNK)), _const_spec((SGU_HEADS, SGU_CHUNK, LANES)),
                  _const_spec((D_MODEL, D_MODEL)), _const_spec((4, D_MODEL)),
                  _layer_weights_spec((D_MODEL, D_FF), 1), _layer_weights_spec((D_FF, D_MODEL), 1)],
        out_specs=tok,
        out_shape=jax.ShapeDtypeStruct((n, D_MODEL), F32),
        compiler_params=_cparams(("parallel",)),
        name="layer1",
    )(x2, w_in, cln2, w_s, b_s_full, w_out, ln4, w1, w2)


_HEAD_OF_SLOT = np.array([0, 4, 1, 5, 2, 6, 3, 7])


def _rope_tables(seq_len):
    t = np.arange(seq_len)
    row = (t // GRID_W).astype(np.float32)
    col = (t % GRID_W).astype(np.float32)
    axis_dim = HEAD_DIM // 2
    inv_freq = ROPE_THETA ** (-jnp.arange(0, axis_dim, 2, dtype=F32) / axis_dim)
    ang = jnp.concatenate([jnp.asarray(row)[:, None] * inv_freq, jnp.asarray(col)[:, None] * inv_freq], axis=-1)
    cos = jnp.repeat(jnp.cos(ang), 2, axis=-1)
    sin = jnp.repeat(jnp.sin(ang), 2, axis=-1)
    sign = jnp.tile(jnp.asarray([-1.0, 1.0], F32), HEAD_DIM // 2)
    reps = LANES // HEAD_DIM
    return jnp.tile(cos, (1, reps)), jnp.tile(sin * sign, (1, reps))


def _selectors():
    col = np.arange(SSM_TAPS)
    row = np.arange(LANES)
    e16 = (col[None, :] // SSM_GROUP == row[:, None])
    r2 = np.arange(SSM_ROW)
    x16 = (r2[:, None] // SSM_GROUP == np.arange(SSM_CHUNK)[None, :])
    as_bf16 = lambda a: jnp.asarray(a.astype(np.float32), BF16)
    return as_bf16(e16), as_bf16(x16)


def _s5_param_layout(a_re_f, a_im_f, ls_f, a_re_b, a_im_b, ls_b, b_re, b_im, c_re, c_im, d):
    g, ns = SSM_GROUPS, SSM_STATE
    zeros = jnp.zeros((g, ns), F32)
    per = lambda a_re, a_im, ls: jnp.stack([a_re, a_im, jnp.broadcast_to(ls[:, None], (g, ns))] + [zeros] * 5,
                                           axis=1)
    pr2 = jnp.concatenate([per(a_re_f, a_im_f, ls_f), per(a_re_b, a_im_b, ls_b)], axis=2)
    pc = jnp.transpose(pr2, (0, 2, 1))
    pad_lanes = lambda b: jnp.pad(jnp.concatenate([b, b], axis=1), ((0, 0), (0, 0), (0, LANES - SSM_GROUP)))
    ccat = jnp.concatenate([c_re, c_im, c_re, c_im], axis=2)
    cr4 = jnp.concatenate([c_re] * 4, axis=2)
    ci4 = jnp.concatenate([c_im] * 4, axis=2)
    dtile = jnp.tile(d.reshape(g, 1, SSM_GROUP), (1, 1, SSM_LAGS))
    return pc, pr2, pad_lanes(b_re), pad_lanes(b_im), ccat, cr4, ci4, dtile


def kernel(x_prompt, x_sample, ab_w_in, ab_q_norm, ab_k_norm, ssm_a_re_f, ssm_a_im_f, ssm_log_step_f,
           ssm_a_re_b, ssm_a_im_b, ssm_log_step_b, ssm_b_re, ssm_b_im, ssm_c_re, ssm_c_im, ssm_d, ssm_w_glu,
           ssm_b_glu, ab_w_out, c_w_in, c_ln_g, c_ln_b, c_w_s, c_b_s, c_w_out, ln_mix_g, ln_mix_b, ff_w1, ff_w2,
           ln_ff_g, ln_ff_b):
    xs = [x_prompt, x_sample]
    row2 = lambda v: v.reshape(1, -1)

    slot_cols = (_HEAD_OF_SLOT[:, None] * HEAD_DIM + np.arange(HEAD_DIM)[None, :]).reshape(-1)
    w_in = ab_w_in[0]
    w_in = jnp.concatenate([w_in[:, slot_cols], w_in[:, ATTN_WIDTH:]], axis=1).astype(BF16)
    scale = HEAD_DIM ** -0.5 * math.log2(math.e)
    gain = jnp.concatenate([jnp.tile(ab_q_norm[0] * scale, N_HEADS), jnp.tile(ab_k_norm[0], N_KV_HEADS)])
    score_bound = (1.05 * HEAD_DIM * scale * jnp.max(jnp.abs(ab_q_norm[0])) * jnp.max(jnp.abs(ab_k_norm[0])))
    score_bound = score_bound.reshape(1, 1).astype(F32)
    hid = np.arange(QK_WIDTH) // HEAD_DIM
    hmean = jnp.asarray((hid[:, None] == hid[None, :]).astype(np.float32) / HEAD_DIM, BF16)
    w_out = ab_w_out[0]
    w_out_a = w_out[:ATTN_WIDTH][slot_cols].astype(BF16)
    w_out_s = w_out[ATTN_WIDTH:].astype(BF16)
    w_glu = ssm_w_glu[0].astype(BF16)

    attn, us, pairs = [], [], []
    for x in xs:
        b, seq_len, _ = x.shape
        cos_t, sin_t = _rope_tables(seq_len)
        q, k, v_t, u = _inproj(x.reshape(b * seq_len, D_MODEL), w_in, row2(gain), cos_t, sin_t, hmean, seq_len)
        attn.append(_attention(score_bound, q.reshape(b, seq_len, ATTN_WIDTH), k.reshape(b, seq_len, KV_WIDTH),
                               v_t).reshape(b * seq_len, ATTN_WIDTH))
        us.append(u)
        pairs.append(seq_len // (2 * SSM_CHUNK))
    ys = _s5(us[0], us[1], *_s5_param_layout(
        ssm_a_re_f[0], ssm_a_im_f[0], ssm_log_step_f[0], ssm_a_re_b[0], ssm_a_im_b[0], ssm_log_step_b[0],
        ssm_b_re[0], ssm_b_im[0], ssm_c_re[0], ssm_c_im[0], ssm_d[0]), *_selectors(), tuple(pairs))

    ff_w1b = ff_w1.astype(BF16)
    ff_w2b = ff_w2.astype(BF16)
    ln4 = [jnp.stack([ln_mix_g[i], ln_mix_b[i], ln_ff_g[i], ln_ff_b[i]]) for i in range(DEPTH)]
    cln2 = jnp.stack([c_ln_g[0], c_ln_b[0]])
    bs_full = jnp.broadcast_to(c_b_s[0][:, :, None], (SGU_HEADS, SGU_CHUNK, LANES))
    c_w_in_b, c_w_s_b, c_w_out_b = c_w_in[0].astype(BF16), c_w_s[0].astype(BF16), c_w_out[0].astype(BF16)
    outs = []
    for x, a, y in zip(xs, attn, ys):
        b, seq_len, _ = x.shape
        n = b * seq_len
        h = _layer0_tail(x.reshape(n, D_MODEL), a, y, w_glu, row2(ssm_b_glu[0]), w_out_a, w_out_s,
                         ln4[0], ff_w1b, ff_w2b)
        h = _layer1(h, c_w_in_b, cln2, c_w_s_b, bs_full, c_w_out_b, ln4[1], ff_w1b, ff_w2b)
        outs.append(h.reshape(b, seq_len, D_MODEL))
    return tuple(outs)
```

```python
import functools
import math

import numpy as np
import jax
import jax.numpy as jnp
from jax import lax
from jax.experimental import pallas as pl
from jax.experimental.pallas import tpu as pltpu

F32 = jnp.float32
BF16 = jnp.bfloat16

D_MODEL = 1024
GRID_W = 64
N_HEADS = 8
N_KV_HEADS = 2
HEAD_DIM = 64
ATTN_WIDTH = N_HEADS * HEAD_DIM
KV_WIDTH = N_KV_HEADS * HEAD_DIM
QK_WIDTH = ATTN_WIDTH + KV_WIDTH
ROPE_THETA = 10000.0
SSM_WIDTH = D_MODEL - ATTN_WIDTH
SSM_GROUP = 16
SSM_GROUPS = SSM_WIDTH // SSM_GROUP
SSM_STATE = 64
SGU_HEADS = 8
SGU_CHUNK = 128
D_FF = 4 * D_MODEL
DEPTH = 2
AB_IN = ATTN_WIDTH + 2 * KV_WIDTH + SSM_WIDTH
ALPHA = (2 * DEPTH) ** 0.25
EPS = 1e-6

LANES = 128
VMEM_LIMIT_BYTES = 56 * 1024 * 1024

SSM_CHUNK = 64
SSM_ROW = SSM_CHUNK * SSM_GROUP
SSM_LAGS = 2 * SSM_CHUNK
SSM_TAPS = SSM_LAGS * SSM_GROUP
SSM_SDIM = 4 * SSM_STATE

TOKEN_TILE = 512


def _cparams(sem):
    return pltpu.CompilerParams(dimension_semantics=sem, vmem_limit_bytes=VMEM_LIMIT_BYTES)


def _const_spec(shape):
    nd = len(shape)
    return pl.BlockSpec(shape, lambda *_: (0,) * nd, pipeline_mode=pl.Buffered(1))


def _layer_norm(r, g, b):
    mu = jnp.mean(r, axis=-1, keepdims=True)
    d = r - mu
    var = jnp.mean(d * d, axis=-1, keepdims=True)
    return d * lax.rsqrt(var + EPS) * g + b


def _split_bf16(x):
    hi = x.astype(BF16)
    lo = (x - hi.astype(F32)).astype(BF16)
    return hi, lo


def _select_cols(x, e):
    hi, lo = _split_bf16(x)
    y = jnp.dot(jnp.concatenate([hi, lo], axis=0), e, preferred_element_type=F32)
    return y[:x.shape[0]] + y[x.shape[0]:]


def _select_rows(e, x):
    hi, lo = _split_bf16(x)
    y = jnp.dot(e, jnp.concatenate([hi, lo], axis=1), preferred_element_type=F32)
    return y[:, :x.shape[1]] + y[:, x.shape[1]:]


def _inproj_kernel(x_ref, w_ref, gain_ref, cos_ref, sin_ref, hmean_ref, q_ref, k_ref, v_ref, u_ref):
    subs = [slice(r0, r0 + INPROJ_SUB) for r0 in range(0, x_ref.shape[0], INPROJ_SUB)]
    hs = [jnp.dot(x_ref[sl, :].astype(BF16), w_ref[...], preferred_element_type=F32) for sl in subs]
    mss = [jnp.dot((h[:, :QK_WIDTH] * h[:, :QK_WIDTH]).astype(BF16), hmean_ref[...], preferred_element_type=F32)
           for h in hs]
    qkns = [h[:, :QK_WIDTH] * lax.rsqrt(ms + EPS) * gain_ref[...] for h, ms in zip(hs, mss)]
    even = (lax.broadcasted_iota(jnp.int32, (INPROJ_SUB, LANES), 1) % 2) == 0
    for sl, qkn in zip(subs, qkns):
        c = cos_ref[sl, :]
        s = sin_ref[sl, :]
        for j in range(QK_WIDTH // LANES):
            xs = qkn[:, j * LANES:(j + 1) * LANES]
            nxt = pltpu.roll(xs, LANES - 1, 1)
            prv = pltpu.roll(xs, 1, 1)
            roped = (xs * c + jnp.where(even, nxt, prv) * s).astype(BF16)
            if j < ATTN_WIDTH // LANES:
                q_ref[sl, j * LANES:(j + 1) * LANES] = roped
            else:
                k_ref[sl, :] = roped
    for sl, h in zip(subs, hs):
        v_ref[:, sl] = h[:, QK_WIDTH:QK_WIDTH + KV_WIDTH].T.astype(BF16)
        for r in range(INPROJ_SUB // LANES):
            blk = h[r * LANES:(r + 1) * LANES, QK_WIDTH + KV_WIDTH:]
            u_ref[:, sl.start // LANES + r, :] = blk.T


INPROJ_TILE = 1024
INPROJ_SUB = 256


def _inproj(x2, w_in, gain, cos_t, sin_t, hmean, seq_len):
    n = x2.shape[0]
    tm = INPROJ_TILE
    nper = seq_len // tm
    return pl.pallas_call(
        _inproj_kernel,
        grid=(n // tm,),
        in_specs=[
            pl.BlockSpec((tm, D_MODEL), lambda i: (i, 0)),
            _const_spec((D_MODEL, AB_IN)),
            _const_spec((1, QK_WIDTH)),
            pl.BlockSpec((tm, LANES), lambda i: (i % nper, 0)),
            pl.BlockSpec((tm, LANES), lambda i: (i % nper, 0)),
            _const_spec((QK_WIDTH, QK_WIDTH)),
        ],
        out_specs=[
            pl.BlockSpec((tm, ATTN_WIDTH), lambda i: (i, 0)),
            pl.BlockSpec((tm, KV_WIDTH), lambda i: (i, 0)),
            pl.BlockSpec((KV_WIDTH, tm), lambda i: (0, i)),
            pl.BlockSpec((SSM_WIDTH, tm // LANES, LANES), lambda i: (0, i, 0)),
        ],
        out_shape=[
            jax.ShapeDtypeStruct((n, ATTN_WIDTH), BF16),
            jax.ShapeDtypeStruct((n, KV_WIDTH), BF16),
            jax.ShapeDtypeStruct((KV_WIDTH, n), BF16),
            jax.ShapeDtypeStruct((SSM_WIDTH, n // LANES, LANES), F32),
        ],
        compiler_params=_cparams(("parallel",)),
        name="inproj",
    )(x2, w_in, gain, cos_t, sin_t, hmean)


ATT_UNIT_Q = 64


ATT_SHIFT_LIMIT = 60.0


def _attn_kernel(bound_ref, q_ref, k_ref, vt_ref, o_ref,
                 s0_ref, s1_ref, p0_ref, p1_ref, m0_ref, m1_ref, l0_ref, l1_ref):
    seq_len = k_ref.shape[1]
    n_units = seq_len // ATT_UNIT_Q
    rep = N_HEADS // N_KV_HEADS
    lane = lax.broadcasted_iota(jnp.int32, (ATT_UNIT_Q, LANES), 1)
    lo_half = lane < HEAD_DIM
    zero = jnp.zeros((ATT_UNIT_Q, LANES), BF16)
    bound = bound_ref[0, 0]

    def rows_of(t):
        return pl.ds(pl.multiple_of(t * ATT_UNIT_Q, ATT_UNIT_Q), ATT_UNIT_Q)

    def raw_scores(t):
        q = q_ref[0, rows_of(t), :]
        slots = [q[:, m * LANES:(m + 1) * LANES] for m in range(rep)]
        qs = jnp.concatenate([jnp.where(lo_half, x, zero) for x in slots]
                             + [jnp.where(lo_half, zero, x) for x in slots], axis=0)
        return lax.dot_general(k_ref[0], qs, (((1,), (1,)), ((), ())), preferred_element_type=F32)

    def scores(t, s_ref, m_ref):
        s = raw_scores(t)
        s_ref[...] = s
        m_ref[...] = jnp.max(s, axis=0, keepdims=True)

    def probs(s_ref, m_ref, p_ref, l_ref):
        p = jnp.exp2(s_ref[...] - m_ref[...])
        l_ref[...] = jnp.sum(p, axis=0, keepdims=True)
        p_ref[...] = p.astype(BF16)

    def bounded_probs(t, p_ref, l_ref):
        p = jnp.exp2(raw_scores(t) - bound)
        l_ref[...] = jnp.sum(p, axis=0, keepdims=True)
        p_ref[...] = p.astype(BF16)

    def output(t, p_ref, l_ref):
        o_t = jnp.dot(vt_ref[...], p_ref[...], preferred_element_type=F32) / l_ref[...]
        o = o_t.T
        for m in range(rep):
            g0 = o[m * ATT_UNIT_Q:(m + 1) * ATT_UNIT_Q]
            g1 = o[(rep + m) * ATT_UNIT_Q:(rep + m + 1) * ATT_UNIT_Q]
            o_ref[0, rows_of(t), m * LANES:(m + 1) * LANES] = jnp.where(lo_half, g0, g1).astype(BF16)

    @pl.when(bound <= ATT_SHIFT_LIMIT)
    def _bounded_shift():
        bounded_probs(0, p0_ref, l0_ref)

        def body(i, carry):
            t = 2 * i + 1
            bounded_probs(t, p1_ref, l1_ref)
            output(t - 1, p0_ref, l0_ref)
            bounded_probs(t + 1, p0_ref, l0_ref)
            output(t, p1_ref, l1_ref)
            return carry

        lax.fori_loop(0, (n_units - 2) // 2, body, 0)
        bounded_probs(n_units - 1, p1_ref, l1_ref)
        output(n_units - 2, p0_ref, l0_ref)
        output(n_units - 1, p1_ref, l1_ref)

    @pl.when(bound > ATT_SHIFT_LIMIT)
    def _row_max_shift():
        scores(0, s0_ref, m0_ref)
        scores(1, s1_ref, m1_ref)
        probs(s0_ref, m0_ref, p0_ref, l0_ref)

        def body(i, carry):
            t = 2 * i + 2
            scores(t, s0_ref, m0_ref)
            probs(s1_ref, m1_ref, p1_ref, l1_ref)
            output(t - 2, p0_ref, l0_ref)
            scores(t + 1, s1_ref, m1_ref)
            probs(s0_ref, m0_ref, p0_ref, l0_ref)
            output(t - 1, p1_ref, l1_ref)
            return carry

        lax.fori_loop(0, (n_units - 2) // 2, body, 0)
        probs(s1_ref, m1_ref, p1_ref, l1_ref)
        output(n_units - 2, p0_ref, l0_ref)
        output(n_units - 1, p1_ref, l1_ref)


def _attention(bound, q, k, v_t):
    b, seq_len, _ = q.shape
    cols = ATT_UNIT_Q * N_HEADS
    seq = lambda w, **kw: pl.BlockSpec((1, seq_len, w), lambda i: (i, 0, 0), **kw)
    once = dict(pipeline_mode=pl.Buffered(1))
    return pl.pallas_call(
        _attn_kernel,
        grid=(b,),
        in_specs=[pl.BlockSpec(memory_space=pltpu.SMEM),
                  seq(ATTN_WIDTH, **once), seq(KV_WIDTH, **once),
                  pl.BlockSpec((KV_WIDTH, seq_len), lambda i: (0, i), **once)],
        out_specs=seq(ATTN_WIDTH),
        out_shape=jax.ShapeDtypeStruct((b, seq_len, ATTN_WIDTH), BF16),
        scratch_shapes=[pltpu.VMEM((seq_len, cols), F32), pltpu.VMEM((seq_len, cols), F32),
                        pltpu.VMEM((seq_len, cols), BF16), pltpu.VMEM((seq_len, cols), BF16),
                        pltpu.VMEM((1, cols), F32), pltpu.VMEM((1, cols), F32),
                        pltpu.VMEM((1, cols), F32), pltpu.VMEM((1, cols), F32)],
        compiler_params=_cparams(("parallel",)),
        name="attention",
    )(bound, q, k, v_t)


SSM_PITCH = 132


def _s5_kernel(xp_ref, xs_ref, pc_ref, pr2_ref, bcr_ref, bci_ref, ccat_ref, cr4_ref, ci4_ref, dt_ref,
               e16_ref, x16_ref, yp_ref, ys_ref, s_ref, zt_ref, m_ref, yt_ref, *, pairs):
    ns = SSM_STATE
    pc = pc_ref[0]
    are_c, aim_c, dt_c = pc[:, 0:1], pc[:, 1:2], jnp.exp(pc[:, 2:3])
    row = lax.broadcasted_iota(jnp.int32, (2 * ns, LANES), 0)
    lan = lax.broadcasted_iota(jnp.int32, (2 * ns, LANES), 1)
    b_rows = row >= ns
    n_g = jnp.where(b_rows, lan - (SSM_CHUNK - 1), (SSM_CHUNK - 1) - lan)
    mask = n_g >= 0
    nf = jnp.where(mask, n_g, 0).astype(F32)
    rho = jnp.exp(are_c * dt_c * nf)
    ang = aim_c * dt_c * nf
    cs = jnp.where(mask, rho * jnp.cos(ang), 0.0)
    sn = jnp.where(mask, rho * jnp.sin(ang), 0.0)
    b_col = b_rows[:, 0:1]
    a1r = jnp.where(b_col, cs[:, SSM_CHUNK:SSM_CHUNK + 1], cs[:, SSM_CHUNK - 2:SSM_CHUNK - 1]) - 1.0
    a1i = jnp.where(b_col, sn[:, SSM_CHUNK:SSM_CHUNK + 1], sn[:, SSM_CHUNK - 2:SSM_CHUNK - 1])
    den = are_c * are_c + aim_c * aim_c
    cr = (a1r * are_c + a1i * aim_c) / den
    ci = (a1i * are_c - a1r * aim_c) / den

    def tile_lanes(x):
        for sh in (SSM_GROUP, 2 * SSM_GROUP, 4 * SSM_GROUP):
            x = x + pltpu.roll(x, sh, 1)
        return jnp.concatenate([x] * (SSM_TAPS // LANES), axis=1)

    bre, bim = bcr_ref[0], bci_ref[0]
    bbr = tile_lanes(cr * bre - ci * bim)
    bbi = tile_lanes(cr * bim + ci * bre)
    base = jnp.concatenate([cs[:ns], sn[:ns], cs[ns:], sn[ns:]], axis=0)
    ex = _select_cols(base, e16_ref[...])
    ecf, esf, ecb, esb = ex[:ns], ex[ns:2 * ns], ex[2 * ns:3 * ns], ex[3 * ns:]
    z_fre = ecf * bbr[:ns] - esf * bbi[:ns]
    z_fim = esf * bbr[:ns] + ecf * bbi[:ns]
    z_bre = ecb * bbr[ns:] - esb * bbi[ns:]
    z_bim = esb * bbr[ns:] + ecb * bbi[ns:]

    g_tab = jnp.concatenate([z_fre, -z_fim, z_bre, -z_bim], axis=0)
    ch, cl = _split_bf16(ccat_ref[0])
    gh, gl = _split_bf16(g_tab)
    w = (jnp.dot(ch, gh, preferred_element_type=F32) + jnp.dot(ch, gl, preferred_element_type=F32)
         + jnp.dot(cl, gh, preferred_element_type=F32))
    wrow = lax.broadcasted_iota(jnp.int32, (SSM_GROUP, SSM_TAPS), 0)
    wcol = lax.broadcasted_iota(jnp.int32, (SSM_GROUP, SSM_TAPS), 1)
    on_diag = (wcol // SSM_GROUP == SSM_CHUNK - 1) & (wcol % SSM_GROUP == wrow)
    w = w + jnp.where(on_diag, dt_ref[0], 0.0)
    per_vreg = LANES // SSM_GROUP
    rolled = [w] + [pltpu.roll(w, SSM_TAPS - SSM_GROUP * kk, 1) for kk in range(1, per_vreg)]
    for j in range(SSM_CHUNK):
        r = SSM_CHUNK - 1 - j
        blk = rolled[r % per_vreg][:, (r // per_vreg) * LANES:(r // per_vreg) * LANES + SSM_ROW]
        m_ref[j * SSM_GROUP:(j + 1) * SSM_GROUP, :] = blk.astype(BF16)

    p2 = jnp.concatenate([z_fre[:, :SSM_ROW], z_fim[:, :SSM_ROW], z_bre[:, SSM_ROW:], z_bim[:, SSM_ROW:]],
                         axis=0).astype(BF16)

    pr2 = pr2_ref[0]
    j_r = lax.broadcasted_iota(jnp.int32, (SSM_CHUNK, LANES), 0)
    l_r = lax.broadcasted_iota(jnp.int32, (SSM_CHUNK, LANES), 1)
    lo_r = l_r < ns
    n_q = jnp.where(lo_r, j_r + 1, (SSM_CHUNK - 1) - j_r).astype(F32)
    dt2 = jnp.exp(pr2[2:3, :])
    rho = jnp.exp(pr2[0:1, :] * dt2 * n_q)
    ang = pr2[1:2, :] * dt2 * n_q
    c2 = rho * jnp.cos(ang)
    s2 = rho * jnp.sin(ang)
    base_q = jnp.concatenate([jnp.where(lo_r, c2, -pltpu.roll(s2, ns, 1)),
                              jnp.where(lo_r, pltpu.roll(c2, ns, 1), -s2)], axis=1)
    a1x = _select_rows(x16_ref[...], base_q)
    sign = jnp.where(lax.broadcasted_iota(jnp.int32, (SSM_ROW, LANES), 1) < ns, 1.0, -1.0)
    a2x = jnp.concatenate([pltpu.roll(a1x[:, :LANES], ns, 1) * sign,
                           pltpu.roll(a1x[:, LANES:], ns, 1) * sign], axis=1)
    q2 = (a1x * jnp.concatenate([cr4_ref[0]] * SSM_CHUNK, axis=0)
          + a2x * jnp.concatenate([ci4_ref[0]] * SSM_CHUNK, axis=0)).astype(BF16)

    n_s = (SSM_CHUNK * jnp.left_shift(1, jnp.minimum(lan, 8))).astype(F32)
    rho = jnp.exp(are_c * dt_c * n_s)
    ang = aim_c * dt_c * n_s
    sc_r = rho * jnp.cos(ang)
    sc_i = rho * jnp.sin(ang)

    for seg, x_ref in enumerate((xp_ref, xs_ref)):
        for hi in range(SSM_GROUP):
            s_ref[seg, hi * SSM_PITCH:hi * SSM_PITCH + LANES, :] = x_ref[hi].T
    for par in range(2):
        for i in range(SSM_CHUNK):
            for seg in range(2):
                rows16 = s_ref[seg, pl.ds(par * SSM_CHUNK + i, SSM_GROUP, stride=SSM_PITCH), :]
                blk = 2 * par + seg
                zt_ref[i * SSM_GROUP:(i + 1) * SSM_GROUP, blk * LANES:(blk + 1) * LANES] = rows16.astype(BF16)

    zt = zt_ref[...]
    yt = jnp.dot(m_ref[...], zt, preferred_element_type=F32)
    st = jnp.dot(p2, zt, preferred_element_type=F32)

    def cmul(kk, lo, v):
        ar, ai = sc_r[lo:lo + ns, kk:kk + 1], sc_i[lo:lo + ns, kk:kk + 1]
        return jnp.concatenate([ar * v[:ns] - ai * v[ns:], ar * v[ns:] + ai * v[:ns]], axis=0)

    carried = [None] * 4
    for seg, npairs in enumerate(pairs):
        c2i = lax.broadcasted_iota(jnp.int32, (2 * ns, LANES), 1) % npairs
        right = lambda v, d: jnp.where(c2i >= d, pltpu.roll(v, d, 1), 0.0)
        left = lambda v, d: jnp.where(c2i < npairs - d, pltpu.roll(v, LANES - d, 1), 0.0)
        ev, od = st[:, seg * LANES:(seg + 1) * LANES], st[:, (2 + seg) * LANES:(3 + seg) * LANES]
        ef, of, eb, ob = ev[:2 * ns], od[:2 * ns], ev[2 * ns:], od[2 * ns:]
        ef, of = ef + cmul(0, 0, right(of, 1)), of + cmul(0, 0, ef)
        eb, ob = eb + cmul(0, ns, ob), ob + cmul(0, ns, left(eb, 1))
        for kk in range(1, int(math.log2(npairs)) + 1):
            d = 1 << (kk - 1)
            ef, of = ef + cmul(kk, 0, right(ef, d)), of + cmul(kk, 0, right(of, d))
            eb, ob = eb + cmul(kk, ns, left(eb, d)), ob + cmul(kk, ns, left(ob, d))
        carried[seg] = jnp.concatenate([right(of, 1), ob], axis=0)
        carried[2 + seg] = jnp.concatenate([ef, left(eb, 1)], axis=0)
    xin = jnp.concatenate(carried, axis=1).astype(BF16)
    yt = yt + jnp.dot(q2, xin, preferred_element_type=F32)

    for blk in range(4):
        yt_ref[blk] = yt[:, blk * LANES:(blk + 1) * LANES]
    for seg, y_ref in enumerate((yp_ref, ys_ref)):
        for ho in range(SSM_GROUP):
            ev = yt_ref[seg, pl.ds(ho, SSM_CHUNK, stride=SSM_GROUP), :]
            od = yt_ref[2 + seg, pl.ds(ho, SSM_CHUNK, stride=SSM_GROUP), :]
            y_ref[ho] = jnp.concatenate([ev, od], axis=0).T


def _s5(u_p, u_s, pc, pr2, bcr, bci, ccat, cr4, ci4, dtile, e16, x16, pairs):
    g = SSM_GROUPS
    rows = u_p.shape[1]
    assert u_s.shape[1] == rows == LANES
    grp = lambda shape: pl.BlockSpec((1,) + shape, lambda i: (i, 0, 0))
    chan = pl.BlockSpec((SSM_GROUP, rows, LANES), lambda i: (i, 0, 0))
    return pl.pallas_call(
        functools.partial(_s5_kernel, pairs=pairs),
        grid=(g,),
        in_specs=[chan, chan, grp((2 * SSM_STATE, 8)), grp((8, LANES)),
                  grp((2 * SSM_STATE, LANES)), grp((2 * SSM_STATE, LANES)),
                  grp((SSM_GROUP, SSM_SDIM)), grp((SSM_GROUP, SSM_SDIM)), grp((SSM_GROUP, SSM_SDIM)),
                  grp((1, SSM_TAPS)), _const_spec((LANES, SSM_TAPS)), _const_spec((SSM_ROW, SSM_CHUNK))],
        out_specs=[chan, chan],
        out_shape=[jax.ShapeDtypeStruct(u_p.shape, F32), jax.ShapeDtypeStruct(u_s.shape, F32)],
        scratch_shapes=[pltpu.VMEM((2, SSM_GROUP * SSM_PITCH, LANES), F32),
                        pltpu.VMEM((SSM_ROW, 4 * LANES), BF16),
                        pltpu.VMEM((SSM_ROW, SSM_ROW), BF16),
                        pltpu.VMEM((4, SSM_ROW, LANES), F32)],
        compiler_params=_cparams(("parallel",)),
        name="s5",
    )(u_p, u_s, pc, pr2, bcr, bci, ccat, cr4, ci4, dtile, e16, x16)


FF_CHUNK = 1024
MIXER_ROWS = 128


def _ffn_ln(x, w1_ref, w2_ref, g, b):
    xb = x.astype(BF16)
    acc = ALPHA * x
    for c in range(D_FF // FF_CHUNK):
        h = jnp.dot(xb, w1_ref[0, :, c * FF_CHUNK:(c + 1) * FF_CHUNK], preferred_element_type=F32)
        h = jnp.maximum(h, 0.0)
        acc = acc + jnp.dot((h * h).astype(BF16), w2_ref[0, c * FF_CHUNK:(c + 1) * FF_CHUNK, :],
                            preferred_element_type=F32)
    return _layer_norm(acc, g, b)


def _layer0_tail_kernel(x_ref, a_ref, y_ref, wg_ref, bg_ref, woa_ref, wos_ref, ln_ref, w1_ref, w2_ref, o_ref):
    rows_per_tile = x_ref.shape[0] // LANES
    row0 = (pl.program_id(0) % 2) * rows_per_tile
    ln = ln_ref[...]
    subs = [slice(r * LANES, (r + 1) * LANES) for r in range(rows_per_tile)]
    ys = [jax.nn.gelu(y_ref[:, row0 + r, :].T) for r in range(rows_per_tile)]
    gates = [jnp.dot(y.astype(BF16), wg_ref[...], preferred_element_type=F32) + bg_ref[...] for y in ys]
    zs = [(y * jax.nn.sigmoid(g)).astype(BF16) for y, g in zip(ys, gates)]
    mixes = [jnp.dot(a_ref[sl, :], woa_ref[...], preferred_element_type=F32)
             + jnp.dot(z, wos_ref[...], preferred_element_type=F32) for sl, z in zip(subs, zs)]
    x1 = jnp.concatenate([_layer_norm(ALPHA * x_ref[sl, :] + m, ln[0:1], ln[1:2]) for sl, m in zip(subs, mixes)],
                         axis=0)
    o_ref[...] = _ffn_ln(x1, w1_ref, w2_ref, ln[2:3], ln[3:4])


def _layer_weights_spec(shape, layer):
    return pl.BlockSpec((1,) + shape, lambda i: (layer, 0, 0), pipeline_mode=pl.Buffered(1))


def _layer0_tail(x2, attn, y, w_glu, b_glu, w_out_a, w_out_s, ln4, w1, w2):
    n = x2.shape[0]
    tm = TOKEN_TILE
    tok = lambda w: pl.BlockSpec((tm, w), lambda i: (i, 0))
    return pl.pallas_call(
        _layer0_tail_kernel,
        grid=(n // tm,),
        in_specs=[tok(D_MODEL), tok(ATTN_WIDTH),
                  pl.BlockSpec((SSM_WIDTH, 2 * tm // LANES, LANES), lambda i: (0, i // 2, 0)),
                  _const_spec((SSM_WIDTH, SSM_WIDTH)), _const_spec((1, SSM_WIDTH)),
                  _const_spec((ATTN_WIDTH, D_MODEL)), _const_spec((SSM_WIDTH, D_MODEL)),
                  _const_spec((4, D_MODEL)),
                  _layer_weights_spec((D_MODEL, D_FF), 0), _layer_weights_spec((D_FF, D_MODEL), 0)],
        out_specs=tok(D_MODEL),
        out_shape=jax.ShapeDtypeStruct((n, D_MODEL), F32),
        compiler_params=_cparams(("parallel",)),
        name="layer0_tail",
    )(x2, attn, y, w_glu, b_glu, w_out_a, w_out_s, ln4, w1, w2)


def _layer1_kernel(x_ref, win_ref, cln_ref, ws_ref, bs_ref, wout_ref, ln_ref, w1_ref, w2_ref, o_ref):
    cln = cln_ref[...]
    ln = ln_ref[...]
    halves = [x_ref[r0:r0 + MIXER_ROWS, :] for r0 in range(0, x_ref.shape[0], MIXER_ROWS)]
    hs = [jax.nn.gelu(jnp.dot(x.astype(BF16), win_ref[...], preferred_element_type=F32)) for x in halves]
    vbs = [_layer_norm(h[:, D_MODEL:], cln[0:1], cln[1:2]).astype(BF16) for h in hs]
    gated = []
    for h, vb in zip(hs, vbs):
        rows = []
        for c in range(MIXER_ROWS // SGU_CHUNK):
            cols = []
            for hd in range(SGU_HEADS):
                blk = vb[c * SGU_CHUNK:(c + 1) * SGU_CHUNK, hd * LANES:(hd + 1) * LANES]
                cols.append(jnp.dot(ws_ref[hd], blk, preferred_element_type=F32) + bs_ref[hd])
            rows.append(jnp.concatenate(cols, axis=1))
        gated.append((h[:, :D_MODEL] * jnp.concatenate(rows, axis=0)).astype(BF16))
    outs = [jnp.dot(g, wout_ref[...], preferred_element_type=F32) for g in gated]
    x1 = jnp.concatenate([_layer_norm(ALPHA * x + o, ln[0:1], ln[1:2]) for x, o in zip(halves, outs)], axis=0)
    o_ref[...] = _ffn_ln(x1, w1_ref, w2_ref, ln[2:3], ln[3:4])


def _layer1(x2, w_in, cln2, w_s, b_s_full, w_out, ln4, w1, w2):
    n = x2.shape[0]
    tm = TOKEN_TILE
    tok = pl.BlockSpec((tm, D_MODEL), lambda i: (i, 0))
    return pl.pallas_call(
        _layer1_kernel,
        grid=(n // tm,),
        in_specs=[tok, _const_spec((D_MODEL, 2 * D_MODEL)), _const_spec((2, D_MODEL)),
                  _const_spec((SGU_HEADS, SGU_CHUNK, SGU_CHUNK)), _const_spec((SGU_HEADS, SGU_CHUNK, LANES)),
                  _const_spec((D_MODEL, D_MODEL)), _const_spec((4, D_MODEL)),
                  _layer_weights_spec((D_MODEL, D_FF), 1), _layer_weights_spec((D_FF, D_MODEL), 1)],
        out_specs=tok,
        out_shape=jax.ShapeDtypeStruct((n, D_MODEL), F32),
        compiler_params=_cparams(("parallel",)),
        name="layer1",
    )(x2, w_in, cln2, w_s, b_s_full, w_out, ln4, w1, w2)


_HEAD_OF_SLOT = np.array([0, 4, 1, 5, 2, 6, 3, 7])


def _rope_tables(seq_len):
    t = np.arange(seq_len)
    row = (t // GRID_W).astype(np.float32)
    col = (t % GRID_W).astype(np.float32)
    axis_dim = HEAD_DIM // 2
    inv_freq = ROPE_THETA ** (-jnp.arange(0, axis_dim, 2, dtype=F32) / axis_dim)
    ang = jnp.concatenate([jnp.asarray(row)[:, None] * inv_freq, jnp.asarray(col)[:, None] * inv_freq], axis=-1)
    cos = jnp.repeat(jnp.cos(ang), 2, axis=-1)
    sin = jnp.repeat(jnp.sin(ang), 2, axis=-1)
    sign = jnp.tile(jnp.asarray([-1.0, 1.0], F32), HEAD_DIM // 2)
    reps = LANES // HEAD_DIM
    return jnp.tile(cos, (1, reps)), jnp.tile(sin * sign, (1, reps))


def _selectors():
    col = np.arange(SSM_TAPS)
    row = np.arange(LANES)
    e16 = (col[None, :] // SSM_GROUP == row[:, None])
    r2 = np.arange(SSM_ROW)
    x16 = (r2[:, None] // SSM_GROUP == np.arange(SSM_CHUNK)[None, :])
    as_bf16 = lambda a: jnp.asarray(a.astype(np.float32), BF16)
    return as_bf16(e16), as_bf16(x16)


def _s5_param_layout(a_re_f, a_im_f, ls_f, a_re_b, a_im_b, ls_b, b_re, b_im, c_re, c_im, d):
    g, ns = SSM_GROUPS, SSM_STATE
    zeros = jnp.zeros((g, ns), F32)
    per = lambda a_re, a_im, ls: jnp.stack([a_re, a_im, jnp.broadcast_to(ls[:, None], (g, ns))] + [zeros] * 5,
                                           axis=1)
    pr2 = jnp.concatenate([per(a_re_f, a_im_f, ls_f), per(a_re_b, a_im_b, ls_b)], axis=2)
    pc = jnp.transpose(pr2, (0, 2, 1))
    pad_lanes = lambda b: jnp.pad(jnp.concatenate([b, b], axis=1), ((0, 0), (0, 0), (0, LANES - SSM_GROUP)))
    ccat = jnp.concatenate([c_re, c_im, c_re, c_im], axis=2)
    cr4 = jnp.concatenate([c_re] * 4, axis=2)
    ci4 = jnp.concatenate([c_im] * 4, axis=2)
    dtile = jnp.tile(d.reshape(g, 1, SSM_GROUP), (1, 1, SSM_LAGS))
    return pc, pr2, pad_lanes(b_re), pad_lanes(b_im), ccat, cr4, ci4, dtile


def kernel(x_prompt, x_sample, ab_w_in, ab_q_norm, ab_k_norm, ssm_a_re_f, ssm_a_im_f, ssm_log_step_f,
           ssm_a_re_b, ssm_a_im_b, ssm_log_step_b, ssm_b_re, ssm_b_im, ssm_c_re, ssm_c_im, ssm_d, ssm_w_glu,
           ssm_b_glu, ab_w_out, c_w_in, c_ln_g, c_ln_b, c_w_s, c_b_s, c_w_out, ln_mix_g, ln_mix_b, ff_w1, ff_w2,
           ln_ff_g, ln_ff_b):
    xs = [x_prompt, x_sample]
    row2 = lambda v: v.reshape(1, -1)

    slot_cols = (_HEAD_OF_SLOT[:, None] * HEAD_DIM + np.arange(HEAD_DIM)[None, :]).reshape(-1)
    w_in = ab_w_in[0]
    w_in = jnp.concatenate([w_in[:, slot_cols], w_in[:, ATTN_WIDTH:]], axis=1).astype(BF16)
    scale = HEAD_DIM ** -0.5 * math.log2(math.e)
    gain = jnp.concatenate([jnp.tile(ab_q_norm[0] * scale, N_HEADS), jnp.tile(ab_k_norm[0], N_KV_HEADS)])
    score_bound = (1.05 * HEAD_DIM * scale * jnp.max(jnp.abs(ab_q_norm[0])) * jnp.max(jnp.abs(ab_k_norm[0])))
    score_bound = score_bound.reshape(1, 1).astype(F32)
    hid = np.arange(QK_WIDTH) // HEAD_DIM
    hmean = jnp.asarray((hid[:, None] == hid[None, :]).astype(np.float32) / HEAD_DIM, BF16)
    w_out = ab_w_out[0]
    w_out_a = w_out[:ATTN_WIDTH][slot_cols].astype(BF16)
    w_out_s = w_out[ATTN_WIDTH:].astype(BF16)
    w_glu = ssm_w_glu[0].astype(BF16)

    attn, us, pairs = [], [], []
    for x in xs:
        b, seq_len, _ = x.shape
        cos_t, sin_t = _rope_tables(seq_len)
        q, k, v_t, u = _inproj(x.reshape(b * seq_len, D_MODEL), w_in, row2(gain), cos_t, sin_t, hmean, seq_len)
        attn.append(_attention(score_bound, q.reshape(b, seq_len, ATTN_WIDTH), k.reshape(b, seq_len, KV_WIDTH),
                               v_t).reshape(b * seq_len, ATTN_WIDTH))
        us.append(u)
        pairs.append(seq_len // (2 * SSM_CHUNK))
    ys = _s5(us[0], us[1], *_s5_param_layout(
        ssm_a_re_f[0], ssm_a_im_f[0], ssm_log_step_f[0], ssm_a_re_b[0], ssm_a_im_b[0], ssm_log_step_b[0],
        ssm_b_re[0], ssm_b_im[0], ssm_c_re[0], ssm_c_im[0], ssm_d[0]), *_selectors(), tuple(pairs))

    ff_w1b = ff_w1.astype(BF16)
    ff_w2b = ff_w2.astype(BF16)
    ln4 = [jnp.stack([ln_mix_g[i], ln_mix_b[i], ln_ff_g[i], ln_ff_b[i]]) for i in range(DEPTH)]
    cln2 = jnp.stack([c_ln_g[0], c_ln_b[0]])
    bs_full = jnp.broadcast_to(c_b_s[0][:, :, None], (SGU_HEADS, SGU_CHUNK, LANES))
    c_w_in_b, c_w_s_b, c_w_out_b = c_w_in[0].astype(BF16), c_w_s[0].astype(BF16), c_w_out[0].astype(BF16)
    outs = []
    for x, a, y in zip(xs, attn, ys):
        b, seq_len, _ = x.shape
        n = b * seq_len
        h = _layer0_tail(x.reshape(n, D_MODEL), a, y, w_glu, row2(ssm_b_glu[0]), w_out_a, w_out_s,
                         ln4[0], ff_w1b, ff_w2b)
        h = _layer1(h, c_w_in_b, cln2, c_w_s_b, bs_full, c_w_out_b, ln4[1], ff_w1b, ff_w2b)
        outs.append(h.reshape(b, seq_len, D_MODEL))
    return tuple(outs)
```

```python
import functools
import math

import numpy as np
import jax
import jax.numpy as jnp
from jax import lax
from jax.experimental import pallas as pl
from jax.experimental.pallas import tpu as pltpu

F32 = jnp.float32
BF16 = jnp.bfloat16

D_MODEL = 1024
GRID_W = 64
N_HEADS = 8
N_KV_HEADS = 2
HEAD_DIM = 64
ATTN_WIDTH = N_HEADS * HEAD_DIM
KV_WIDTH = N_KV_HEADS * HEAD_DIM
QK_WIDTH = ATTN_WIDTH + KV_WIDTH
ROPE_THETA = 10000.0
SSM_WIDTH = D_MODEL - ATTN_WIDTH
SSM_GROUP = 16
SSM_GROUPS = SSM_WIDTH // SSM_GROUP
SSM_STATE = 64
SGU_HEADS = 8
SGU_CHUNK = 128
D_FF = 4 * D_MODEL
DEPTH = 2
AB_IN = ATTN_WIDTH + 2 * KV_WIDTH + SSM_WIDTH
ALPHA = (2 * DEPTH) ** 0.25
EPS = 1e-6

LANES = 128
VMEM_LIMIT_BYTES = 56 * 1024 * 1024

SSM_CHUNK = 64
SSM_ROW = SSM_CHUNK * SSM_GROUP
SSM_LAGS = 2 * SSM_CHUNK
SSM_TAPS = SSM_LAGS * SSM_GROUP
SSM_SDIM = 4 * SSM_STATE

TOKEN_TILE = 512


def _cparams(sem):
    return pltpu.CompilerParams(dimension_semantics=sem, vmem_limit_bytes=VMEM_LIMIT_BYTES)


def _const_spec(shape):
    nd = len(shape)
    return pl.BlockSpec(shape, lambda *_: (0,) * nd, pipeline_mode=pl.Buffered(1))


def _layer_norm(r, g, b):
    mu = jnp.mean(r, axis=-1, keepdims=True)
    d = r - mu
    var = jnp.mean(d * d, axis=-1, keepdims=True)
    return d * lax.rsqrt(var + EPS) * g + b


def _split_bf16(x):
    hi = x.astype(BF16)
    lo = (x - hi.astype(F32)).astype(BF16)
    return hi, lo


def _select_cols(x, e):
    hi, lo = _split_bf16(x)
    y = jnp.dot(jnp.concatenate([hi, lo], axis=0), e, preferred_element_type=F32)
    return y[:x.shape[0]] + y[x.shape[0]:]


def _select_rows(e, x):
    hi, lo = _split_bf16(x)
    y = jnp.dot(e, jnp.concatenate([hi, lo], axis=1), preferred_element_type=F32)
    return y[:, :x.shape[1]] + y[:, x.shape[1]:]


def _inproj_kernel(x_ref, w_ref, gain_ref, cos_ref, sin_ref, hmean_ref, q_ref, k_ref, v_ref, u_ref):
    subs = [slice(r0, r0 + INPROJ_SUB) for r0 in range(0, x_ref.shape[0], INPROJ_SUB)]
    hs = [jnp.dot(x_ref[sl, :].astype(BF16), w_ref[...], preferred_element_type=F32) for sl in subs]
    mss = [jnp.dot((h[:, :QK_WIDTH] * h[:, :QK_WIDTH]).astype(BF16), hmean_ref[...], preferred_element_type=F32)
           for h in hs]
    qkns = [h[:, :QK_WIDTH] * lax.rsqrt(ms + EPS) * gain_ref[...] for h, ms in zip(hs, mss)]
    even = (lax.broadcasted_iota(jnp.int32, (INPROJ_SUB, LANES), 1) % 2) == 0
    for sl, qkn in zip(subs, qkns):
        c = cos_ref[sl, :]
        s = sin_ref[sl, :]
        for j in range(QK_WIDTH // LANES):
            xs = qkn[:, j * LANES:(j + 1) * LANES]
            nxt = pltpu.roll(xs, LANES - 1, 1)
            prv = pltpu.roll(xs, 1, 1)
            roped = (xs * c + jnp.where(even, nxt, prv) * s).astype(BF16)
            if j < ATTN_WIDTH // LANES:
                q_ref[sl, j * LANES:(j + 1) * LANES] = roped
            else:
                k_ref[sl, :] = roped
    for sl, h in zip(subs, hs):
        v_ref[:, sl] = h[:, QK_WIDTH:QK_WIDTH + KV_WIDTH].T.astype(BF16)
        for r in range(INPROJ_SUB // LANES):
            blk = h[r * LANES:(r + 1) * LANES, QK_WIDTH + KV_WIDTH:]
            u_ref[:, sl.start // LANES + r, :] = blk.T


INPROJ_TILE = 1024
INPROJ_SUB = 256


def _inproj(x2, w_in, gain, cos_t, sin_t, hmean, seq_len):
    n = x2.shape[0]
    tm = INPROJ_TILE
    nper = seq_len // tm
    return pl.pallas_call(
        _inproj_kernel,
        grid=(n // tm,),
        in_specs=[
            pl.BlockSpec((tm, D_MODEL), lambda i: (i, 0)),
            _const_spec((D_MODEL, AB_IN)),
            _const_spec((1, QK_WIDTH)),
            pl.BlockSpec((tm, LANES), lambda i: (i % nper, 0)),
            pl.BlockSpec((tm, LANES), lambda i: (i % nper, 0)),
            _const_spec((QK_WIDTH, QK_WIDTH)),
        ],
        out_specs=[
            pl.BlockSpec((tm, ATTN_WIDTH), lambda i: (i, 0)),
            pl.BlockSpec((tm, KV_WIDTH), lambda i: (i, 0)),
            pl.BlockSpec((KV_WIDTH, tm), lambda i: (0, i)),
            pl.BlockSpec((SSM_WIDTH, tm // LANES, LANES), lambda i: (0, i, 0)),
        ],
        out_shape=[
            jax.ShapeDtypeStruct((n, ATTN_WIDTH), BF16),
            jax.ShapeDtypeStruct((n, KV_WIDTH), BF16),
            jax.ShapeDtypeStruct((KV_WIDTH, n), BF16),
            jax.ShapeDtypeStruct((SSM_WIDTH, n // LANES, LANES), F32),
        ],
        compiler_params=_cparams(("parallel",)),
        name="inproj",
    )(x2, w_in, gain, cos_t, sin_t, hmean)


ATT_UNIT_Q = 64


ATT_SHIFT_LIMIT = 60.0


def _attn_kernel(bound_ref, q_ref, k_ref, vt_ref, o_ref,
                 s0_ref, s1_ref, p0_ref, p1_ref, m0_ref, m1_ref, l0_ref, l1_ref):
    seq_len = k_ref.shape[1]
    n_units = seq_len // ATT_UNIT_Q
    rep = N_HEADS // N_KV_HEADS
    lane = lax.broadcasted_iota(jnp.int32, (ATT_UNIT_Q, LANES), 1)
    lo_half = lane < HEAD_DIM
    zero = jnp.zeros((ATT_UNIT_Q, LANES), BF16)
    bound = bound_ref[0, 0]

    def rows_of(t):
        return pl.ds(pl.multiple_of(t * ATT_UNIT_Q, ATT_UNIT_Q), ATT_UNIT_Q)

    def raw_scores(t):
        q = q_ref[0, rows_of(t), :]
        slots = [q[:, m * LANES:(m + 1) * LANES] for m in range(rep)]
        qs = jnp.concatenate([jnp.where(lo_half, x, zero) for x in slots]
                             + [jnp.where(lo_half, zero, x) for x in slots], axis=0)
        return lax.dot_general(k_ref[0], qs, (((1,), (1,)), ((), ())), preferred_element_type=F32)

    def scores(t, s_ref, m_ref):
        s = raw_scores(t)
        s_ref[...] = s
        m_ref[...] = jnp.max(s, axis=0, keepdims=True)

    def probs(s_ref, m_ref, p_ref, l_ref):
        p = jnp.exp2(s_ref[...] - m_ref[...])
        l_ref[...] = jnp.sum(p, axis=0, keepdims=True)
        p_ref[...] = p.astype(BF16)

    def bounded_probs(t, p_ref, l_ref):
        p = jnp.exp2(raw_scores(t) - bound)
        l_ref[...] = jnp.sum(p, axis=0, keepdims=True)
        p_ref[...] = p.astype(BF16)

    def output(t, p_ref, l_ref):
        o_t = jnp.dot(vt_ref[...], p_ref[...], preferred_element_type=F32) / l_ref[...]
        o = o_t.T
        for m in range(rep):
            g0 = o[m * ATT_UNIT_Q:(m + 1) * ATT_UNIT_Q]
            g1 = o[(rep + m) * ATT_UNIT_Q:(rep + m + 1) * ATT_UNIT_Q]
            o_ref[0, rows_of(t), m * LANES:(m + 1) * LANES] = jnp.where(lo_half, g0, g1).astype(BF16)

    @pl.when(bound <= ATT_SHIFT_LIMIT)
    def _bounded_shift():
        bounded_probs(0, p0_ref, l0_ref)

        def body(i, carry):
            t = 2 * i + 1
            bounded_probs(t, p1_ref, l1_ref)
            output(t - 1, p0_ref, l0_ref)
            bounded_probs(t + 1, p0_ref, l0_ref)
            output(t, p1_ref, l1_ref)
            return carry

        lax.fori_loop(0, (n_units - 2) // 2, body, 0)
        bounded_probs(n_units - 1, p1_ref, l1_ref)
        output(n_units - 2, p0_ref, l0_ref)
        output(n_units - 1, p1_ref, l1_ref)

    @pl.when(bound > ATT_SHIFT_LIMIT)
    def _row_max_shift():
        scores(0, s0_ref, m0_ref)
        scores(1, s1_ref, m1_ref)
        probs(s0_ref, m0_ref, p0_ref, l0_ref)

        def body(i, carry):
            t = 2 * i + 2
            scores(t, s0_ref, m0_ref)
            probs(s1_ref, m1_ref, p1_ref, l1_ref)
            output(t - 2, p0_ref, l0_ref)
            scores(t + 1, s1_ref, m1_ref)
            probs(s0_ref, m0_ref, p0_ref, l0_ref)
            output(t - 1, p1_ref, l1_ref)
            return carry

        lax.fori_loop(0, (n_units - 2) // 2, body, 0)
        probs(s1_ref, m1_ref, p1_ref, l1_ref)
        output(n_units - 2, p0_ref, l0_ref)
        output(n_units - 1, p1_ref, l1_ref)


def _attention(bound, q, k, v_t):
    b, seq_len, _ = q.shape
    cols = ATT_UNIT_Q * N_HEADS
    seq = lambda w, **kw: pl.BlockSpec((1, seq_len, w), lambda i: (i, 0, 0), **kw)
    once = dict(pipeline_mode=pl.Buffered(1))
    return pl.pallas_call(
        _attn_kernel,
        grid=(b,),
        in_specs=[pl.BlockSpec(memory_space=pltpu.SMEM),
                  seq(ATTN_WIDTH, **once), seq(KV_WIDTH, **once),
                  pl.BlockSpec((KV_WIDTH, seq_len), lambda i: (0, i), **once)],
        out_specs=seq(ATTN_WIDTH),
        out_shape=jax.ShapeDtypeStruct((b, seq_len, ATTN_WIDTH), BF16),
        scratch_shapes=[pltpu.VMEM((seq_len, cols), F32), pltpu.VMEM((seq_len, cols), F32),
                        pltpu.VMEM((seq_len, cols), BF16), pltpu.VMEM((seq_len, cols), BF16),
                        pltpu.VMEM((1, cols), F32), pltpu.VMEM((1, cols), F32),
                        pltpu.VMEM((1, cols), F32), pltpu.VMEM((1, cols), F32)],
        compiler_params=_cparams(("parallel",)),
        name="attention",
    )(bound, q, k, v_t)


SSM_PITCH = 132


def _s5_kernel(xp_ref, xs_ref, pc_ref, pr2_ref, bcr_ref, bci_ref, ccat_ref, cr4_ref, ci4_ref, dt_ref,
               e16_ref, x16_ref, yp_ref, ys_ref, s_ref, zt_ref, m_ref, yt_ref, *, pairs):
    ns = SSM_STATE
    pc = pc_ref[0]
    are_c, aim_c, dt_c = pc[:, 0:1], pc[:, 1:2], jnp.exp(pc[:, 2:3])
    row = lax.broadcasted_iota(jnp.int32, (2 * ns, LANES), 0)
    lan = lax.broadcasted_iota(jnp.int32, (2 * ns, LANES), 1)
    b_rows = row >= ns
    n_g = jnp.where(b_rows, lan - (SSM_CHUNK - 1), (SSM_CHUNK - 1) - lan)
    mask = n_g >= 0
    nf = jnp.where(mask, n_g, 0).astype(F32)
    rho = jnp.exp(are_c * dt_c * nf)
    ang = aim_c * dt_c * nf
    cs = jnp.where(mask, rho * jnp.cos(ang), 0.0)
    sn = jnp.where(mask, rho * jnp.sin(ang), 0.0)
    b_col = b_rows[:, 0:1]
    a1r = jnp.where(b_col, cs[:, SSM_CHUNK:SSM_CHUNK + 1], cs[:, SSM_CHUNK - 2:SSM_CHUNK - 1]) - 1.0
    a1i = jnp.where(b_col, sn[:, SSM_CHUNK:SSM_CHUNK + 1], sn[:, SSM_CHUNK - 2:SSM_CHUNK - 1])
    den = are_c * are_c + aim_c * aim_c
    cr = (a1r * are_c + a1i * aim_c) / den
    ci = (a1i * are_c - a1r * aim_c) / den

    def tile_lanes(x):
        for sh in (SSM_GROUP, 2 * SSM_GROUP, 4 * SSM_GROUP):
            x = x + pltpu.roll(x, sh, 1)
        return jnp.concatenate([x] * (SSM_TAPS // LANES), axis=1)

    bre, bim = bcr_ref[0], bci_ref[0]
    bbr = tile_lanes(cr * bre - ci * bim)
    bbi = tile_lanes(cr * bim + ci * bre)
    base = jnp.concatenate([cs[:ns], sn[:ns], cs[ns:], sn[ns:]], axis=0)
    ex = _select_cols(base, e16_ref[...])
    ecf, esf, ecb, esb = ex[:ns], ex[ns:2 * ns], ex[2 * ns:3 * ns], ex[3 * ns:]
    z_fre = ecf * bbr[:ns] - esf * bbi[:ns]
    z_fim = esf * bbr[:ns] + ecf * bbi[:ns]
    z_bre = ecb * bbr[ns:] - esb * bbi[ns:]
    z_bim = esb * bbr[ns:] + ecb * bbi[ns:]

    g_tab = jnp.concatenate([z_fre, -z_fim, z_bre, -z_bim], axis=0)
    ch, cl = _split_bf16(ccat_ref[0])
    gh, gl = _split_bf16(g_tab)
    w = (jnp.dot(ch, gh, preferred_element_type=F32) + jnp.dot(ch, gl, preferred_element_type=F32)
         + jnp.dot(cl, gh, preferred_element_type=F32))
    wrow = lax.broadcasted_iota(jnp.int32, (SSM_GROUP, SSM_TAPS), 0)
    wcol = lax.broadcasted_iota(jnp.int32, (SSM_GROUP, SSM_TAPS), 1)
    on_diag = (wcol // SSM_GROUP == SSM_CHUNK - 1) & (wcol % SSM_GROUP == wrow)
    w = w + jnp.where(on_diag, dt_ref[0], 0.0)
    per_vreg = LANES // SSM_GROUP
    rolled = [w] + [pltpu.roll(w, SSM_TAPS - SSM_GROUP * kk, 1) for kk in range(1, per_vreg)]
    for j in range(SSM_CHUNK):
        r = SSM_CHUNK - 1 - j
        blk = rolled[r % per_vreg][:, (r // per_vreg) * LANES:(r // per_vreg) * LANES + SSM_ROW]
        m_ref[j * SSM_GROUP:(j + 1) * SSM_GROUP, :] = blk.astype(BF16)

    p2 = jnp.concatenate([z_fre[:, :SSM_ROW], z_fim[:, :SSM_ROW], z_bre[:, SSM_ROW:], z_bim[:, SSM_ROW:]],
                         axis=0).astype(BF16)

    pr2 = pr2_ref[0]
    j_r = lax.broadcasted_iota(jnp.int32, (SSM_CHUNK, LANES), 0)
    l_r = lax.broadcasted_iota(jnp.int32, (SSM_CHUNK, LANES), 1)
    lo_r = l_r < ns
    n_q = jnp.where(lo_r, j_r + 1, (SSM_CHUNK - 1) - j_r).astype(F32)
    dt2 = jnp.exp(pr2[2:3, :])
    rho = jnp.exp(pr2[0:1, :] * dt2 * n_q)
    ang = pr2[1:2, :] * dt2 * n_q
    c2 = rho * jnp.cos(ang)
    s2 = rho * jnp.sin(ang)
    base_q = jnp.concatenate([jnp.where(lo_r, c2, -pltpu.roll(s2, ns, 1)),
                              jnp.where(lo_r, pltpu.roll(c2, ns, 1), -s2)], axis=1)
    a1x = _select_rows(x16_ref[...], base_q)
    sign = jnp.where(lax.broadcasted_iota(jnp.int32, (SSM_ROW, LANES), 1) < ns, 1.0, -1.0)
    a2x = jnp.concatenate([pltpu.roll(a1x[:, :LANES], ns, 1) * sign,
                           pltpu.roll(a1x[:, LANES:], ns, 1) * sign], axis=1)
    q2 = (a1x * jnp.concatenate([cr4_ref[0]] * SSM_CHUNK, axis=0)
          + a2x * jnp.concatenate([ci4_ref[0]] * SSM_CHUNK, axis=0)).astype(BF16)

    n_s = (SSM_CHUNK * jnp.left_shift(1, jnp.minimum(lan, 8))).astype(F32)
    rho = jnp.exp(are_c * dt_c * n_s)
    ang = aim_c * dt_c * n_s
    sc_r = rho * jnp.cos(ang)
    sc_i = rho * jnp.sin(ang)

    for seg, x_ref in enumerate((xp_ref, xs_ref)):
        for hi in range(SSM_GROUP):
            s_ref[seg, hi * SSM_PITCH:hi * SSM_PITCH + LANES, :] = x_ref[hi].T
    for par in range(2):
        for i in range(SSM_CHUNK):
            for seg in range(2):
                rows16 = s_ref[seg, pl.ds(par * SSM_CHUNK + i, SSM_GROUP, stride=SSM_PITCH), :]
                blk = 2 * par + seg
                zt_ref[i * SSM_GROUP:(i + 1) * SSM_GROUP, blk * LANES:(blk + 1) * LANES] = rows16.astype(BF16)

    zt = zt_ref[...]
    yt = jnp.dot(m_ref[...], zt, preferred_element_type=F32)
    st = jnp.dot(p2, zt, preferred_element_type=F32)

    def cmul(kk, lo, v):
        ar, ai = sc_r[lo:lo + ns, kk:kk + 1], sc_i[lo:lo + ns, kk:kk + 1]
        return jnp.concatenate([ar * v[:ns] - ai * v[ns:], ar * v[ns:] + ai * v[:ns]], axis=0)

    carried = [None] * 4
    for seg, npairs in enumerate(pairs):
        c2i = lax.broadcasted_iota(jnp.int32, (2 * ns, LANES), 1) % npairs
        right = lambda v, d: jnp.where(c2i >= d, pltpu.roll(v, d, 1), 0.0)
        left = lambda v, d: jnp.where(c2i < npairs - d, pltpu.roll(v, LANES - d, 1), 0.0)
        ev, od = st[:, seg * LANES:(seg + 1) * LANES], st[:, (2 + seg) * LANES:(3 + seg) * LANES]
        ef, of, eb, ob = ev[:2 * ns], od[:2 * ns], ev[2 * ns:], od[2 * ns:]
        ef, of = ef + cmul(0, 0, right(of, 1)), of + cmul(0, 0, ef)
        eb, ob = eb + cmul(0, ns, ob), ob + cmul(0, ns, left(eb, 1))
        for kk in range(1, int(math.log2(npairs)) + 1):
            d = 1 << (kk - 1)
            ef, of = ef + cmul(kk, 0, right(ef, d)), of + cmul(kk, 0, right(of, d))
            eb, ob = eb + cmul(kk, ns, left(eb, d)), ob + cmul(kk, ns, left(ob, d))
        carried[seg] = jnp.concatenate([right(of, 1), ob], axis=0)
        carried[2 + seg] = jnp.concatenate([ef, left(eb, 1)], axis=0)
    xin = jnp.concatenate(carried, axis=1).astype(BF16)
    yt = yt + jnp.dot(q2, xin, preferred_element_type=F32)

    for blk in range(4):
        yt_ref[blk] = yt[:, blk * LANES:(blk + 1) * LANES]
    for seg, y_ref in enumerate((yp_ref, ys_ref)):
        for ho in range(SSM_GROUP):
            ev = yt_ref[seg, pl.ds(ho, SSM_CHUNK, stride=SSM_GROUP), :]
            od = yt_ref[2 + seg, pl.ds(ho, SSM_CHUNK, stride=SSM_GROUP), :]
            y_ref[ho] = jnp.concatenate([ev, od], axis=0).T


def _s5(u_p, u_s, pc, pr2, bcr, bci, ccat, cr4, ci4, dtile, e16, x16, pairs):
    g = SSM_GROUPS
    rows = u_p.shape[1]
    assert u_s.shape[1] == rows == LANES
    grp = lambda shape: pl.BlockSpec((1,) + shape, lambda i: (i, 0, 0))
    chan = pl.BlockSpec((SSM_GROUP, rows, LANES), lambda i: (i, 0, 0))
    return pl.pallas_call(
        functools.partial(_s5_kernel, pairs=pairs),
        grid=(g,),
        in_specs=[chan, chan, grp((2 * SSM_STATE, 8)), grp((8, LANES)),
                  grp((2 * SSM_STATE, LANES)), grp((2 * SSM_STATE, LANES)),
                  grp((SSM_GROUP, SSM_SDIM)), grp((SSM_GROUP, SSM_SDIM)), grp((SSM_GROUP, SSM_SDIM)),
                  grp((1, SSM_TAPS)), _const_spec((LANES, SSM_TAPS)), _const_spec((SSM_ROW, SSM_CHUNK))],
        out_specs=[chan, chan],
        out_shape=[jax.ShapeDtypeStruct(u_p.shape, F32), jax.ShapeDtypeStruct(u_s.shape, F32)],
        scratch_shapes=[pltpu.VMEM((2, SSM_GROUP * SSM_PITCH, LANES), F32),
                        pltpu.VMEM((SSM_ROW, 4 * LANES), BF16),
                        pltpu.VMEM((SSM_ROW, SSM_ROW), BF16),
                        pltpu.VMEM((4, SSM_ROW, LANES), F32)],
        compiler_params=_cparams(("parallel",)),
        name="s5",
    )(u_p, u_s, pc, pr2, bcr, bci, ccat, cr4, ci4, dtile, e16, x16)


FF_CHUNK = 1024
MIXER_ROWS = 128


def _ffn_ln(x, w1_ref, w2_ref, g, b):
    half = x.shape[0] // 2
    xs = [x[:half], x[half:]]
    xbs = [v.astype(BF16) for v in xs]
    accs = [ALPHA * v for v in xs]
    for c in range(D_FF // FF_CHUNK):
        for i in range(2):
            h = jnp.dot(xbs[i], w1_ref[0, :, c * FF_CHUNK:(c + 1) * FF_CHUNK], preferred_element_type=F32)
            h = jnp.maximum(h, 0.0)
            accs[i] = accs[i] + jnp.dot((h * h).astype(BF16), w2_ref[0, c * FF_CHUNK:(c + 1) * FF_CHUNK, :],
                                        preferred_element_type=F32)
    return jnp.concatenate([_layer_norm(a, g, b) for a in accs], axis=0)


def _layer0_tail_kernel(x_ref, a_ref, y_ref, wg_ref, bg_ref, woa_ref, wos_ref, ln_ref, w1_ref, w2_ref, o_ref):
    rows_per_tile = x_ref.shape[0] // LANES
    row0 = (pl.program_id(0) % 2) * rows_per_tile
    ln = ln_ref[...]
    subs = [slice(r * LANES, (r + 1) * LANES) for r in range(rows_per_tile)]
    ys = [jax.nn.gelu(y_ref[:, row0 + r, :].T) for r in range(rows_per_tile)]
    gates = [jnp.dot(y.astype(BF16), wg_ref[...], preferred_element_type=F32) + bg_ref[...] for y in ys]
    zs = [(y * jax.nn.sigmoid(g)).astype(BF16) for y, g in zip(ys, gates)]
    mixes = [jnp.dot(a_ref[sl, :], woa_ref[...], preferred_element_type=F32)
             + jnp.dot(z, wos_ref[...], preferred_element_type=F32) for sl, z in zip(subs, zs)]
    x1 = jnp.concatenate([_layer_norm(ALPHA * x_ref[sl, :] + m, ln[0:1], ln[1:2]) for sl, m in zip(subs, mixes)],
                         axis=0)
    o_ref[...] = _ffn_ln(x1, w1_ref, w2_ref, ln[2:3], ln[3:4])


def _layer_weights_spec(shape, layer):
    return pl.BlockSpec((1,) + shape, lambda i: (layer, 0, 0), pipeline_mode=pl.Buffered(1))


def _layer0_tail(x2, attn, y, w_glu, b_glu, w_out_a, w_out_s, ln4, w1, w2):
    n = x2.shape[0]
    tm = TOKEN_TILE
    tok = lambda w: pl.BlockSpec((tm, w), lambda i: (i, 0))
    return pl.pallas_call(
        _layer0_tail_kernel,
        grid=(n // tm,),
        in_specs=[tok(D_MODEL), tok(ATTN_WIDTH),
                  pl.BlockSpec((SSM_WIDTH, 2 * tm // LANES, LANES), lambda i: (0, i // 2, 0)),
                  _const_spec((SSM_WIDTH, SSM_WIDTH)), _const_spec((1, SSM_WIDTH)),
                  _const_spec((ATTN_WIDTH, D_MODEL)), _const_spec((SSM_WIDTH, D_MODEL)),
                  _const_spec((4, D_MODEL)),
                  _layer_weights_spec((D_MODEL, D_FF), 0), _layer_weights_spec((D_FF, D_MODEL), 0)],
        out_specs=tok(D_MODEL),
        out_shape=jax.ShapeDtypeStruct((n, D_MODEL), F32),
        compiler_params=_cparams(("parallel",)),
        name="layer0_tail",
    )(x2, attn, y, w_glu, b_glu, w_out_a, w_out_s, ln4, w1, w2)


def _layer1_kernel(x_ref, win_ref, cln_ref, ws_ref, bs_ref, wout_ref, ln_ref, w1_ref, w2_ref, o_ref):
    cln = cln_ref[...]
    ln = ln_ref[...]
    halves = [x_ref[r0:r0 + MIXER_ROWS, :] for r0 in range(0, x_ref.shape[0], MIXER_ROWS)]
    hs = [jax.nn.gelu(jnp.dot(x.astype(BF16), win_ref[...], preferred_element_type=F32)) for x in halves]
    vbs = [_layer_norm(h[:, D_MODEL:], cln[0:1], cln[1:2]).astype(BF16) for h in hs]
    gated = []
    for h, vb in zip(hs, vbs):
        rows = []
        for c in range(MIXER_ROWS // SGU_CHUNK):
            cols = []
            for hd in range(SGU_HEADS):
                blk = vb[c * SGU_CHUNK:(c + 1) * SGU_CHUNK, hd * LANES:(hd + 1) * LANES]
                cols.append(jnp.dot(ws_ref[hd], blk, preferred_element_type=F32) + bs_ref[hd])
            rows.append(jnp.concatenate(cols, axis=1))
        gated.append((h[:, :D_MODEL] * jnp.concatenate(rows, axis=0)).astype(BF16))
    outs = [jnp.dot(g, wout_ref[...], preferred_element_type=F32) for g in gated]
    x1 = jnp.concatenate([_layer_norm(ALPHA * x + o, ln[0:1], ln[1:2]) for x, o in zip(halves, outs)], axis=0)
    o_ref[...] = _ffn_ln(x1, w1_ref, w2_ref, ln[2:3], ln[3:4])


def _layer1(x2, w_in, cln2, w_s, b_s_full, w_out, ln4, w1, w2):
    n = x2.shape[0]
    tm = TOKEN_TILE
    tok = pl.BlockSpec((tm, D_MODEL), lambda i: (i, 0))
    return pl.pallas_call(
        _layer1_kernel,
        grid=(n // tm,),
        in_specs=[tok, _const_spec((D_MODEL, 2 * D_MODEL)), _const_spec((2, D_MODEL)),
                  _const_spec((SGU_HEADS, SGU_CHUNK, SGU_CHUNK)), _const_spec((SGU_HEADS, SGU_CHUNK, LANES)),
                  _const_spec((D_MODEL, D_MODEL)), _const_spec((4, D_MODEL)),
                  _layer_weights_spec((D_MODEL, D_FF), 1), _layer_weights_spec((D_FF, D_MODEL), 1)],
        out_specs=tok,
        out_shape=jax.ShapeDtypeStruct((n, D_MODEL), F32),
        compiler_params=_cparams(("parallel",)),
        name="layer1",
    )(x2, w_in, cln2, w_s, b_s_full, w_out, ln4, w1, w2)


_HEAD_OF_SLOT = np.array([0, 4, 1, 5, 2, 6, 3, 7])


def _rope_tables(seq_len):
    t = np.arange(seq_len)
    axis_dim = HEAD_DIM // 2
    inv_freq = ROPE_THETA ** (-np.arange(0, axis_dim, 2, dtype=np.float64) / axis_dim)
    ang = np.concatenate([(t // GRID_W)[:, None] * inv_freq, (t % GRID_W)[:, None] * inv_freq], axis=-1)
    cos = np.repeat(np.cos(ang), 2, axis=-1)
    sin = np.repeat(np.sin(ang), 2, axis=-1) * np.tile([-1.0, 1.0], HEAD_DIM // 2)
    reps = LANES // HEAD_DIM
    return (jnp.asarray(np.tile(cos, (1, reps)), F32), jnp.asarray(np.tile(sin, (1, reps)), F32))


def _selectors():
    col = np.arange(SSM_TAPS)
    row = np.arange(LANES)
    e16 = (col[None, :] // SSM_GROUP == row[:, None])
    r2 = np.arange(SSM_ROW)
    x16 = (r2[:, None] // SSM_GROUP == np.arange(SSM_CHUNK)[None, :])
    as_bf16 = lambda a: jnp.asarray(a.astype(np.float32), BF16)
    return as_bf16(e16), as_bf16(x16)


def _s5_param_layout(a_re_f, a_im_f, ls_f, a_re_b, a_im_b, ls_b, b_re, b_im, c_re, c_im, d):
    g, ns = SSM_GROUPS, SSM_STATE
    zeros = jnp.zeros((g, ns), F32)
    per = lambda a_re, a_im, ls: jnp.stack([a_re, a_im, jnp.broadcast_to(ls[:, None], (g, ns))] + [zeros] * 5,
                                           axis=1)
    pr2 = jnp.concatenate([per(a_re_f, a_im_f, ls_f), per(a_re_b, a_im_b, ls_b)], axis=2)
    pc = jnp.transpose(pr2, (0, 2, 1))
    pad_lanes = lambda b: jnp.pad(jnp.concatenate([b, b], axis=1), ((0, 0), (0, 0), (0, LANES - SSM_GROUP)))
    ccat = jnp.concatenate([c_re, c_im, c_re, c_im], axis=2)
    cr4 = jnp.concatenate([c_re] * 4, axis=2)
    ci4 = jnp.concatenate([c_im] * 4, axis=2)
    dtile = jnp.tile(d.reshape(g, 1, SSM_GROUP), (1, 1, SSM_LAGS))
    return pc, pr2, pad_lanes(b_re), pad_lanes(b_im), ccat, cr4, ci4, dtile


def kernel(x_prompt, x_sample, ab_w_in, ab_q_norm, ab_k_norm, ssm_a_re_f, ssm_a_im_f, ssm_log_step_f,
           ssm_a_re_b, ssm_a_im_b, ssm_log_step_b, ssm_b_re, ssm_b_im, ssm_c_re, ssm_c_im, ssm_d, ssm_w_glu,
           ssm_b_glu, ab_w_out, c_w_in, c_ln_g, c_ln_b, c_w_s, c_b_s, c_w_out, ln_mix_g, ln_mix_b, ff_w1, ff_w2,
           ln_ff_g, ln_ff_b):
    xs = [x_prompt, x_sample]
    row2 = lambda v: v.reshape(1, -1)

    slot_cols = (_HEAD_OF_SLOT[:, None] * HEAD_DIM + np.arange(HEAD_DIM)[None, :]).reshape(-1)
    w_in = ab_w_in[0]
    w_in = jnp.concatenate([w_in[:, slot_cols], w_in[:, ATTN_WIDTH:]], axis=1).astype(BF16)
    scale = HEAD_DIM ** -0.5 * math.log2(math.e)
    gain = jnp.concatenate([jnp.tile(ab_q_norm[0] * scale, N_HEADS), jnp.tile(ab_k_norm[0], N_KV_HEADS)])
    score_bound = (1.05 * HEAD_DIM * scale * jnp.max(jnp.abs(ab_q_norm[0])) * jnp.max(jnp.abs(ab_k_norm[0])))
    score_bound = score_bound.reshape(1, 1).astype(F32)
    hid = np.arange(QK_WIDTH) // HEAD_DIM
    hmean = jnp.asarray((hid[:, None] == hid[None, :]).astype(np.float32) / HEAD_DIM, BF16)
    w_out = ab_w_out[0]
    w_out_a = w_out[:ATTN_WIDTH][slot_cols].astype(BF16)
    w_out_s = w_out[ATTN_WIDTH:].astype(BF16)
    w_glu = ssm_w_glu[0].astype(BF16)

    attn, us, pairs = [], [], []
    cos_t, sin_t = _rope_tables(max(x.shape[1] for x in xs))
    for x in xs:
        b, seq_len, _ = x.shape
        q, k, v_t, u = _inproj(x.reshape(b * seq_len, D_MODEL), w_in, row2(gain), cos_t, sin_t, hmean, seq_len)
        attn.append(_attention(score_bound, q.reshape(b, seq_len, ATTN_WIDTH), k.reshape(b, seq_len, KV_WIDTH),
                               v_t).reshape(b * seq_len, ATTN_WIDTH))
        us.append(u)
        pairs.append(seq_len // (2 * SSM_CHUNK))
    ys = _s5(us[0], us[1], *_s5_param_layout(
        ssm_a_re_f[0], ssm_a_im_f[0], ssm_log_step_f[0], ssm_a_re_b[0], ssm_a_im_b[0], ssm_log_step_b[0],
        ssm_b_re[0], ssm_b_im[0], ssm_c_re[0], ssm_c_im[0], ssm_d[0]), *_selectors(), tuple(pairs))

    ff_w1b = ff_w1.astype(BF16)
    ff_w2b = ff_w2.astype(BF16)
    ln4 = [jnp.stack([ln_mix_g[i], ln_mix_b[i], ln_ff_g[i], ln_ff_b[i]]) for i in range(DEPTH)]
    cln2 = jnp.stack([c_ln_g[0], c_ln_b[0]])
    bs_full = jnp.broadcast_to(c_b_s[0][:, :, None], (SGU_HEADS, SGU_CHUNK, LANES))
    c_w_in_b, c_w_s_b, c_w_out_b = c_w_in[0].astype(BF16), c_w_s[0].astype(BF16), c_w_out[0].astype(BF16)
    outs = []
    for x, a, y in zip(xs, attn, ys):
        b, seq_len, _ = x.shape
        n = b * seq_len
        h = _layer0_tail(x.reshape(n, D_MODEL), a, y, w_glu, row2(ssm_b_glu[0]), w_out_a, w_out_s,
                         ln4[0], ff_w1b, ff_w2b)
        h = _layer1(h, c_w_in_b, cln2, c_w_s_b, bs_full, c_w_out_b, ln4[1], ff_w1b, ff_w2b)
        outs.append(h.reshape(b, seq_len, D_MODEL))
    return tuple(outs)
```

```python
import functools
import math

import numpy as np
import jax
import jax.numpy as jnp
from jax import lax
from jax.experimental import pallas as pl
from jax.experimental.pallas import tpu as pltpu

F32 = jnp.float32
BF16 = jnp.bfloat16

D_MODEL = 1024
GRID_W = 64
N_HEADS = 8
N_KV_HEADS = 2
HEAD_DIM = 64
ATTN_WIDTH = N_HEADS * HEAD_DIM
KV_WIDTH = N_KV_HEADS * HEAD_DIM
QK_WIDTH = ATTN_WIDTH + KV_WIDTH
ROPE_THETA = 10000.0
SSM_WIDTH = D_MODEL - ATTN_WIDTH
SSM_GROUP = 16
SSM_GROUPS = SSM_WIDTH // SSM_GROUP
SSM_STATE = 64
SGU_HEADS = 8
SGU_CHUNK = 128
D_FF = 4 * D_MODEL
DEPTH = 2
AB_IN = ATTN_WIDTH + 2 * KV_WIDTH + SSM_WIDTH
ALPHA = (2 * DEPTH) ** 0.25
EPS = 1e-6

LANES = 128
VMEM_LIMIT_BYTES = 56 * 1024 * 1024

SSM_CHUNK = 64
SSM_ROW = SSM_CHUNK * SSM_GROUP
SSM_LAGS = 2 * SSM_CHUNK
SSM_TAPS = SSM_LAGS * SSM_GROUP
SSM_SDIM = 4 * SSM_STATE

TOKEN_TILE = 512


def _cparams(sem):
    return pltpu.CompilerParams(dimension_semantics=sem, vmem_limit_bytes=VMEM_LIMIT_BYTES)


def _const_spec(shape):
    nd = len(shape)
    return pl.BlockSpec(shape, lambda *_: (0,) * nd, pipeline_mode=pl.Buffered(1))


def _layer_norm(r, g, b):
    mu = jnp.mean(r, axis=-1, keepdims=True)
    d = r - mu
    var = jnp.mean(d * d, axis=-1, keepdims=True)
    return d * lax.rsqrt(var + EPS) * g + b


def _split_bf16(x):
    hi = x.astype(BF16)
    lo = (x - hi.astype(F32)).astype(BF16)
    return hi, lo


def _select_cols(x, e):
    hi, lo = _split_bf16(x)
    y = jnp.dot(jnp.concatenate([hi, lo], axis=0), e, preferred_element_type=F32)
    return y[:x.shape[0]] + y[x.shape[0]:]


def _select_rows(e, x):
    hi, lo = _split_bf16(x)
    y = jnp.dot(e, jnp.concatenate([hi, lo], axis=1), preferred_element_type=F32)
    return y[:, :x.shape[1]] + y[:, x.shape[1]:]


def _inproj_kernel(x_ref, w_ref, gain_ref, cos_ref, sin_ref, hmean_ref, q_ref, k_ref, v_ref, u_ref):
    subs = [slice(r0, r0 + INPROJ_SUB) for r0 in range(0, x_ref.shape[0], INPROJ_SUB)]
    hs = [jnp.dot(x_ref[sl, :].astype(BF16), w_ref[...], preferred_element_type=F32) for sl in subs]
    mss = [jnp.dot((h[:, :QK_WIDTH] * h[:, :QK_WIDTH]).astype(BF16), hmean_ref[...], preferred_element_type=F32)
           for h in hs]
    qkns = [h[:, :QK_WIDTH] * lax.rsqrt(ms + EPS) * gain_ref[...] for h, ms in zip(hs, mss)]
    even = (lax.broadcasted_iota(jnp.int32, (INPROJ_SUB, LANES), 1) % 2) == 0
    for sl, qkn in zip(subs, qkns):
        c = cos_ref[sl, :]
        s = sin_ref[sl, :]
        for j in range(QK_WIDTH // LANES):
            xs = qkn[:, j * LANES:(j + 1) * LANES]
            nxt = pltpu.roll(xs, LANES - 1, 1)
            prv = pltpu.roll(xs, 1, 1)
            roped = (xs * c + jnp.where(even, nxt, prv) * s).astype(BF16)
            if j < ATTN_WIDTH // LANES:
                q_ref[sl, j * LANES:(j + 1) * LANES] = roped
            else:
                k_ref[sl, :] = roped
    for sl, h in zip(subs, hs):
        v_ref[:, sl] = h[:, QK_WIDTH:QK_WIDTH + KV_WIDTH].T.astype(BF16)
        for r in range(INPROJ_SUB // LANES):
            blk = h[r * LANES:(r + 1) * LANES, QK_WIDTH + KV_WIDTH:]
            u_ref[:, sl.start // LANES + r, :] = blk.T


INPROJ_TILE = 1024
INPROJ_SUB = 256


def _inproj(x2, w_in, gain, cos_t, sin_t, hmean, seq_len):
    n = x2.shape[0]
    tm = INPROJ_TILE
    nper = seq_len // tm
    return pl.pallas_call(
        _inproj_kernel,
        grid=(n // tm,),
        in_specs=[
            pl.BlockSpec((tm, D_MODEL), lambda i: (i, 0)),
            _const_spec((D_MODEL, AB_IN)),
            _const_spec((1, QK_WIDTH)),
            pl.BlockSpec((tm, LANES), lambda i: (i % nper, 0)),
            pl.BlockSpec((tm, LANES), lambda i: (i % nper, 0)),
            _const_spec((QK_WIDTH, QK_WIDTH)),
        ],
        out_specs=[
            pl.BlockSpec((tm, ATTN_WIDTH), lambda i: (i, 0)),
            pl.BlockSpec((tm, KV_WIDTH), lambda i: (i, 0)),
            pl.BlockSpec((KV_WIDTH, tm), lambda i: (0, i)),
            pl.BlockSpec((SSM_WIDTH, tm // LANES, LANES), lambda i: (0, i, 0)),
        ],
        out_shape=[
            jax.ShapeDtypeStruct((n, ATTN_WIDTH), BF16),
            jax.ShapeDtypeStruct((n, KV_WIDTH), BF16),
            jax.ShapeDtypeStruct((KV_WIDTH, n), BF16),
            jax.ShapeDtypeStruct((SSM_WIDTH, n // LANES, LANES), F32),
        ],
        compiler_params=_cparams(("parallel",)),
        name="inproj",
    )(x2, w_in, gain, cos_t, sin_t, hmean)


ATT_UNIT_Q = 64


ATT_SHIFT_LIMIT = 60.0


def _attn_kernel(bound_ref, q_ref, k_ref, vt_ref, o_ref,
                 s0_ref, s1_ref, p0_ref, p1_ref, m0_ref, m1_ref, l0_ref, l1_ref):
    seq_len = k_ref.shape[1]
    n_units = seq_len // ATT_UNIT_Q
    rep = N_HEADS // N_KV_HEADS
    lane = lax.broadcasted_iota(jnp.int32, (ATT_UNIT_Q, LANES), 1)
    lo_half = lane < HEAD_DIM
    zero = jnp.zeros((ATT_UNIT_Q, LANES), BF16)
    bound = bound_ref[0, 0]

    def rows_of(t):
        return pl.ds(pl.multiple_of(t * ATT_UNIT_Q, ATT_UNIT_Q), ATT_UNIT_Q)

    def raw_scores(t):
        q = q_ref[0, rows_of(t), :]
        slots = [q[:, m * LANES:(m + 1) * LANES] for m in range(rep)]
        qs = jnp.concatenate([jnp.where(lo_half, x, zero) for x in slots]
                             + [jnp.where(lo_half, zero, x) for x in slots], axis=0)
        return lax.dot_general(k_ref[0], qs, (((1,), (1,)), ((), ())), preferred_element_type=F32)

    def scores(t, s_ref, m_ref):
        s = raw_scores(t)
        s_ref[...] = s
        m_ref[...] = jnp.max(s, axis=0, keepdims=True)

    def probs(s_ref, m_ref, p_ref, l_ref):
        p = jnp.exp2(s_ref[...] - m_ref[...])
        l_ref[...] = jnp.sum(p, axis=0, keepdims=True)
        p_ref[...] = p.astype(BF16)

    def bounded_probs(t, p_ref, l_ref):
        p = jnp.exp2(raw_scores(t) - bound)
        l_ref[...] = jnp.sum(p, axis=0, keepdims=True)
        p_ref[...] = p.astype(BF16)

    def output(t, p_ref, l_ref):
        o_t = jnp.dot(vt_ref[...], p_ref[...], preferred_element_type=F32) / l_ref[...]
        o = o_t.T
        for m in range(rep):
            g0 = o[m * ATT_UNIT_Q:(m + 1) * ATT_UNIT_Q]
            g1 = o[(rep + m) * ATT_UNIT_Q:(rep + m + 1) * ATT_UNIT_Q]
            o_ref[0, rows_of(t), m * LANES:(m + 1) * LANES] = jnp.where(lo_half, g0, g1).astype(BF16)

    @pl.when(bound <= ATT_SHIFT_LIMIT)
    def _bounded_shift():
        bounded_probs(0, p0_ref, l0_ref)

        def body(i, carry):
            t = 2 * i + 1
            bounded_probs(t, p1_ref, l1_ref)
            output(t - 1, p0_ref, l0_ref)
            bounded_probs(t + 1, p0_ref, l0_ref)
            output(t, p1_ref, l1_ref)
            return carry

        lax.fori_loop(0, (n_units - 2) // 2, body, 0)
        bounded_probs(n_units - 1, p1_ref, l1_ref)
        output(n_units - 2, p0_ref, l0_ref)
        output(n_units - 1, p1_ref, l1_ref)

    @pl.when(bound > ATT_SHIFT_LIMIT)
    def _row_max_shift():
        scores(0, s0_ref, m0_ref)
        scores(1, s1_ref, m1_ref)
        probs(s0_ref, m0_ref, p0_ref, l0_ref)

        def body(i, carry):
            t = 2 * i + 2
            scores(t, s0_ref, m0_ref)
            probs(s1_ref, m1_ref, p1_ref, l1_ref)
            output(t - 2, p0_ref, l0_ref)
            scores(t + 1, s1_ref, m1_ref)
            probs(s0_ref, m0_ref, p0_ref, l0_ref)
            output(t - 1, p1_ref, l1_ref)
            return carry

        lax.fori_loop(0, (n_units - 2) // 2, body, 0)
        probs(s1_ref, m1_ref, p1_ref, l1_ref)
        output(n_units - 2, p0_ref, l0_ref)
        output(n_units - 1, p1_ref, l1_ref)


def _attention(bound, q, k, v_t):
    b, seq_len, _ = q.shape
    cols = ATT_UNIT_Q * N_HEADS
    seq = lambda w: pl.BlockSpec((1, seq_len, w), lambda i: (i, 0, 0))
    return pl.pallas_call(
        _attn_kernel,
        grid=(b,),
        in_specs=[pl.BlockSpec(memory_space=pltpu.SMEM), seq(ATTN_WIDTH), seq(KV_WIDTH),
                  pl.BlockSpec((KV_WIDTH, seq_len), lambda i: (0, i))],
        out_specs=seq(ATTN_WIDTH),
        out_shape=jax.ShapeDtypeStruct((b, seq_len, ATTN_WIDTH), BF16),
        scratch_shapes=[pltpu.VMEM((seq_len, cols), F32), pltpu.VMEM((seq_len, cols), F32),
                        pltpu.VMEM((seq_len, cols), BF16), pltpu.VMEM((seq_len, cols), BF16),
                        pltpu.VMEM((1, cols), F32), pltpu.VMEM((1, cols), F32),
                        pltpu.VMEM((1, cols), F32), pltpu.VMEM((1, cols), F32)],
        compiler_params=_cparams(("parallel",)),
        name="attention",
    )(bound, q, k, v_t)


SSM_PITCH = 132


def _s5_kernel(xp_ref, xs_ref, pc_ref, pr2_ref, bcr_ref, bci_ref, ccat_ref, cr4_ref, ci4_ref, dt_ref,
               e16_ref, x16_ref, yp_ref, ys_ref, s_ref, zt_ref, m_ref, yt_ref, *, pairs):
    ns = SSM_STATE
    pc = pc_ref[0]
    are_c, aim_c, dt_c = pc[:, 0:1], pc[:, 1:2], jnp.exp(pc[:, 2:3])
    row = lax.broadcasted_iota(jnp.int32, (2 * ns, LANES), 0)
    lan = lax.broadcasted_iota(jnp.int32, (2 * ns, LANES), 1)
    b_rows = row >= ns
    n_g = jnp.where(b_rows, lan - (SSM_CHUNK - 1), (SSM_CHUNK - 1) - lan)
    mask = n_g >= 0
    nf = jnp.where(mask, n_g, 0).astype(F32)
    rho = jnp.exp(are_c * dt_c * nf)
    ang = aim_c * dt_c * nf
    cs = jnp.where(mask, rho * jnp.cos(ang), 0.0)
    sn = jnp.where(mask, rho * jnp.sin(ang), 0.0)
    b_col = b_rows[:, 0:1]
    a1r = jnp.where(b_col, cs[:, SSM_CHUNK:SSM_CHUNK + 1], cs[:, SSM_CHUNK - 2:SSM_CHUNK - 1]) - 1.0
    a1i = jnp.where(b_col, sn[:, SSM_CHUNK:SSM_CHUNK + 1], sn[:, SSM_CHUNK - 2:SSM_CHUNK - 1])
    den = are_c * are_c + aim_c * aim_c
    cr = (a1r * are_c + a1i * aim_c) / den
    ci = (a1i * are_c - a1r * aim_c) / den

    def tile_lanes(x):
        return jnp.concatenate([x] * (SSM_TAPS // LANES), axis=1)

    bre, bim = bcr_ref[0], bci_ref[0]
    bbr = tile_lanes(cr * bre - ci * bim)
    bbi = tile_lanes(cr * bim + ci * bre)
    base = jnp.concatenate([cs[:ns], sn[:ns], cs[ns:], sn[ns:]], axis=0)
    ex = _select_cols(base, e16_ref[...])
    ecf, esf, ecb, esb = ex[:ns], ex[ns:2 * ns], ex[2 * ns:3 * ns], ex[3 * ns:]
    z_fre = ecf * bbr[:ns] - esf * bbi[:ns]
    z_fim = esf * bbr[:ns] + ecf * bbi[:ns]
    z_bre = ecb * bbr[ns:] - esb * bbi[ns:]
    z_bim = esb * bbr[ns:] + ecb * bbi[ns:]

    g_tab = jnp.concatenate([z_fre, -z_fim, z_bre, -z_bim], axis=0)
    ch, cl = _split_bf16(ccat_ref[0])
    gh, gl = _split_bf16(g_tab)
    w = (jnp.dot(ch, gh, preferred_element_type=F32) + jnp.dot(ch, gl, preferred_element_type=F32)
         + jnp.dot(cl, gh, preferred_element_type=F32))
    wrow = lax.broadcasted_iota(jnp.int32, (SSM_GROUP, SSM_TAPS), 0)
    wcol = lax.broadcasted_iota(jnp.int32, (SSM_GROUP, SSM_TAPS), 1)
    on_diag = (wcol // SSM_GROUP == SSM_CHUNK - 1) & (wcol % SSM_GROUP == wrow)
    w = w + jnp.where(on_diag, dt_ref[0], 0.0)
    per_vreg = LANES // SSM_GROUP
    rolled = [w] + [pltpu.roll(w, SSM_TAPS - SSM_GROUP * kk, 1) for kk in range(1, per_vreg)]
    for j in range(SSM_CHUNK):
        r = SSM_CHUNK - 1 - j
        blk = rolled[r % per_vreg][:, (r // per_vreg) * LANES:(r // per_vreg) * LANES + SSM_ROW]
        m_ref[j * SSM_GROUP:(j + 1) * SSM_GROUP, :] = blk.astype(BF16)

    p2 = jnp.concatenate([z_fre[:, :SSM_ROW], z_fim[:, :SSM_ROW], z_bre[:, SSM_ROW:], z_bim[:, SSM_ROW:]],
                         axis=0).astype(BF16)

    pr2 = pr2_ref[0]
    j_r = lax.broadcasted_iota(jnp.int32, (SSM_CHUNK, LANES), 0)
    l_r = lax.broadcasted_iota(jnp.int32, (SSM_CHUNK, LANES), 1)
    lo_r = l_r < ns
    n_q = jnp.where(lo_r, j_r + 1, (SSM_CHUNK - 1) - j_r).astype(F32)
    dt2 = jnp.exp(pr2[2:3, :])
    rho = jnp.exp(pr2[0:1, :] * dt2 * n_q)
    ang = pr2[1:2, :] * dt2 * n_q
    c2 = rho * jnp.cos(ang)
    s2 = rho * jnp.sin(ang)
    base_q = jnp.concatenate([jnp.where(lo_r, c2, -pltpu.roll(s2, ns, 1)),
                              jnp.where(lo_r, pltpu.roll(c2, ns, 1), -s2)], axis=1)
    a1x = _select_rows(x16_ref[...], base_q)
    sign = jnp.where(lax.broadcasted_iota(jnp.int32, (SSM_ROW, LANES), 1) < ns, 1.0, -1.0)
    a2x = jnp.concatenate([pltpu.roll(a1x[:, :LANES], ns, 1) * sign,
                           pltpu.roll(a1x[:, LANES:], ns, 1) * sign], axis=1)
    q2 = (a1x * jnp.concatenate([cr4_ref[0]] * SSM_CHUNK, axis=0)
          + a2x * jnp.concatenate([ci4_ref[0]] * SSM_CHUNK, axis=0)).astype(BF16)

    n_s = (SSM_CHUNK * jnp.left_shift(1, jnp.minimum(lan, 8))).astype(F32)
    rho = jnp.exp(are_c * dt_c * n_s)
    ang = aim_c * dt_c * n_s
    sc_r = rho * jnp.cos(ang)
    sc_i = rho * jnp.sin(ang)

    for seg, x_ref in enumerate((xp_ref, xs_ref)):
        for hi in range(SSM_GROUP):
            s_ref[seg, hi * SSM_PITCH:hi * SSM_PITCH + LANES, :] = x_ref[hi].T
    for par in range(2):
        for i in range(SSM_CHUNK):
            for seg in range(2):
                rows16 = s_ref[seg, pl.ds(par * SSM_CHUNK + i, SSM_GROUP, stride=SSM_PITCH), :]
                blk = 2 * par + seg
                zt_ref[i * SSM_GROUP:(i + 1) * SSM_GROUP, blk * LANES:(blk + 1) * LANES] = rows16.astype(BF16)

    zt = zt_ref[...]
    yt = jnp.dot(m_ref[...], zt, preferred_element_type=F32)
    st = jnp.dot(p2, zt, preferred_element_type=F32)

    def cmul(kk, lo, v):
        ar, ai = sc_r[lo:lo + ns, kk:kk + 1], sc_i[lo:lo + ns, kk:kk + 1]
        return jnp.concatenate([ar * v[:ns] - ai * v[ns:], ar * v[ns:] + ai * v[:ns]], axis=0)

    carried = [None] * 4
    for seg, npairs in enumerate(pairs):
        c2i = lax.broadcasted_iota(jnp.int32, (2 * ns, LANES), 1) % npairs
        right = lambda v, d: jnp.where(c2i >= d, pltpu.roll(v, d, 1), 0.0)
        left = lambda v, d: jnp.where(c2i < npairs - d, pltpu.roll(v, LANES - d, 1), 0.0)
        ev, od = st[:, seg * LANES:(seg + 1) * LANES], st[:, (2 + seg) * LANES:(3 + seg) * LANES]
        ef, of, eb, ob = ev[:2 * ns], od[:2 * ns], ev[2 * ns:], od[2 * ns:]
        ef, of = ef + cmul(0, 0, right(of, 1)), of + cmul(0, 0, ef)
        eb, ob = eb + cmul(0, ns, ob), ob + cmul(0, ns, left(eb, 1))
        for kk in range(1, int(math.log2(npairs)) + 1):
            d = 1 << (kk - 1)
            ef, of = ef + cmul(kk, 0, right(ef, d)), of + cmul(kk, 0, right(of, d))
            eb, ob = eb + cmul(kk, ns, left(eb, d)), ob + cmul(kk, ns, left(ob, d))
        carried[seg] = jnp.concatenate([right(of, 1), ob], axis=0)
        carried[2 + seg] = jnp.concatenate([ef, left(eb, 1)], axis=0)
    xin = jnp.concatenate(carried, axis=1).astype(BF16)
    yt = yt + jnp.dot(q2, xin, preferred_element_type=F32)

    for blk in range(4):
        yt_ref[blk] = yt[:, blk * LANES:(blk + 1) * LANES]
    for seg, y_ref in enumerate((yp_ref, ys_ref)):
        for ho in range(SSM_GROUP):
            ev = yt_ref[seg, pl.ds(ho, SSM_CHUNK, stride=SSM_GROUP), :]
            od = yt_ref[2 + seg, pl.ds(ho, SSM_CHUNK, stride=SSM_GROUP), :]
            y_ref[ho] = jnp.concatenate([ev, od], axis=0).T


def _s5(u_p, u_s, pc, pr2, bcr, bci, ccat, cr4, ci4, dtile, e16, x16, pairs):
    g = SSM_GROUPS
    rows = u_p.shape[1]
    assert u_s.shape[1] == rows == LANES
    grp = lambda shape: pl.BlockSpec((1,) + shape, lambda i: (i, 0, 0))
    chan = pl.BlockSpec((SSM_GROUP, rows, LANES), lambda i: (i, 0, 0))
    return pl.pallas_call(
        functools.partial(_s5_kernel, pairs=pairs),
        grid=(g,),
        in_specs=[chan, chan, grp((2 * SSM_STATE, 8)), grp((8, LANES)),
                  grp((2 * SSM_STATE, LANES)), grp((2 * SSM_STATE, LANES)),
                  grp((SSM_GROUP, SSM_SDIM)), grp((SSM_GROUP, SSM_SDIM)), grp((SSM_GROUP, SSM_SDIM)),
                  grp((1, SSM_TAPS)), _const_spec((LANES, SSM_TAPS)), _const_spec((SSM_ROW, SSM_CHUNK))],
        out_specs=[chan, chan],
        out_shape=[jax.ShapeDtypeStruct(u_p.shape, F32), jax.ShapeDtypeStruct(u_s.shape, F32)],
        scratch_shapes=[pltpu.VMEM((2, SSM_GROUP * SSM_PITCH, LANES), F32),
                        pltpu.VMEM((SSM_ROW, 4 * LANES), BF16),
                        pltpu.VMEM((SSM_ROW, SSM_ROW), BF16),
                        pltpu.VMEM((4, SSM_ROW, LANES), F32)],
        compiler_params=_cparams(("parallel",)),
        name="s5",
    )(u_p, u_s, pc, pr2, bcr, bci, ccat, cr4, ci4, dtile, e16, x16)


FF_CHUNK = 1024
MIXER_ROWS = 128


def _ffn_ln(x, w1_ref, w2_ref, g, b):
    half = x.shape[0] // 2
    xs = [x[:half], x[half:]]
    xbs = [v.astype(BF16) for v in xs]
    accs = [ALPHA * v for v in xs]
    for c in range(D_FF // FF_CHUNK):
        for i in range(2):
            h = jnp.dot(xbs[i], w1_ref[0, :, c * FF_CHUNK:(c + 1) * FF_CHUNK], preferred_element_type=F32)
            h = jnp.maximum(h, 0.0)
            accs[i] = accs[i] + jnp.dot((h * h).astype(BF16), w2_ref[0, c * FF_CHUNK:(c + 1) * FF_CHUNK, :],
                                        preferred_element_type=F32)
    return jnp.concatenate([_layer_norm(a, g, b) for a in accs], axis=0)


def _layer0_tail_kernel(x_ref, a_ref, y_ref, wg_ref, bg_ref, woa_ref, wos_ref, ln_ref, w1_ref, w2_ref, o_ref):
    rows_per_tile = x_ref.shape[0] // LANES
    row0 = (pl.program_id(0) % 2) * rows_per_tile
    ln = ln_ref[...]
    subs = [slice(r * LANES, (r + 1) * LANES) for r in range(rows_per_tile)]
    ys = [jax.nn.gelu(y_ref[:, row0 + r, :].T) for r in range(rows_per_tile)]
    gates = [jnp.dot(y.astype(BF16), wg_ref[...], preferred_element_type=F32) + bg_ref[...] for y in ys]
    zs = [(y * jax.nn.sigmoid(g)).astype(BF16) for y, g in zip(ys, gates)]
    mixes = [jnp.dot(a_ref[sl, :], woa_ref[...], preferred_element_type=F32)
             + jnp.dot(z, wos_ref[...], preferred_element_type=F32) for sl, z in zip(subs, zs)]
    x1 = jnp.concatenate([_layer_norm(ALPHA * x_ref[sl, :] + m, ln[0:1], ln[1:2]) for sl, m in zip(subs, mixes)],
                         axis=0)
    o_ref[...] = _ffn_ln(x1, w1_ref, w2_ref, ln[2:3], ln[3:4])


def _layer_weights_spec(shape, layer):
    return pl.BlockSpec((1,) + shape, lambda i: (layer, 0, 0), pipeline_mode=pl.Buffered(1))


def _layer0_tail(x2, attn, y, w_glu, b_glu, w_out_a, w_out_s, ln4, w1, w2):
    n = x2.shape[0]
    tm = TOKEN_TILE
    tok = lambda w: pl.BlockSpec((tm, w), lambda i: (i, 0))
    return pl.pallas_call(
        _layer0_tail_kernel,
        grid=(n // tm,),
        in_specs=[tok(D_MODEL), tok(ATTN_WIDTH),
                  pl.BlockSpec((SSM_WIDTH, 2 * tm // LANES, LANES), lambda i: (0, i // 2, 0)),
                  _const_spec((SSM_WIDTH, SSM_WIDTH)), _const_spec((1, SSM_WIDTH)),
                  _const_spec((ATTN_WIDTH, D_MODEL)), _const_spec((SSM_WIDTH, D_MODEL)),
                  _const_spec((4, D_MODEL)),
                  _layer_weights_spec((D_MODEL, D_FF), 0), _layer_weights_spec((D_FF, D_MODEL), 0)],
        out_specs=tok(D_MODEL),
        out_shape=jax.ShapeDtypeStruct((n, D_MODEL), F32),
        compiler_params=_cparams(("parallel",)),
        name="layer0_tail",
    )(x2, attn, y, w_glu, b_glu, w_out_a, w_out_s, ln4, w1, w2)


def _layer1_kernel(x_ref, win_ref, cln_ref, ws_ref, bs_ref, wout_ref, ln_ref, w1_ref, w2_ref, o_ref):
    cln = cln_ref[...]
    ln = ln_ref[...]
    halves = [x_ref[r0:r0 + MIXER_ROWS, :] for r0 in range(0, x_ref.shape[0], MIXER_ROWS)]
    hs = [jax.nn.gelu(jnp.dot(x.astype(BF16), win_ref[...], preferred_element_type=F32)) for x in halves]
    vbs = [_layer_norm(h[:, D_MODEL:], cln[0:1], cln[1:2]).astype(BF16) for h in hs]
    gated = []
    for h, vb in zip(hs, vbs):
        rows = []
        for c in range(MIXER_ROWS // SGU_CHUNK):
            cols = []
            for hd in range(SGU_HEADS):
                blk = vb[c * SGU_CHUNK:(c + 1) * SGU_CHUNK, hd * LANES:(hd + 1) * LANES]
                cols.append(jnp.dot(ws_ref[hd], blk, preferred_element_type=F32) + bs_ref[hd])
            rows.append(jnp.concatenate(cols, axis=1))
        gated.append((h[:, :D_MODEL] * jnp.concatenate(rows, axis=0)).astype(BF16))
    outs = [jnp.dot(g, wout_ref[...], preferred_element_type=F32) for g in gated]
    x1 = jnp.concatenate([_layer_norm(ALPHA * x + o, ln[0:1], ln[1:2]) for x, o in zip(halves, outs)], axis=0)
    o_ref[...] = _ffn_ln(x1, w1_ref, w2_ref, ln[2:3], ln[3:4])


def _layer1(x2, w_in, cln2, w_s, b_s_full, w_out, ln4, w1, w2):
    n = x2.shape[0]
    tm = TOKEN_TILE
    tok = pl.BlockSpec((tm, D_MODEL), lambda i: (i, 0))
    return pl.pallas_call(
        _layer1_kernel,
        grid=(n // tm,),
        in_specs=[tok, _const_spec((D_MODEL, 2 * D_MODEL)), _const_spec((2, D_MODEL)),
                  _const_spec((SGU_HEADS, SGU_CHUNK, SGU_CHUNK)), _const_spec((SGU_HEADS, SGU_CHUNK, LANES)),
                  _const_spec((D_MODEL, D_MODEL)), _const_spec((4, D_MODEL)),
                  _layer_weights_spec((D_MODEL, D_FF), 1), _layer_weights_spec((D_FF, D_MODEL), 1)],
        out_specs=tok,
        out_shape=jax.ShapeDtypeStruct((n, D_MODEL), F32),
        compiler_params=_cparams(("parallel",)),
        name="layer1",
    )(x2, w_in, cln2, w_s, b_s_full, w_out, ln4, w1, w2)


_HEAD_OF_SLOT = np.array([0, 4, 1, 5, 2, 6, 3, 7])


def _rope_tables(seq_len):
    t = np.arange(seq_len)
    axis_dim = HEAD_DIM // 2
    inv_freq = ROPE_THETA ** (-np.arange(0, axis_dim, 2, dtype=np.float64) / axis_dim)
    ang = np.concatenate([(t // GRID_W)[:, None] * inv_freq, (t % GRID_W)[:, None] * inv_freq], axis=-1)
    cos = np.repeat(np.cos(ang), 2, axis=-1)
    sin = np.repeat(np.sin(ang), 2, axis=-1) * np.tile([-1.0, 1.0], HEAD_DIM // 2)
    reps = LANES // HEAD_DIM
    return (jnp.asarray(np.tile(cos, (1, reps)), F32), jnp.asarray(np.tile(sin, (1, reps)), F32))


def _selectors():
    col = np.arange(SSM_TAPS)
    row = np.arange(LANES)
    e16 = (col[None, :] // SSM_GROUP == row[:, None])
    r2 = np.arange(SSM_ROW)
    x16 = (r2[:, None] // SSM_GROUP == np.arange(SSM_CHUNK)[None, :])
    as_bf16 = lambda a: jnp.asarray(a.astype(np.float32), BF16)
    return as_bf16(e16), as_bf16(x16)


def _s5_param_layout(a_re_f, a_im_f, ls_f, a_re_b, a_im_b, ls_b, b_re, b_im, c_re, c_im, d):
    g, ns = SSM_GROUPS, SSM_STATE
    zeros = jnp.zeros((g, ns), F32)
    per = lambda a_re, a_im, ls: jnp.stack([a_re, a_im, jnp.broadcast_to(ls[:, None], (g, ns))] + [zeros] * 5,
                                           axis=1)
    pr2 = jnp.concatenate([per(a_re_f, a_im_f, ls_f), per(a_re_b, a_im_b, ls_b)], axis=2)
    pc = jnp.transpose(pr2, (0, 2, 1))
    pad_lanes = lambda b: jnp.tile(jnp.concatenate([b, b], axis=1), (1, 1, LANES // SSM_GROUP))
    ccat = jnp.concatenate([c_re, c_im, c_re, c_im], axis=2)
    cr4 = jnp.concatenate([c_re] * 4, axis=2)
    ci4 = jnp.concatenate([c_im] * 4, axis=2)
    dtile = jnp.tile(d.reshape(g, 1, SSM_GROUP), (1, 1, SSM_LAGS))
    return pc, pr2, pad_lanes(b_re), pad_lanes(b_im), ccat, cr4, ci4, dtile


def kernel(x_prompt, x_sample, ab_w_in, ab_q_norm, ab_k_norm, ssm_a_re_f, ssm_a_im_f, ssm_log_step_f,
           ssm_a_re_b, ssm_a_im_b, ssm_log_step_b, ssm_b_re, ssm_b_im, ssm_c_re, ssm_c_im, ssm_d, ssm_w_glu,
           ssm_b_glu, ab_w_out, c_w_in, c_ln_g, c_ln_b, c_w_s, c_b_s, c_w_out, ln_mix_g, ln_mix_b, ff_w1, ff_w2,
           ln_ff_g, ln_ff_b):
    xs = [x_prompt, x_sample]
    row2 = lambda v: v.reshape(1, -1)

    slot_cols = (_HEAD_OF_SLOT[:, None] * HEAD_DIM + np.arange(HEAD_DIM)[None, :]).reshape(-1)
    w_in = ab_w_in[0]
    w_in = jnp.concatenate([w_in[:, slot_cols], w_in[:, ATTN_WIDTH:]], axis=1).astype(BF16)
    scale = HEAD_DIM ** -0.5 * math.log2(math.e)
    gain = jnp.concatenate([jnp.tile(ab_q_norm[0] * scale, N_HEADS), jnp.tile(ab_k_norm[0], N_KV_HEADS)])
    score_bound = (1.05 * HEAD_DIM * scale * jnp.max(jnp.abs(ab_q_norm[0])) * jnp.max(jnp.abs(ab_k_norm[0])))
    score_bound = score_bound.reshape(1, 1).astype(F32)
    hid = np.arange(QK_WIDTH) // HEAD_DIM
    hmean = jnp.asarray((hid[:, None] == hid[None, :]).astype(np.float32) / HEAD_DIM, BF16)
    w_out = ab_w_out[0]
    w_out_a = w_out[:ATTN_WIDTH][slot_cols].astype(BF16)
    w_out_s = w_out[ATTN_WIDTH:].astype(BF16)
    w_glu = ssm_w_glu[0].astype(BF16)

    attn, us, pairs = [], [], []
    cos_t, sin_t = _rope_tables(max(x.shape[1] for x in xs))
    for x in xs:
        b, seq_len, _ = x.shape
        q, k, v_t, u = _inproj(x.reshape(b * seq_len, D_MODEL), w_in, row2(gain), cos_t, sin_t, hmean, seq_len)
        attn.append(_attention(score_bound, q.reshape(b, seq_len, ATTN_WIDTH), k.reshape(b, seq_len, KV_WIDTH),
                               v_t).reshape(b * seq_len, ATTN_WIDTH))
        us.append(u)
        pairs.append(seq_len // (2 * SSM_CHUNK))
    ys = _s5(us[0], us[1], *_s5_param_layout(
        ssm_a_re_f[0], ssm_a_im_f[0], ssm_log_step_f[0], ssm_a_re_b[0], ssm_a_im_b[0], ssm_log_step_b[0],
        ssm_b_re[0], ssm_b_im[0], ssm_c_re[0], ssm_c_im[0], ssm_d[0]), *_selectors(), tuple(pairs))

    ff_w1b = ff_w1.astype(BF16)
    ff_w2b = ff_w2.astype(BF16)
    ln4 = [jnp.stack([ln_mix_g[i], ln_mix_b[i], ln_ff_g[i], ln_ff_b[i]]) for i in range(DEPTH)]
    cln2 = jnp.stack([c_ln_g[0], c_ln_b[0]])
    bs_full = jnp.broadcast_to(c_b_s[0][:, :, None], (SGU_HEADS, SGU_CHUNK, LANES))
    c_w_in_b, c_w_s_b, c_w_out_b = c_w_in[0].astype(BF16), c_w_s[0].astype(BF16), c_w_out[0].astype(BF16)
    outs = []
    for x, a, y in zip(xs, attn, ys):
        b, seq_len, _ = x.shape
        n = b * seq_len
        h = _layer0_tail(x.reshape(n, D_MODEL), a, y, w_glu, row2(ssm_b_glu[0]), w_out_a, w_out_s,
                         ln4[0], ff_w1b, ff_w2b)
        h = _layer1(h, c_w_in_b, cln2, c_w_s_b, bs_full, c_w_out_b, ln4[1], ff_w1b, ff_w2b)
        outs.append(h.reshape(b, seq_len, D_MODEL))
    return tuple(outs)
```

```python
import functools
import math

import numpy as np
import jax
import jax.numpy as jnp
from jax import lax
from jax.experimental import pallas as pl
from jax.experimental.pallas import tpu as pltpu

F32 = jnp.float32
BF16 = jnp.bfloat16

D_MODEL = 1024
GRID_W = 64
N_HEADS = 8
N_KV_HEADS = 2
HEAD_DIM = 64
ATTN_WIDTH = N_HEADS * HEAD_DIM
KV_WIDTH = N_KV_HEADS * HEAD_DIM
QK_WIDTH = ATTN_WIDTH + KV_WIDTH
ROPE_THETA = 10000.0
SSM_WIDTH = D_MODEL - ATTN_WIDTH
SSM_GROUP = 16
SSM_GROUPS = SSM_WIDTH // SSM_GROUP
SSM_STATE = 64
SGU_HEADS = 8
SGU_CHUNK = 128
D_FF = 4 * D_MODEL
DEPTH = 2
AB_IN = ATTN_WIDTH + 2 * KV_WIDTH + SSM_WIDTH
ALPHA = (2 * DEPTH) ** 0.25
EPS = 1e-6

LANES = 128
VMEM_LIMIT_BYTES = 56 * 1024 * 1024

SSM_CHUNK = 64
SSM_ROW = SSM_CHUNK * SSM_GROUP
SSM_LAGS = 2 * SSM_CHUNK
SSM_TAPS = SSM_LAGS * SSM_GROUP
SSM_SDIM = 4 * SSM_STATE

TOKEN_TILE = 1024


def _cparams(sem):
    return pltpu.CompilerParams(dimension_semantics=sem, vmem_limit_bytes=VMEM_LIMIT_BYTES)


def _const_spec(shape):
    nd = len(shape)
    return pl.BlockSpec(shape, lambda *_: (0,) * nd, pipeline_mode=pl.Buffered(1))


def _layer_norm(r, g, b):
    mu = jnp.mean(r, axis=-1, keepdims=True)
    d = r - mu
    var = jnp.mean(d * d, axis=-1, keepdims=True)
    return d * lax.rsqrt(var + EPS) * g + b


def _split_bf16(x):
    hi = x.astype(BF16)
    lo = (x - hi.astype(F32)).astype(BF16)
    return hi, lo


def _select_cols(x, e):
    hi, lo = _split_bf16(x)
    y = jnp.dot(jnp.concatenate([hi, lo], axis=0), e, preferred_element_type=F32)
    return y[:x.shape[0]] + y[x.shape[0]:]


def _select_rows(e, x):
    hi, lo = _split_bf16(x)
    y = jnp.dot(e, jnp.concatenate([hi, lo], axis=1), preferred_element_type=F32)
    return y[:, :x.shape[1]] + y[:, x.shape[1]:]


def _inproj_kernel(x_ref, w_ref, gain_ref, cos_ref, sin_ref, hmean_ref, q_ref, k_ref, v_ref, u_ref):
    subs = [slice(r0, r0 + INPROJ_SUB) for r0 in range(0, x_ref.shape[0], INPROJ_SUB)]
    hs = [jnp.dot(x_ref[sl, :].astype(BF16), w_ref[...], preferred_element_type=F32) for sl in subs]
    mss = [jnp.dot((h[:, :QK_WIDTH] * h[:, :QK_WIDTH]).astype(BF16), hmean_ref[...], preferred_element_type=F32)
           for h in hs]
    qkns = [h[:, :QK_WIDTH] * lax.rsqrt(ms + EPS) * gain_ref[...] for h, ms in zip(hs, mss)]
    even = (lax.broadcasted_iota(jnp.int32, (INPROJ_SUB, LANES), 1) % 2) == 0
    for sl, qkn in zip(subs, qkns):
        c = cos_ref[sl, :]
        s = sin_ref[sl, :]
        for j in range(QK_WIDTH // LANES):
            xs = qkn[:, j * LANES:(j + 1) * LANES]
            nxt = pltpu.roll(xs, LANES - 1, 1)
            prv = pltpu.roll(xs, 1, 1)
            roped = (xs * c + jnp.where(even, nxt, prv) * s).astype(BF16)
            if j < ATTN_WIDTH // LANES:
                q_ref[sl, j * LANES:(j + 1) * LANES] = roped
            else:
                k_ref[sl, :] = roped
    for sl, h in zip(subs, hs):
        v_ref[:, sl] = h[:, QK_WIDTH:QK_WIDTH + KV_WIDTH].T.astype(BF16)
        for r in range(INPROJ_SUB // LANES):
            blk = h[r * LANES:(r + 1) * LANES, QK_WIDTH + KV_WIDTH:]
            u_ref[:, sl.start // LANES + r, :] = blk.T


INPROJ_TILE = 1024
INPROJ_SUB = 256


def _inproj(x2, w_in, gain, cos_t, sin_t, hmean, seq_len):
    n = x2.shape[0]
    tm = INPROJ_TILE
    nper = seq_len // tm
    return pl.pallas_call(
        _inproj_kernel,
        grid=(n // tm,),
        in_specs=[
            pl.BlockSpec((tm, D_MODEL), lambda i: (i, 0)),
            _const_spec((D_MODEL, AB_IN)),
            _const_spec((1, QK_WIDTH)),
            pl.BlockSpec((tm, LANES), lambda i: (i % nper, 0)),
            pl.BlockSpec((tm, LANES), lambda i: (i % nper, 0)),
            _const_spec((QK_WIDTH, QK_WIDTH)),
        ],
        out_specs=[
            pl.BlockSpec((tm, ATTN_WIDTH), lambda i: (i, 0)),
            pl.BlockSpec((tm, KV_WIDTH), lambda i: (i, 0)),
            pl.BlockSpec((KV_WIDTH, tm), lambda i: (0, i)),
            pl.BlockSpec((SSM_WIDTH, tm // LANES, LANES), lambda i: (0, i, 0)),
        ],
        out_shape=[
            jax.ShapeDtypeStruct((n, ATTN_WIDTH), BF16),
            jax.ShapeDtypeStruct((n, KV_WIDTH), BF16),
            jax.ShapeDtypeStruct((KV_WIDTH, n), BF16),
            jax.ShapeDtypeStruct((SSM_WIDTH, n // LANES, LANES), F32),
        ],
        compiler_params=_cparams(("parallel",)),
        name="inproj",
    )(x2, w_in, gain, cos_t, sin_t, hmean)


ATT_UNIT_Q = 64


ATT_SHIFT_LIMIT = 60.0


def _attn_kernel(bound_ref, q_ref, k_ref, vt_ref, o_ref,
                 s0_ref, s1_ref, p0_ref, p1_ref, m0_ref, m1_ref, l0_ref, l1_ref):
    seq_len = k_ref.shape[1]
    n_units = seq_len // ATT_UNIT_Q
    rep = N_HEADS // N_KV_HEADS
    lane = lax.broadcasted_iota(jnp.int32, (ATT_UNIT_Q, LANES), 1)
    lo_half = lane < HEAD_DIM
    zero = jnp.zeros((ATT_UNIT_Q, LANES), BF16)
    bound = bound_ref[0, 0]

    def rows_of(t):
        return pl.ds(pl.multiple_of(t * ATT_UNIT_Q, ATT_UNIT_Q), ATT_UNIT_Q)

    def raw_scores(t):
        q = q_ref[0, rows_of(t), :]
        slots = [q[:, m * LANES:(m + 1) * LANES] for m in range(rep)]
        qs = jnp.concatenate([jnp.where(lo_half, x, zero) for x in slots]
                             + [jnp.where(lo_half, zero, x) for x in slots], axis=0)
        return lax.dot_general(k_ref[0], qs, (((1,), (1,)), ((), ())), preferred_element_type=F32)

    def scores(t, s_ref, m_ref):
        s = raw_scores(t)
        s_ref[...] = s
        m_ref[...] = jnp.max(s, axis=0, keepdims=True)

    def probs(s_ref, m_ref, p_ref, l_ref):
        p = jnp.exp2(s_ref[...] - m_ref[...])
        l_ref[...] = jnp.sum(p, axis=0, keepdims=True)
        p_ref[...] = p.astype(BF16)

    def bounded_probs(t, p_ref, l_ref):
        p = jnp.exp2(raw_scores(t) - bound)
        l_ref[...] = jnp.sum(p, axis=0, keepdims=True)
        p_ref[...] = p.astype(BF16)

    def output(t, p_ref, l_ref):
        o_t = jnp.dot(vt_ref[...], p_ref[...], preferred_element_type=F32) / l_ref[...]
        o = o_t.T
        for m in range(rep):
            g0 = o[m * ATT_UNIT_Q:(m + 1) * ATT_UNIT_Q]
            g1 = o[(rep + m) * ATT_UNIT_Q:(rep + m + 1) * ATT_UNIT_Q]
            o_ref[0, rows_of(t), m * LANES:(m + 1) * LANES] = jnp.where(lo_half, g0, g1).astype(BF16)

    @pl.when(bound <= ATT_SHIFT_LIMIT)
    def _bounded_shift():
        bounded_probs(0, p0_ref, l0_ref)

        def body(i, carry):
            t = 2 * i + 1
            bounded_probs(t, p1_ref, l1_ref)
            output(t - 1, p0_ref, l0_ref)
            bounded_probs(t + 1, p0_ref, l0_ref)
            output(t, p1_ref, l1_ref)
            return carry

        lax.fori_loop(0, (n_units - 2) // 2, body, 0)
        bounded_probs(n_units - 1, p1_ref, l1_ref)
        output(n_units - 2, p0_ref, l0_ref)
        output(n_units - 1, p1_ref, l1_ref)

    @pl.when(bound > ATT_SHIFT_LIMIT)
    def _row_max_shift():
        scores(0, s0_ref, m0_ref)
        scores(1, s1_ref, m1_ref)
        probs(s0_ref, m0_ref, p0_ref, l0_ref)

        def body(i, carry):
            t = 2 * i + 2
            scores(t, s0_ref, m0_ref)
            probs(s1_ref, m1_ref, p1_ref, l1_ref)
            output(t - 2, p0_ref, l0_ref)
            scores(t + 1, s1_ref, m1_ref)
            probs(s0_ref, m0_ref, p0_ref, l0_ref)
            output(t - 1, p1_ref, l1_ref)
            return carry

        lax.fori_loop(0, (n_units - 2) // 2, body, 0)
        probs(s1_ref, m1_ref, p1_ref, l1_ref)
        output(n_units - 2, p0_ref, l0_ref)
        output(n_units - 1, p1_ref, l1_ref)


def _attention(bound, q, k, v_t):
    b, seq_len, _ = q.shape
    cols = ATT_UNIT_Q * N_HEADS
    seq = lambda w: pl.BlockSpec((1, seq_len, w), lambda i: (i, 0, 0))
    return pl.pallas_call(
        _attn_kernel,
        grid=(b,),
        in_specs=[pl.BlockSpec(memory_space=pltpu.SMEM), seq(ATTN_WIDTH), seq(KV_WIDTH),
                  pl.BlockSpec((KV_WIDTH, seq_len), lambda i: (0, i))],
        out_specs=seq(ATTN_WIDTH),
        out_shape=jax.ShapeDtypeStruct((b, seq_len, ATTN_WIDTH), BF16),
        scratch_shapes=[pltpu.VMEM((seq_len, cols), F32), pltpu.VMEM((seq_len, cols), F32),
                        pltpu.VMEM((seq_len, cols), BF16), pltpu.VMEM((seq_len, cols), BF16),
                        pltpu.VMEM((1, cols), F32), pltpu.VMEM((1, cols), F32),
                        pltpu.VMEM((1, cols), F32), pltpu.VMEM((1, cols), F32)],
        compiler_params=_cparams(("parallel",)),
        name="attention",
    )(bound, q, k, v_t)


SSM_PITCH = 132


def _s5_kernel(xp_ref, xs_ref, pc_ref, pr2_ref, bcr_ref, bci_ref, ccat_ref, cr4_ref, ci4_ref, dt_ref,
               e16_ref, x16_ref, yp_ref, ys_ref, s_ref, zt_ref, m_ref, yt_ref, *, pairs):
    ns = SSM_STATE
    pc = pc_ref[0]
    are_c, aim_c, dt_c = pc[:, 0:1], pc[:, 1:2], jnp.exp(pc[:, 2:3])
    row = lax.broadcasted_iota(jnp.int32, (2 * ns, LANES), 0)
    lan = lax.broadcasted_iota(jnp.int32, (2 * ns, LANES), 1)
    b_rows = row >= ns
    n_g = jnp.where(b_rows, lan - (SSM_CHUNK - 1), (SSM_CHUNK - 1) - lan)
    mask = n_g >= 0
    nf = jnp.where(mask, n_g, 0).astype(F32)
    rho = jnp.exp(are_c * dt_c * nf)
    ang = aim_c * dt_c * nf
    cs = jnp.where(mask, rho * jnp.cos(ang), 0.0)
    sn = jnp.where(mask, rho * jnp.sin(ang), 0.0)
    b_col = b_rows[:, 0:1]
    a1r = jnp.where(b_col, cs[:, SSM_CHUNK:SSM_CHUNK + 1], cs[:, SSM_CHUNK - 2:SSM_CHUNK - 1]) - 1.0
    a1i = jnp.where(b_col, sn[:, SSM_CHUNK:SSM_CHUNK + 1], sn[:, SSM_CHUNK - 2:SSM_CHUNK - 1])
    den = are_c * are_c + aim_c * aim_c
    cr = (a1r * are_c + a1i * aim_c) / den
    ci = (a1i * are_c - a1r * aim_c) / den

    def tile_lanes(x):
        return jnp.concatenate([x] * (SSM_TAPS // LANES), axis=1)

    bre, bim = bcr_ref[0], bci_ref[0]
    bbr = tile_lanes(cr * bre - ci * bim)
    bbi = tile_lanes(cr * bim + ci * bre)
    base = jnp.concatenate([cs[:ns], sn[:ns], cs[ns:], sn[ns:]], axis=0)
    ex = _select_cols(base, e16_ref[...])
    ecf, esf, ecb, esb = ex[:ns], ex[ns:2 * ns], ex[2 * ns:3 * ns], ex[3 * ns:]
    z_fre = ecf * bbr[:ns] - esf * bbi[:ns]
    z_fim = esf * bbr[:ns] + ecf * bbi[:ns]
    z_bre = ecb * bbr[ns:] - esb * bbi[ns:]
    z_bim = esb * bbr[ns:] + ecb * bbi[ns:]

    g_tab = jnp.concatenate([z_fre, -z_fim, z_bre, -z_bim], axis=0)
    ch, cl = _split_bf16(ccat_ref[0])
    gh, gl = _split_bf16(g_tab)
    w = (jnp.dot(ch, gh, preferred_element_type=F32) + jnp.dot(ch, gl, preferred_element_type=F32)
         + jnp.dot(cl, gh, preferred_element_type=F32))
    wrow = lax.broadcasted_iota(jnp.int32, (SSM_GROUP, SSM_TAPS), 0)
    wcol = lax.broadcasted_iota(jnp.int32, (SSM_GROUP, SSM_TAPS), 1)
    on_diag = (wcol // SSM_GROUP == SSM_CHUNK - 1) & (wcol % SSM_GROUP == wrow)
    w = w + jnp.where(on_diag, dt_ref[0], 0.0)
    per_vreg = LANES // SSM_GROUP
    rolled = [w] + [pltpu.roll(w, SSM_TAPS - SSM_GROUP * kk, 1) for kk in range(1, per_vreg)]
    for j in range(SSM_CHUNK):
        r = SSM_CHUNK - 1 - j
        blk = rolled[r % per_vreg][:, (r // per_vreg) * LANES:(r // per_vreg) * LANES + SSM_ROW]
        m_ref[j * SSM_GROUP:(j + 1) * SSM_GROUP, :] = blk.astype(BF16)

    p2 = jnp.concatenate([z_fre[:, :SSM_ROW], z_fim[:, :SSM_ROW], z_bre[:, SSM_ROW:], z_bim[:, SSM_ROW:]],
                         axis=0).astype(BF16)

    pr2 = pr2_ref[0]
    j_r = lax.broadcasted_iota(jnp.int32, (SSM_CHUNK, LANES), 0)
    l_r = lax.broadcasted_iota(jnp.int32, (SSM_CHUNK, LANES), 1)
    lo_r = l_r < ns
    n_q = jnp.where(lo_r, j_r + 1, (SSM_CHUNK - 1) - j_r).astype(F32)
    dt2 = jnp.exp(pr2[2:3, :])
    rho = jnp.exp(pr2[0:1, :] * dt2 * n_q)
    ang = pr2[1:2, :] * dt2 * n_q
    c2 = rho * jnp.cos(ang)
    s2 = rho * jnp.sin(ang)
    base_q = jnp.concatenate([jnp.where(lo_r, c2, -pltpu.roll(s2, ns, 1)),
                              jnp.where(lo_r, pltpu.roll(c2, ns, 1), -s2)], axis=1)
    a1x = _select_rows(x16_ref[...], base_q)
    sign = jnp.where(lax.broadcasted_iota(jnp.int32, (SSM_ROW, LANES), 1) < ns, 1.0, -1.0)
    a2x = jnp.concatenate([pltpu.roll(a1x[:, :LANES], ns, 1) * sign,
                           pltpu.roll(a1x[:, LANES:], ns, 1) * sign], axis=1)
    q2 = (a1x * jnp.concatenate([cr4_ref[0]] * SSM_CHUNK, axis=0)
          + a2x * jnp.concatenate([ci4_ref[0]] * SSM_CHUNK, axis=0)).astype(BF16)

    n_s = (SSM_CHUNK * jnp.left_shift(1, jnp.minimum(lan, 8))).astype(F32)
    rho = jnp.exp(are_c * dt_c * n_s)
    ang = aim_c * dt_c * n_s
    sc_r = rho * jnp.cos(ang)
    sc_i = rho * jnp.sin(ang)

    for seg, x_ref in enumerate((xp_ref, xs_ref)):
        for hi in range(SSM_GROUP):
            s_ref[seg, hi * SSM_PITCH:hi * SSM_PITCH + LANES, :] = x_ref[hi].T
    for par in range(2):
        for i in range(SSM_CHUNK):
            for seg in range(2):
                rows16 = s_ref[seg, pl.ds(par * SSM_CHUNK + i, SSM_GROUP, stride=SSM_PITCH), :]
                blk = 2 * par + seg
                zt_ref[i * SSM_GROUP:(i + 1) * SSM_GROUP, blk * LANES:(blk + 1) * LANES] = rows16.astype(BF16)

    zt = zt_ref[...]
    yt = jnp.dot(m_ref[...], zt, preferred_element_type=F32)
    st = jnp.dot(p2, zt, preferred_element_type=F32)

    def cmul(kk, lo, v):
        ar, ai = sc_r[lo:lo + ns, kk:kk + 1], sc_i[lo:lo + ns, kk:kk + 1]
        return jnp.concatenate([ar * v[:ns] - ai * v[ns:], ar * v[ns:] + ai * v[:ns]], axis=0)

    carried = [None] * 4
    for seg, npairs in enumerate(pairs):
        c2i = lax.broadcasted_iota(jnp.int32, (2 * ns, LANES), 1) % npairs
        right = lambda v, d: jnp.where(c2i >= d, pltpu.roll(v, d, 1), 0.0)
        left = lambda v, d: jnp.where(c2i < npairs - d, pltpu.roll(v, LANES - d, 1), 0.0)
        ev, od = st[:, seg * LANES:(seg + 1) * LANES], st[:, (2 + seg) * LANES:(3 + seg) * LANES]
        ef, of, eb, ob = ev[:2 * ns], od[:2 * ns], ev[2 * ns:], od[2 * ns:]
        ef, of = ef + cmul(0, 0, right(of, 1)), of + cmul(0, 0, ef)
        eb, ob = eb + cmul(0, ns, ob), ob + cmul(0, ns, left(eb, 1))
        for kk in range(1, int(math.log2(npairs)) + 1):
            d = 1 << (kk - 1)
            ef, of = ef + cmul(kk, 0, right(ef, d)), of + cmul(kk, 0, right(of, d))
            eb, ob = eb + cmul(kk, ns, left(eb, d)), ob + cmul(kk, ns, left(ob, d))
        carried[seg] = jnp.concatenate([right(of, 1), ob], axis=0)
        carried[2 + seg] = jnp.concatenate([ef, left(eb, 1)], axis=0)
    xin = jnp.concatenate(carried, axis=1).astype(BF16)
    yt = yt + jnp.dot(q2, xin, preferred_element_type=F32)

    for blk in range(4):
        yt_ref[blk] = yt[:, blk * LANES:(blk + 1) * LANES]
    for seg, y_ref in enumerate((yp_ref, ys_ref)):
        for ho in range(SSM_GROUP):
            ev = yt_ref[seg, pl.ds(ho, SSM_CHUNK, stride=SSM_GROUP), :]
            od = yt_ref[2 + seg, pl.ds(ho, SSM_CHUNK, stride=SSM_GROUP), :]
            y_ref[ho] = jnp.concatenate([ev, od], axis=0).T


def _s5(u_p, u_s, pc, pr2, bcr, bci, ccat, cr4, ci4, dtile, e16, x16, pairs):
    g = SSM_GROUPS
    rows = u_p.shape[1]
    assert u_s.shape[1] == rows == LANES
    grp = lambda shape: pl.BlockSpec((1,) + shape, lambda i: (i, 0, 0))
    chan = pl.BlockSpec((SSM_GROUP, rows, LANES), lambda i: (i, 0, 0))
    return pl.pallas_call(
        functools.partial(_s5_kernel, pairs=pairs),
        grid=(g,),
        in_specs=[chan, chan, grp((2 * SSM_STATE, 8)), grp((8, LANES)),
                  grp((2 * SSM_STATE, LANES)), grp((2 * SSM_STATE, LANES)),
                  grp((SSM_GROUP, SSM_SDIM)), grp((SSM_GROUP, SSM_SDIM)), grp((SSM_GROUP, SSM_SDIM)),
                  grp((1, SSM_TAPS)), _const_spec((LANES, SSM_TAPS)), _const_spec((SSM_ROW, SSM_CHUNK))],
        out_specs=[chan, chan],
        out_shape=[jax.ShapeDtypeStruct(u_p.shape, F32), jax.ShapeDtypeStruct(u_s.shape, F32)],
        scratch_shapes=[pltpu.VMEM((2, SSM_GROUP * SSM_PITCH, LANES), F32),
                        pltpu.VMEM((SSM_ROW, 4 * LANES), BF16),
                        pltpu.VMEM((SSM_ROW, SSM_ROW), BF16),
                        pltpu.VMEM((4, SSM_ROW, LANES), F32)],
        compiler_params=_cparams(("parallel",)),
        name="s5",
    )(u_p, u_s, pc, pr2, bcr, bci, ccat, cr4, ci4, dtile, e16, x16)


FF_CHUNK = 1024
MIXER_ROWS = 128


def _ffn_ln(x, w1_ref, w2_ref, g, b):
    half = x.shape[0] // 2
    xs = [x[:half], x[half:]]
    xbs = [v.astype(BF16) for v in xs]
    accs = [ALPHA * v for v in xs]
    for c in range(D_FF // FF_CHUNK):
        for i in range(2):
            h = jnp.dot(xbs[i], w1_ref[0, :, c * FF_CHUNK:(c + 1) * FF_CHUNK], preferred_element_type=F32)
            h = jnp.maximum(h, 0.0)
            accs[i] = accs[i] + jnp.dot((h * h).astype(BF16), w2_ref[0, c * FF_CHUNK:(c + 1) * FF_CHUNK, :],
                                        preferred_element_type=F32)
    return jnp.concatenate([_layer_norm(a, g, b) for a in accs], axis=0)


def _layer0_tail_kernel(x_ref, a_ref, y_ref, wg_ref, bg_ref, woa_ref, wos_ref, ln_ref, w1_ref, w2_ref, o_ref):
    rows_per_tile = x_ref.shape[0] // LANES
    row0 = (pl.program_id(0) % (y_ref.shape[1] // rows_per_tile)) * rows_per_tile
    ln = ln_ref[...]
    subs = [slice(r * LANES, (r + 1) * LANES) for r in range(rows_per_tile)]
    ys = [jax.nn.gelu(y_ref[:, row0 + r, :].T) for r in range(rows_per_tile)]
    gates = [jnp.dot(y.astype(BF16), wg_ref[...], preferred_element_type=F32) + bg_ref[...] for y in ys]
    zs = [(y * jax.nn.sigmoid(g)).astype(BF16) for y, g in zip(ys, gates)]
    mixes = [jnp.dot(a_ref[sl, :], woa_ref[...], preferred_element_type=F32)
             + jnp.dot(z, wos_ref[...], preferred_element_type=F32) for sl, z in zip(subs, zs)]
    x1 = jnp.concatenate([_layer_norm(ALPHA * x_ref[sl, :] + m, ln[0:1], ln[1:2]) for sl, m in zip(subs, mixes)],
                         axis=0)
    o_ref[...] = _ffn_ln(x1, w1_ref, w2_ref, ln[2:3], ln[3:4])


def _layer_weights_spec(shape, layer):
    return pl.BlockSpec((1,) + shape, lambda i: (layer, 0, 0), pipeline_mode=pl.Buffered(1))


def _layer0_tail(x2, attn, y, w_glu, b_glu, w_out_a, w_out_s, ln4, w1, w2):
    n = x2.shape[0]
    tm = TOKEN_TILE
    tok = lambda w: pl.BlockSpec((tm, w), lambda i: (i, 0))
    y_rows = max(8, tm // LANES)
    return pl.pallas_call(
        _layer0_tail_kernel,
        grid=(n // tm,),
        in_specs=[tok(D_MODEL), tok(ATTN_WIDTH),
                  pl.BlockSpec((SSM_WIDTH, y_rows, LANES), lambda i: (0, i // (y_rows * LANES // tm), 0)),
                  _const_spec((SSM_WIDTH, SSM_WIDTH)), _const_spec((1, SSM_WIDTH)),
                  _const_spec((ATTN_WIDTH, D_MODEL)), _const_spec((SSM_WIDTH, D_MODEL)),
                  _const_spec((4, D_MODEL)),
                  _layer_weights_spec((D_MODEL, D_FF), 0), _layer_weights_spec((D_FF, D_MODEL), 0)],
        out_specs=tok(D_MODEL),
        out_shape=jax.ShapeDtypeStruct((n, D_MODEL), F32),
        compiler_params=_cparams(("parallel",)),
        name="layer0_tail",
    )(x2, attn, y, w_glu, b_glu, w_out_a, w_out_s, ln4, w1, w2)


def _layer1_kernel(x_ref, win_ref, cln_ref, ws_ref, bs_ref, wout_ref, ln_ref, w1_ref, w2_ref, o_ref):
    cln = cln_ref[...]
    ln = ln_ref[...]
    halves = [x_ref[r0:r0 + MIXER_ROWS, :] for r0 in range(0, x_ref.shape[0], MIXER_ROWS)]
    hs = [jax.nn.gelu(jnp.dot(x.astype(BF16), win_ref[...], preferred_element_type=F32)) for x in halves]
    vbs = [_layer_norm(h[:, D_MODEL:], cln[0:1], cln[1:2]).astype(BF16) for h in hs]
    gated = []
    for h, vb in zip(hs, vbs):
        rows = []
        for c in range(MIXER_ROWS // SGU_CHUNK):
            cols = []
            for hd in range(SGU_HEADS):
                blk = vb[c * SGU_CHUNK:(c + 1) * SGU_CHUNK, hd * LANES:(hd + 1) * LANES]
                cols.append(jnp.dot(ws_ref[hd], blk, preferred_element_type=F32) + bs_ref[hd])
            rows.append(jnp.concatenate(cols, axis=1))
        gated.append((h[:, :D_MODEL] * jnp.concatenate(rows, axis=0)).astype(BF16))
    outs = [jnp.dot(g, wout_ref[...], preferred_element_type=F32) for g in gated]
    x1 = jnp.concatenate([_layer_norm(ALPHA * x + o, ln[0:1], ln[1:2]) for x, o in zip(halves, outs)], axis=0)
    o_ref[...] = _ffn_ln(x1, w1_ref, w2_ref, ln[2:3], ln[3:4])


def _layer1(x2, w_in, cln2, w_s, b_s_full, w_out, ln4, w1, w2):
    n = x2.shape[0]
    tm = TOKEN_TILE
    tok = pl.BlockSpec((tm, D_MODEL), lambda i: (i, 0))
    return pl.pallas_call(
        _layer1_kernel,
        grid=(n // tm,),
        in_specs=[tok, _const_spec((D_MODEL, 2 * D_MODEL)), _const_spec((2, D_MODEL)),
                  _const_spec((SGU_HEADS, SGU_CHUNK, SGU_CHUNK)), _const_spec((SGU_HEADS, SGU_CHUNK, LANES)),
                  _const_spec((D_MODEL, D_MODEL)), _const_spec((4, D_MODEL)),
                  _layer_weights_spec((D_MODEL, D_FF), 1), _layer_weights_spec((D_FF, D_MODEL), 1)],
        out_specs=tok,
        out_shape=jax.ShapeDtypeStruct((n, D_MODEL), F32),
        compiler_params=_cparams(("parallel",)),
        name="layer1",
    )(x2, w_in, cln2, w_s, b_s_full, w_out, ln4, w1, w2)


_HEAD_OF_SLOT = np.array([0, 4, 1, 5, 2, 6, 3, 7])


def _rope_tables(seq_len):
    t = np.arange(seq_len)
    axis_dim = HEAD_DIM // 2
    inv_freq = ROPE_THETA ** (-np.arange(0, axis_dim, 2, dtype=np.float64) / axis_dim)
    ang = np.concatenate([(t // GRID_W)[:, None] * inv_freq, (t % GRID_W)[:, None] * inv_freq], axis=-1)
    cos = np.repeat(np.cos(ang), 2, axis=-1)
    sin = np.repeat(np.sin(ang), 2, axis=-1) * np.tile([-1.0, 1.0], HEAD_DIM // 2)
    reps = LANES // HEAD_DIM
    return (jnp.asarray(np.tile(cos, (1, reps)), F32), jnp.asarray(np.tile(sin, (1, reps)), F32))


def _selectors():
    col = np.arange(SSM_TAPS)
    row = np.arange(LANES)
    e16 = (col[None, :] // SSM_GROUP == row[:, None])
    r2 = np.arange(SSM_ROW)
    x16 = (r2[:, None] // SSM_GROUP == np.arange(SSM_CHUNK)[None, :])
    as_bf16 = lambda a: jnp.asarray(a.astype(np.float32), BF16)
    return as_bf16(e16), as_bf16(x16)


def _s5_param_layout(a_re_f, a_im_f, ls_f, a_re_b, a_im_b, ls_b, b_re, b_im, c_re, c_im, d):
    g, ns = SSM_GROUPS, SSM_STATE
    zeros = jnp.zeros((g, ns), F32)
    per = lambda a_re, a_im, ls: jnp.stack([a_re, a_im, jnp.broadcast_to(ls[:, None], (g, ns))] + [zeros] * 5,
                                           axis=1)
    pr2 = jnp.concatenate([per(a_re_f, a_im_f, ls_f), per(a_re_b, a_im_b, ls_b)], axis=2)
    pc = jnp.transpose(pr2, (0, 2, 1))
    pad_lanes = lambda b: jnp.tile(jnp.concatenate([b, b], axis=1), (1, 1, LANES // SSM_GROUP))
    ccat = jnp.concatenate([c_re, c_im, c_re, c_im], axis=2)
    cr4 = jnp.concatenate([c_re] * 4, axis=2)
    ci4 = jnp.concatenate([c_im] * 4, axis=2)
    dtile = jnp.tile(d.reshape(g, 1, SSM_GROUP), (1, 1, SSM_LAGS))
    return pc, pr2, pad_lanes(b_re), pad_lanes(b_im), ccat, cr4, ci4, dtile


def kernel(x_prompt, x_sample, ab_w_in, ab_q_norm, ab_k_norm, ssm_a_re_f, ssm_a_im_f, ssm_log_step_f,
           ssm_a_re_b, ssm_a_im_b, ssm_log_step_b, ssm_b_re, ssm_b_im, ssm_c_re, ssm_c_im, ssm_d, ssm_w_glu,
           ssm_b_glu, ab_w_out, c_w_in, c_ln_g, c_ln_b, c_w_s, c_b_s, c_w_out, ln_mix_g, ln_mix_b, ff_w1, ff_w2,
           ln_ff_g, ln_ff_b):
    xs = [x_prompt, x_sample]
    row2 = lambda v: v.reshape(1, -1)

    slot_cols = (_HEAD_OF_SLOT[:, None] * HEAD_DIM + np.arange(HEAD_DIM)[None, :]).reshape(-1)
    w_in = ab_w_in[0]
    w_in = jnp.concatenate([w_in[:, slot_cols], w_in[:, ATTN_WIDTH:]], axis=1).astype(BF16)
    scale = HEAD_DIM ** -0.5 * math.log2(math.e)
    gain = jnp.concatenate([jnp.tile(ab_q_norm[0] * scale, N_HEADS), jnp.tile(ab_k_norm[0], N_KV_HEADS)])
    score_bound = (1.05 * HEAD_DIM * scale * jnp.max(jnp.abs(ab_q_norm[0])) * jnp.max(jnp.abs(ab_k_norm[0])))
    score_bound = score_bound.reshape(1, 1).astype(F32)
    hid = np.arange(QK_WIDTH) // HEAD_DIM
    hmean = jnp.asarray((hid[:, None] == hid[None, :]).astype(np.float32) / HEAD_DIM, BF16)
    w_out = ab_w_out[0]
    w_out_a = w_out[:ATTN_WIDTH][slot_cols].astype(BF16)
    w_out_s = w_out[ATTN_WIDTH:].astype(BF16)
    w_glu = ssm_w_glu[0].astype(BF16)

    attn, us, pairs = [], [], []
    cos_t, sin_t = _rope_tables(max(x.shape[1] for x in xs))
    for x in xs:
        b, seq_len, _ = x.shape
        q, k, v_t, u = _inproj(x.reshape(b * seq_len, D_MODEL), w_in, row2(gain), cos_t, sin_t, hmean, seq_len)
        attn.append(_attention(score_bound, q.reshape(b, seq_len, ATTN_WIDTH), k.reshape(b, seq_len, KV_WIDTH),
                               v_t).reshape(b * seq_len, ATTN_WIDTH))
        us.append(u)
        pairs.append(seq_len // (2 * SSM_CHUNK))
    ys = _s5(us[0], us[1], *_s5_param_layout(
        ssm_a_re_f[0], ssm_a_im_f[0], ssm_log_step_f[0], ssm_a_re_b[0], ssm_a_im_b[0], ssm_log_step_b[0],
        ssm_b_re[0], ssm_b_im[0], ssm_c_re[0], ssm_c_im[0], ssm_d[0]), *_selectors(), tuple(pairs))

    ff_w1b = ff_w1.astype(BF16)
    ff_w2b = ff_w2.astype(BF16)
    ln4 = [jnp.stack([ln_mix_g[i], ln_mix_b[i], ln_ff_g[i], ln_ff_b[i]]) for i in range(DEPTH)]
    cln2 = jnp.stack([c_ln_g[0], c_ln_b[0]])
    bs_full = jnp.broadcast_to(c_b_s[0][:, :, None], (SGU_HEADS, SGU_CHUNK, LANES))
    c_w_in_b, c_w_s_b, c_w_out_b = c_w_in[0].astype(BF16), c_w_s[0].astype(BF16), c_w_out[0].astype(BF16)
    outs = []
    for x, a, y in zip(xs, attn, ys):
        b, seq_len, _ = x.shape
        n = b * seq_len
        h = _layer0_tail(x.reshape(n, D_MODEL), a, y, w_glu, row2(ssm_b_glu[0]), w_out_a, w_out_s,
                         ln4[0], ff_w1b, ff_w2b)
        h = _layer1(h, c_w_in_b, cln2, c_w_s_b, bs_full, c_w_out_b, ln4[1], ff_w1b, ff_w2b)
        outs.append(h.reshape(b, seq_len, D_MODEL))
    return tuple(outs)
```

```python
import functools
import math

import numpy as np
import jax
import jax.numpy as jnp
from jax import lax
from jax.experimental import pallas as pl
from jax.experimental.pallas import tpu as pltpu

F32 = jnp.float32
BF16 = jnp.bfloat16

D_MODEL = 1024
GRID_W = 64
N_HEADS = 8
N_KV_HEADS = 2
HEAD_DIM = 64
ATTN_WIDTH = N_HEADS * HEAD_DIM
KV_WIDTH = N_KV_HEADS * HEAD_DIM
QK_WIDTH = ATTN_WIDTH + KV_WIDTH
ROPE_THETA = 10000.0
SSM_WIDTH = D_MODEL - ATTN_WIDTH
SSM_GROUP = 16
SSM_GROUPS = SSM_WIDTH // SSM_GROUP
SSM_STATE = 64
SGU_HEADS = 8
SGU_CHUNK = 128
D_FF = 4 * D_MODEL
DEPTH = 2
AB_IN = ATTN_WIDTH + 2 * KV_WIDTH + SSM_WIDTH
ALPHA = (2 * DEPTH) ** 0.25
EPS = 1e-6

LANES = 128
VMEM_LIMIT_BYTES = 56 * 1024 * 1024

SSM_CHUNK = 64
SSM_ROW = SSM_CHUNK * SSM_GROUP
SSM_LAGS = 2 * SSM_CHUNK
SSM_TAPS = SSM_LAGS * SSM_GROUP
SSM_SDIM = 4 * SSM_STATE

LAYER0_TAIL_TILE = 1024
LAYER1_TILE = 512


def _cparams(sem):
    return pltpu.CompilerParams(dimension_semantics=sem, vmem_limit_bytes=VMEM_LIMIT_BYTES)


def _const_spec(shape):
    nd = len(shape)
    return pl.BlockSpec(shape, lambda *_: (0,) * nd, pipeline_mode=pl.Buffered(1))


def _layer_norm(r, g, b):
    mu = jnp.mean(r, axis=-1, keepdims=True)
    d = r - mu
    var = jnp.mean(d * d, axis=-1, keepdims=True)
    return d * lax.rsqrt(var + EPS) * g + b


def _split_bf16(x):
    hi = x.astype(BF16)
    lo = (x - hi.astype(F32)).astype(BF16)
    return hi, lo


def _select_cols(x, e):
    hi, lo = _split_bf16(x)
    y = jnp.dot(jnp.concatenate([hi, lo], axis=0), e, preferred_element_type=F32)
    return y[:x.shape[0]] + y[x.shape[0]:]


def _select_rows(e, x):
    hi, lo = _split_bf16(x)
    y = jnp.dot(e, jnp.concatenate([hi, lo], axis=1), preferred_element_type=F32)
    return y[:, :x.shape[1]] + y[:, x.shape[1]:]


def _inproj_kernel(x_ref, w_ref, gain_ref, cos_ref, sin_ref, hmean_ref, q_ref, k_ref, v_ref, u_ref):
    subs = [slice(r0, r0 + INPROJ_SUB) for r0 in range(0, x_ref.shape[0], INPROJ_SUB)]
    hs = [jnp.dot(x_ref[sl, :].astype(BF16), w_ref[...], preferred_element_type=F32) for sl in subs]
    mss = []
    for h in hs:
        sq = (h[:, :QK_WIDTH] * h[:, :QK_WIDTH]).astype(BF16)
        mss.append(jnp.concatenate(
            [jnp.dot(sq[:, j * LANES:(j + 1) * LANES], hmean_ref[...], preferred_element_type=F32)
             for j in range(QK_WIDTH // LANES)], axis=1))
    qkns = [h[:, :QK_WIDTH] * lax.rsqrt(ms + EPS) * gain_ref[...] for h, ms in zip(hs, mss)]
    even = (lax.broadcasted_iota(jnp.int32, (INPROJ_SUB, LANES), 1) % 2) == 0
    for sl, qkn in zip(subs, qkns):
        c = cos_ref[sl, :]
        s = sin_ref[sl, :]
        for j in range(QK_WIDTH // LANES):
            xs = qkn[:, j * LANES:(j + 1) * LANES]
            nxt = pltpu.roll(xs, LANES - 1, 1)
            prv = pltpu.roll(xs, 1, 1)
            roped = (xs * c + jnp.where(even, nxt, prv) * s).astype(BF16)
            if j < ATTN_WIDTH // LANES:
                q_ref[sl, j * LANES:(j + 1) * LANES] = roped
            else:
                k_ref[sl, :] = roped
    for sl, h in zip(subs, hs):
        v_ref[:, sl] = h[:, QK_WIDTH:QK_WIDTH + KV_WIDTH].T.astype(BF16)
        for r in range(INPROJ_SUB // LANES):
            blk = h[r * LANES:(r + 1) * LANES, QK_WIDTH + KV_WIDTH:]
            u_ref[:, sl.start // LANES + r, :] = blk.T


INPROJ_TILE = 1024
INPROJ_SUB = 256


def _inproj(x2, w_in, gain, cos_t, sin_t, hmean, seq_len):
    n = x2.shape[0]
    tm = INPROJ_TILE
    nper = seq_len // tm
    return pl.pallas_call(
        _inproj_kernel,
        grid=(n // tm,),
        in_specs=[
            pl.BlockSpec((tm, D_MODEL), lambda i: (i, 0)),
            _const_spec((D_MODEL, AB_IN)),
            _const_spec((1, QK_WIDTH)),
            pl.BlockSpec((tm, LANES), lambda i: (i % nper, 0)),
            pl.BlockSpec((tm, LANES), lambda i: (i % nper, 0)),
            _const_spec((LANES, LANES)),
        ],
        out_specs=[
            pl.BlockSpec((tm, ATTN_WIDTH), lambda i: (i, 0)),
            pl.BlockSpec((tm, KV_WIDTH), lambda i: (i, 0)),
            pl.BlockSpec((KV_WIDTH, tm), lambda i: (0, i)),
            pl.BlockSpec((SSM_WIDTH, tm // LANES, LANES), lambda i: (0, i, 0)),
        ],
        out_shape=[
            jax.ShapeDtypeStruct((n, ATTN_WIDTH), BF16),
            jax.ShapeDtypeStruct((n, KV_WIDTH), BF16),
            jax.ShapeDtypeStruct((KV_WIDTH, n), BF16),
            jax.ShapeDtypeStruct((SSM_WIDTH, n // LANES, LANES), F32),
        ],
        compiler_params=_cparams(("parallel",)),
        name="inproj",
    )(x2, w_in, gain, cos_t, sin_t, hmean)


ATT_UNIT_Q = 64


ATT_SHIFT_LIMIT = 60.0


def _attn_kernel(bound_ref, q_ref, k_ref, vt_ref, o_ref,
                 s0_ref, s1_ref, p0_ref, p1_ref, m0_ref, m1_ref, l0_ref, l1_ref):
    seq_len = k_ref.shape[1]
    n_units = seq_len // ATT_UNIT_Q
    rep = N_HEADS // N_KV_HEADS
    lane = lax.broadcasted_iota(jnp.int32, (ATT_UNIT_Q, LANES), 1)
    lo_half = lane < HEAD_DIM
    zero = jnp.zeros((ATT_UNIT_Q, LANES), BF16)
    bound = bound_ref[0, 0]

    def rows_of(t):
        return pl.ds(pl.multiple_of(t * ATT_UNIT_Q, ATT_UNIT_Q), ATT_UNIT_Q)

    def raw_scores(t):
        q = q_ref[0, rows_of(t), :]
        slots = [q[:, m * LANES:(m + 1) * LANES] for m in range(rep)]
        qs = jnp.concatenate([jnp.where(lo_half, x, zero) for x in slots]
                             + [jnp.where(lo_half, zero, x) for x in slots], axis=0)
        return lax.dot_general(k_ref[0], qs, (((1,), (1,)), ((), ())), preferred_element_type=F32)

    def scores(t, s_ref, m_ref):
        s = raw_scores(t)
        s_ref[...] = s
        m_ref[...] = jnp.max(s, axis=0, keepdims=True)

    def probs(s_ref, m_ref, p_ref, l_ref):
        p = jnp.exp2(s_ref[...] - m_ref[...])
        l_ref[...] = jnp.sum(p, axis=0, keepdims=True)
        p_ref[...] = p.astype(BF16)

    def bounded_probs(t, p_ref, l_ref):
        p = jnp.exp2(raw_scores(t) - bound)
        l_ref[...] = jnp.sum(p, axis=0, keepdims=True)
        p_ref[...] = p.astype(BF16)

    def output(t, p_ref, l_ref):
        o_t = jnp.dot(vt_ref[...], p_ref[...], preferred_element_type=F32) / l_ref[...]
        o = o_t.T
        for m in range(rep):
            g0 = o[m * ATT_UNIT_Q:(m + 1) * ATT_UNIT_Q]
            g1 = o[(rep + m) * ATT_UNIT_Q:(rep + m + 1) * ATT_UNIT_Q]
            o_ref[0, rows_of(t), m * LANES:(m + 1) * LANES] = jnp.where(lo_half, g0, g1).astype(BF16)

    @pl.when(bound <= ATT_SHIFT_LIMIT)
    def _bounded_shift():
        bounded_probs(0, p0_ref, l0_ref)

        def body(i, carry):
            t = 2 * i + 1
            bounded_probs(t, p1_ref, l1_ref)
            output(t - 1, p0_ref, l0_ref)
            bounded_probs(t + 1, p0_ref, l0_ref)
            output(t, p1_ref, l1_ref)
            return carry

        lax.fori_loop(0, (n_units - 2) // 2, body, 0)
        bounded_probs(n_units - 1, p1_ref, l1_ref)
        output(n_units - 2, p0_ref, l0_ref)
        output(n_units - 1, p1_ref, l1_ref)

    @pl.when(bound > ATT_SHIFT_LIMIT)
    def _row_max_shift():
        scores(0, s0_ref, m0_ref)
        scores(1, s1_ref, m1_ref)
        probs(s0_ref, m0_ref, p0_ref, l0_ref)

        def body(i, carry):
            t = 2 * i + 2
            scores(t, s0_ref, m0_ref)
            probs(s1_ref, m1_ref, p1_ref, l1_ref)
            output(t - 2, p0_ref, l0_ref)
            scores(t + 1, s1_ref, m1_ref)
            probs(s0_ref, m0_ref, p0_ref, l0_ref)
            output(t - 1, p1_ref, l1_ref)
            return carry

        lax.fori_loop(0, (n_units - 2) // 2, body, 0)
        probs(s1_ref, m1_ref, p1_ref, l1_ref)
        output(n_units - 2, p0_ref, l0_ref)
        output(n_units - 1, p1_ref, l1_ref)


def _attention(bound, q, k, v_t):
    b, seq_len, _ = q.shape
    cols = ATT_UNIT_Q * N_HEADS
    seq = lambda w: pl.BlockSpec((1, seq_len, w), lambda i: (i, 0, 0))
    return pl.pallas_call(
        _attn_kernel,
        grid=(b,),
        in_specs=[pl.BlockSpec(memory_space=pltpu.SMEM), seq(ATTN_WIDTH), seq(KV_WIDTH),
                  pl.BlockSpec((KV_WIDTH, seq_len), lambda i: (0, i))],
        out_specs=seq(ATTN_WIDTH),
        out_shape=jax.ShapeDtypeStruct((b, seq_len, ATTN_WIDTH), BF16),
        scratch_shapes=[pltpu.VMEM((seq_len, cols), F32), pltpu.VMEM((seq_len, cols), F32),
                        pltpu.VMEM((seq_len, cols), BF16), pltpu.VMEM((seq_len, cols), BF16),
                        pltpu.VMEM((1, cols), F32), pltpu.VMEM((1, cols), F32),
                        pltpu.VMEM((1, cols), F32), pltpu.VMEM((1, cols), F32)],
        compiler_params=_cparams(("parallel",)),
        name="attention",
    )(bound, q, k, v_t)


SSM_PITCH = 132


def _s5_kernel(xp_ref, xs_ref, pc_ref, pr2_ref, bcr_ref, bci_ref, ccat_ref, cr4_ref, ci4_ref, dt_ref,
               e16_ref, x16_ref, yp_ref, ys_ref, s_ref, zt_ref, m_ref, yt_ref, *, pairs):
    ns = SSM_STATE
    pc = pc_ref[0]
    are_c, aim_c, dt_c = pc[:, 0:1], pc[:, 1:2], jnp.exp(pc[:, 2:3])
    row = lax.broadcasted_iota(jnp.int32, (2 * ns, LANES), 0)
    lan = lax.broadcasted_iota(jnp.int32, (2 * ns, LANES), 1)
    b_rows = row >= ns
    n_g = jnp.where(b_rows, lan - (SSM_CHUNK - 1), (SSM_CHUNK - 1) - lan)
    mask = n_g >= 0
    nf = jnp.where(mask, n_g, 0).astype(F32)
    rho = jnp.exp(are_c * dt_c * nf)
    ang = aim_c * dt_c * nf
    cs = jnp.where(mask, rho * jnp.cos(ang), 0.0)
    sn = jnp.where(mask, rho * jnp.sin(ang), 0.0)
    b_col = b_rows[:, 0:1]
    a1r = jnp.where(b_col, cs[:, SSM_CHUNK:SSM_CHUNK + 1], cs[:, SSM_CHUNK - 2:SSM_CHUNK - 1]) - 1.0
    a1i = jnp.where(b_col, sn[:, SSM_CHUNK:SSM_CHUNK + 1], sn[:, SSM_CHUNK - 2:SSM_CHUNK - 1])
    den = are_c * are_c + aim_c * aim_c
    cr = (a1r * are_c + a1i * aim_c) / den
    ci = (a1i * are_c - a1r * aim_c) / den

    def tile_lanes(x):
        return jnp.concatenate([x] * (SSM_TAPS // LANES), axis=1)

    bre, bim = bcr_ref[0], bci_ref[0]
    bbr = tile_lanes(cr * bre - ci * bim)
    bbi = tile_lanes(cr * bim + ci * bre)
    base = jnp.concatenate([cs[:ns], sn[:ns], cs[ns:], sn[ns:]], axis=0)
    ex = _select_cols(base, e16_ref[...])
    ecf, esf, ecb, esb = ex[:ns], ex[ns:2 * ns], ex[2 * ns:3 * ns], ex[3 * ns:]
    z_fre = ecf * bbr[:ns] - esf * bbi[:ns]
    z_fim = esf * bbr[:ns] + ecf * bbi[:ns]
    z_bre = ecb * bbr[ns:] - esb * bbi[ns:]
    z_bim = esb * bbr[ns:] + ecb * bbi[ns:]

    g_tab = jnp.concatenate([z_fre, -z_fim, z_bre, -z_bim], axis=0)
    ch, cl = _split_bf16(ccat_ref[0])
    gh, gl = _split_bf16(g_tab)
    w = (jnp.dot(ch, gh, preferred_element_type=F32) + jnp.dot(ch, gl, preferred_element_type=F32)
         + jnp.dot(cl, gh, preferred_element_type=F32))
    wrow = lax.broadcasted_iota(jnp.int32, (SSM_GROUP, SSM_TAPS), 0)
    wcol = lax.broadcasted_iota(jnp.int32, (SSM_GROUP, SSM_TAPS), 1)
    on_diag = (wcol // SSM_GROUP == SSM_CHUNK - 1) & (wcol % SSM_GROUP == wrow)
    w = w + jnp.where(on_diag, dt_ref[0], 0.0)
    per_vreg = LANES // SSM_GROUP
    rolled = [w] + [pltpu.roll(w, SSM_TAPS - SSM_GROUP * kk, 1) for kk in range(1, per_vreg)]
    for j in range(SSM_CHUNK):
        r = SSM_CHUNK - 1 - j
        blk = rolled[r % per_vreg][:, (r // per_vreg) * LANES:(r // per_vreg) * LANES + SSM_ROW]
        m_ref[j * SSM_GROUP:(j + 1) * SSM_GROUP, :] = blk.astype(BF16)

    p2 = jnp.concatenate([z_fre[:, :SSM_ROW], z_fim[:, :SSM_ROW], z_bre[:, SSM_ROW:], z_bim[:, SSM_ROW:]],
                         axis=0).astype(BF16)

    pr2 = pr2_ref[0]
    j_r = lax.broadcasted_iota(jnp.int32, (SSM_CHUNK, LANES), 0)
    l_r = lax.broadcasted_iota(jnp.int32, (SSM_CHUNK, LANES), 1)
    lo_r = l_r < ns
    n_q = jnp.where(lo_r, j_r + 1, (SSM_CHUNK - 1) - j_r).astype(F32)
    dt2 = jnp.exp(pr2[2:3, :])
    rho = jnp.exp(pr2[0:1, :] * dt2 * n_q)
    ang = pr2[1:2, :] * dt2 * n_q
    c2 = rho * jnp.cos(ang)
    s2 = rho * jnp.sin(ang)
    base_q = jnp.concatenate([jnp.where(lo_r, c2, -pltpu.roll(s2, ns, 1)),
                              jnp.where(lo_r, pltpu.roll(c2, ns, 1), -s2)], axis=1)
    a1x = _select_rows(x16_ref[...], base_q)
    sign = jnp.where(lax.broadcasted_iota(jnp.int32, (SSM_ROW, LANES), 1) < ns, 1.0, -1.0)
    a2x = jnp.concatenate([pltpu.roll(a1x[:, :LANES], ns, 1) * sign,
                           pltpu.roll(a1x[:, LANES:], ns, 1) * sign], axis=1)
    q2 = (a1x * jnp.concatenate([cr4_ref[0]] * SSM_CHUNK, axis=0)
          + a2x * jnp.concatenate([ci4_ref[0]] * SSM_CHUNK, axis=0)).astype(BF16)

    n_s = (SSM_CHUNK * jnp.left_shift(1, jnp.minimum(lan, 8))).astype(F32)
    rho = jnp.exp(are_c * dt_c * n_s)
    ang = aim_c * dt_c * n_s
    sc_r = rho * jnp.cos(ang)
    sc_i = rho * jnp.sin(ang)

    for seg, x_ref in enumerate((xp_ref, xs_ref)):
        for hi in range(SSM_GROUP):
            s_ref[seg, hi * SSM_PITCH:hi * SSM_PITCH + LANES, :] = x_ref[hi].T
    for par in range(2):
        for i in range(SSM_CHUNK):
            for seg in range(2):
                rows16 = s_ref[seg, pl.ds(par * SSM_CHUNK + i, SSM_GROUP, stride=SSM_PITCH), :]
                blk = 2 * par + seg
                zt_ref[i * SSM_GROUP:(i + 1) * SSM_GROUP, blk * LANES:(blk + 1) * LANES] = rows16.astype(BF16)

    zt = zt_ref[...]
    yt = jnp.dot(m_ref[...], zt, preferred_element_type=F32)
    st = jnp.dot(p2, zt, preferred_element_type=F32)

    def cmul(kk, lo, v):
        ar, ai = sc_r[lo:lo + ns, kk:kk + 1], sc_i[lo:lo + ns, kk:kk + 1]
        return jnp.concatenate([ar * v[:ns] - ai * v[ns:], ar * v[ns:] + ai * v[:ns]], axis=0)

    carried = [None] * 4
    for seg, npairs in enumerate(pairs):
        c2i = lax.broadcasted_iota(jnp.int32, (2 * ns, LANES), 1) % npairs
        right = lambda v, d: jnp.where(c2i >= d, pltpu.roll(v, d, 1), 0.0)
        left = lambda v, d: jnp.where(c2i < npairs - d, pltpu.roll(v, LANES - d, 1), 0.0)
        ev, od = st[:, seg * LANES:(seg + 1) * LANES], st[:, (2 + seg) * LANES:(3 + seg) * LANES]
        ef, of, eb, ob = ev[:2 * ns], od[:2 * ns], ev[2 * ns:], od[2 * ns:]
        ef, of = ef + cmul(0, 0, right(of, 1)), of + cmul(0, 0, ef)
        eb, ob = eb + cmul(0, ns, ob), ob + cmul(0, ns, left(eb, 1))
        for kk in range(1, int(math.log2(npairs)) + 1):
            d = 1 << (kk - 1)
            ef, of = ef + cmul(kk, 0, right(ef, d)), of + cmul(kk, 0, right(of, d))
            eb, ob = eb + cmul(kk, ns, left(eb, d)), ob + cmul(kk, ns, left(ob, d))
        carried[seg] = jnp.concatenate([right(of, 1), ob], axis=0)
        carried[2 + seg] = jnp.concatenate([ef, left(eb, 1)], axis=0)
    xin = jnp.concatenate(carried, axis=1).astype(BF16)
    yt = yt + jnp.dot(q2, xin, preferred_element_type=F32)

    for blk in range(4):
        yt_ref[blk] = yt[:, blk * LANES:(blk + 1) * LANES]
    for seg, y_ref in enumerate((yp_ref, ys_ref)):
        for ho in range(SSM_GROUP):
            ev = yt_ref[seg, pl.ds(ho, SSM_CHUNK, stride=SSM_GROUP), :]
            od = yt_ref[2 + seg, pl.ds(ho, SSM_CHUNK, stride=SSM_GROUP), :]
            y_ref[ho] = jnp.concatenate([ev, od], axis=0).T


def _s5(u_p, u_s, pc, pr2, bcr, bci, ccat, cr4, ci4, dtile, e16, x16, pairs):
    g = SSM_GROUPS
    rows = u_p.shape[1]
    assert u_s.shape[1] == rows == LANES
    grp = lambda shape: pl.BlockSpec((1,) + shape, lambda i: (i, 0, 0))
    chan = pl.BlockSpec((SSM_GROUP, rows, LANES), lambda i: (i, 0, 0))
    return pl.pallas_call(
        functools.partial(_s5_kernel, pairs=pairs),
        grid=(g,),
        in_specs=[chan, chan, grp((2 * SSM_STATE, 8)), grp((8, LANES)),
                  grp((2 * SSM_STATE, LANES)), grp((2 * SSM_STATE, LANES)),
                  grp((SSM_GROUP, SSM_SDIM)), grp((SSM_GROUP, SSM_SDIM)), grp((SSM_GROUP, SSM_SDIM)),
                  grp((1, SSM_TAPS)), _const_spec((LANES, SSM_TAPS)), _const_spec((SSM_ROW, SSM_CHUNK))],
        out_specs=[chan, chan],
        out_shape=[jax.ShapeDtypeStruct(u_p.shape, F32), jax.ShapeDtypeStruct(u_s.shape, F32)],
        scratch_shapes=[pltpu.VMEM((2, SSM_GROUP * SSM_PITCH, LANES), F32),
                        pltpu.VMEM((SSM_ROW, 4 * LANES), BF16),
                        pltpu.VMEM((SSM_ROW, SSM_ROW), BF16),
                        pltpu.VMEM((4, SSM_ROW, LANES), F32)],
        compiler_params=_cparams(("parallel",)),
        name="s5",
    )(u_p, u_s, pc, pr2, bcr, bci, ccat, cr4, ci4, dtile, e16, x16)


FF_CHUNK = 1024
FF_ROWS = 256
MIXER_ROWS = 128


def _ffn_ln(x, w1_ref, w2_ref, g, b):
    xs = [x[r0:r0 + FF_ROWS] for r0 in range(0, x.shape[0], FF_ROWS)]
    xbs = [v.astype(BF16) for v in xs]
    accs = [ALPHA * v for v in xs]
    for c in range(D_FF // FF_CHUNK):
        for i in range(len(xs)):
            h = jnp.dot(xbs[i], w1_ref[0, :, c * FF_CHUNK:(c + 1) * FF_CHUNK], preferred_element_type=F32)
            h = jnp.maximum(h, 0.0)
            accs[i] = accs[i] + jnp.dot((h * h).astype(BF16), w2_ref[0, c * FF_CHUNK:(c + 1) * FF_CHUNK, :],
                                        preferred_element_type=F32)
    return jnp.concatenate([_layer_norm(a, g, b) for a in accs], axis=0)


def _layer0_tail_kernel(x_ref, a_ref, y_ref, wg_ref, bg_ref, woa_ref, wos_ref, ln_ref, w1_ref, w2_ref, o_ref):
    rows_per_tile = x_ref.shape[0] // LANES
    row0 = (pl.program_id(0) % (y_ref.shape[1] // rows_per_tile)) * rows_per_tile
    ln = ln_ref[...]
    subs = [slice(r * LANES, (r + 1) * LANES) for r in range(rows_per_tile)]
    ys = [jax.nn.gelu(y_ref[:, row0 + r, :].T) for r in range(rows_per_tile)]
    gates = [jnp.dot(y.astype(BF16), wg_ref[...], preferred_element_type=F32) + bg_ref[...] for y in ys]
    zs = [(y * jax.nn.sigmoid(g)).astype(BF16) for y, g in zip(ys, gates)]
    mixes = [jnp.dot(a_ref[sl, :], woa_ref[...], preferred_element_type=F32)
             + jnp.dot(z, wos_ref[...], preferred_element_type=F32) for sl, z in zip(subs, zs)]
    x1 = jnp.concatenate([_layer_norm(ALPHA * x_ref[sl, :] + m, ln[0:1], ln[1:2]) for sl, m in zip(subs, mixes)],
                         axis=0)
    o_ref[...] = _ffn_ln(x1, w1_ref, w2_ref, ln[2:3], ln[3:4])


def _layer_weights_spec(shape, layer):
    return pl.BlockSpec((1,) + shape, lambda i: (layer, 0, 0), pipeline_mode=pl.Buffered(1))


def _layer0_tail(x2, attn, y, w_glu, b_glu, w_out_a, w_out_s, ln4, w1, w2):
    n = x2.shape[0]
    tm = LAYER0_TAIL_TILE
    tok = lambda w: pl.BlockSpec((tm, w), lambda i: (i, 0))
    y_rows = max(8, tm // LANES)
    return pl.pallas_call(
        _layer0_tail_kernel,
        grid=(n // tm,),
        in_specs=[tok(D_MODEL), tok(ATTN_WIDTH),
                  pl.BlockSpec((SSM_WIDTH, y_rows, LANES), lambda i: (0, i // (y_rows * LANES // tm), 0)),
                  _const_spec((SSM_WIDTH, SSM_WIDTH)), _const_spec((1, SSM_WIDTH)),
                  _const_spec((ATTN_WIDTH, D_MODEL)), _const_spec((SSM_WIDTH, D_MODEL)),
                  _const_spec((4, D_MODEL)),
                  _layer_weights_spec((D_MODEL, D_FF), 0), _layer_weights_spec((D_FF, D_MODEL), 0)],
        out_specs=tok(D_MODEL),
        out_shape=jax.ShapeDtypeStruct((n, D_MODEL), F32),
        compiler_params=_cparams(("parallel",)),
        name="layer0_tail",
    )(x2, attn, y, w_glu, b_glu, w_out_a, w_out_s, ln4, w1, w2)


def _layer1_kernel(x_ref, win_ref, cln_ref, ws_ref, bs_ref, wout_ref, ln_ref, w1_ref, w2_ref, o_ref):
    cln = cln_ref[...]
    ln = ln_ref[...]
    halves = [x_ref[r0:r0 + MIXER_ROWS, :] for r0 in range(0, x_ref.shape[0], MIXER_ROWS)]
    hs = [jax.nn.gelu(jnp.dot(x.astype(BF16), win_ref[...], preferred_element_type=F32)) for x in halves]
    vbs = [_layer_norm(h[:, D_MODEL:], cln[0:1], cln[1:2]).astype(BF16) for h in hs]
    gated = []
    for h, vb in zip(hs, vbs):
        rows = []
        for c in range(MIXER_ROWS // SGU_CHUNK):
            cols = []
            for hd in range(SGU_HEADS):
                blk = vb[c * SGU_CHUNK:(c + 1) * SGU_CHUNK, hd * LANES:(hd + 1) * LANES]
                cols.append(jnp.dot(ws_ref[hd], blk, preferred_element_type=F32) + bs_ref[hd])
            rows.append(jnp.concatenate(cols, axis=1))
        gated.append((h[:, :D_MODEL] * jnp.concatenate(rows, axis=0)).astype(BF16))
    outs = [jnp.dot(g, wout_ref[...], preferred_element_type=F32) for g in gated]
    x1 = jnp.concatenate([_layer_norm(ALPHA * x + o, ln[0:1], ln[1:2]) for x, o in zip(halves, outs)], axis=0)
    o_ref[...] = _ffn_ln(x1, w1_ref, w2_ref, ln[2:3], ln[3:4])


def _layer1(x2, w_in, cln2, w_s, b_s_full, w_out, ln4, w1, w2):
    n = x2.shape[0]
    tm = LAYER1_TILE
    tok = pl.BlockSpec((tm, D_MODEL), lambda i: (i, 0))
    return pl.pallas_call(
        _layer1_kernel,
        grid=(n // tm,),
        in_specs=[tok, _const_spec((D_MODEL, 2 * D_MODEL)), _const_spec((2, D_MODEL)),
                  _const_spec((SGU_HEADS, SGU_CHUNK, SGU_CHUNK)), _const_spec((SGU_HEADS, SGU_CHUNK, LANES)),
                  _const_spec((D_MODEL, D_MODEL)), _const_spec((4, D_MODEL)),
                  _layer_weights_spec((D_MODEL, D_FF), 1), _layer_weights_spec((D_FF, D_MODEL), 1)],
        out_specs=tok,
        out_shape=jax.ShapeDtypeStruct((n, D_MODEL), F32),
        compiler_params=_cparams(("parallel",)),
        name="layer1",
    )(x2, w_in, cln2, w_s, b_s_full, w_out, ln4, w1, w2)


_HEAD_OF_SLOT = np.array([0, 4, 1, 5, 2, 6, 3, 7])


def _rope_tables(seq_len):
    t = np.arange(seq_len)
    axis_dim = HEAD_DIM // 2
    inv_freq = ROPE_THETA ** (-np.arange(0, axis_dim, 2, dtype=np.float64) / axis_dim)
    ang = np.concatenate([(t // GRID_W)[:, None] * inv_freq, (t % GRID_W)[:, None] * inv_freq], axis=-1)
    cos = np.repeat(np.cos(ang), 2, axis=-1)
    sin = np.repeat(np.sin(ang), 2, axis=-1) * np.tile([-1.0, 1.0], HEAD_DIM // 2)
    reps = LANES // HEAD_DIM
    return (jnp.asarray(np.tile(cos, (1, reps)), F32), jnp.asarray(np.tile(sin, (1, reps)), F32))


def _selectors():
    col = np.arange(SSM_TAPS)
    row = np.arange(LANES)
    e16 = (col[None, :] // SSM_GROUP == row[:, None])
    r2 = np.arange(SSM_ROW)
    x16 = (r2[:, None] // SSM_GROUP == np.arange(SSM_CHUNK)[None, :])
    as_bf16 = lambda a: jnp.asarray(a.astype(np.float32), BF16)
    return as_bf16(e16), as_bf16(x16)


def _s5_param_layout(a_re_f, a_im_f, ls_f, a_re_b, a_im_b, ls_b, b_re, b_im, c_re, c_im, d):
    g, ns = SSM_GROUPS, SSM_STATE
    zeros = jnp.zeros((g, ns), F32)
    per = lambda a_re, a_im, ls: jnp.stack([a_re, a_im, jnp.broadcast_to(ls[:, None], (g, ns))] + [zeros] * 5,
                                           axis=1)
    pr2 = jnp.concatenate([per(a_re_f, a_im_f, ls_f), per(a_re_b, a_im_b, ls_b)], axis=2)
    pc = jnp.transpose(pr2, (0, 2, 1))
    tile_b = lambda b: jnp.tile(jnp.concatenate([b, b], axis=1), (1, 1, LANES // SSM_GROUP))
    ccat = jnp.concatenate([c_re, c_im, c_re, c_im], axis=2)
    cr4 = jnp.concatenate([c_re] * 4, axis=2)
    ci4 = jnp.concatenate([c_im] * 4, axis=2)
    dtile = jnp.tile(d.reshape(g, 1, SSM_GROUP), (1, 1, SSM_LAGS))
    return pc, pr2, tile_b(b_re), tile_b(b_im), ccat, cr4, ci4, dtile


def kernel(x_prompt, x_sample, ab_w_in, ab_q_norm, ab_k_norm, ssm_a_re_f, ssm_a_im_f, ssm_log_step_f,
           ssm_a_re_b, ssm_a_im_b, ssm_log_step_b, ssm_b_re, ssm_b_im, ssm_c_re, ssm_c_im, ssm_d, ssm_w_glu,
           ssm_b_glu, ab_w_out, c_w_in, c_ln_g, c_ln_b, c_w_s, c_b_s, c_w_out, ln_mix_g, ln_mix_b, ff_w1, ff_w2,
           ln_ff_g, ln_ff_b):
    xs = [x_prompt, x_sample]
    row2 = lambda v: v.reshape(1, -1)

    slot_cols = (_HEAD_OF_SLOT[:, None] * HEAD_DIM + np.arange(HEAD_DIM)[None, :]).reshape(-1)
    w_in = ab_w_in[0]
    w_in = jnp.concatenate([w_in[:, slot_cols], w_in[:, ATTN_WIDTH:]], axis=1).astype(BF16)
    scale = HEAD_DIM ** -0.5 * math.log2(math.e)
    gain = jnp.concatenate([jnp.tile(ab_q_norm[0] * scale, N_HEADS), jnp.tile(ab_k_norm[0], N_KV_HEADS)])
    score_bound = (1.05 * HEAD_DIM * scale * jnp.max(jnp.abs(ab_q_norm[0])) * jnp.max(jnp.abs(ab_k_norm[0])))
    score_bound = score_bound.reshape(1, 1).astype(F32)
    hid = np.arange(LANES) // HEAD_DIM
    hmean = jnp.asarray((hid[:, None] == hid[None, :]).astype(np.float32) / HEAD_DIM, BF16)
    w_out = ab_w_out[0]
    w_out_a = w_out[:ATTN_WIDTH][slot_cols].astype(BF16)
    w_out_s = w_out[ATTN_WIDTH:].astype(BF16)
    w_glu = ssm_w_glu[0].astype(BF16)

    attn, us, pairs = [], [], []
    cos_t, sin_t = _rope_tables(max(x.shape[1] for x in xs))
    for x in xs:
        b, seq_len, _ = x.shape
        q, k, v_t, u = _inproj(x.reshape(b * seq_len, D_MODEL), w_in, row2(gain), cos_t, sin_t, hmean, seq_len)
        attn.append(_attention(score_bound, q.reshape(b, seq_len, ATTN_WIDTH), k.reshape(b, seq_len, KV_WIDTH),
                               v_t).reshape(b * seq_len, ATTN_WIDTH))
        us.append(u)
        pairs.append(seq_len // (2 * SSM_CHUNK))
    ys = _s5(us[0], us[1], *_s5_param_layout(
        ssm_a_re_f[0], ssm_a_im_f[0], ssm_log_step_f[0], ssm_a_re_b[0], ssm_a_im_b[0], ssm_log_step_b[0],
        ssm_b_re[0], ssm_b_im[0], ssm_c_re[0], ssm_c_im[0], ssm_d[0]), *_selectors(), tuple(pairs))

    ff_w1b = ff_w1.astype(BF16)
    ff_w2b = ff_w2.astype(BF16)
    ln4 = [jnp.stack([ln_mix_g[i], ln_mix_b[i], ln_ff_g[i], ln_ff_b[i]]) for i in range(DEPTH)]
    cln2 = jnp.stack([c_ln_g[0], c_ln_b[0]])
    bs_full = jnp.broadcast_to(c_b_s[0][:, :, None], (SGU_HEADS, SGU_CHUNK, LANES))
    c_w_in_b, c_w_s_b, c_w_out_b = c_w_in[0].astype(BF16), c_w_s[0].astype(BF16), c_w_out[0].astype(BF16)
    outs = []
    for x, a, y in zip(xs, attn, ys):
        b, seq_len, _ = x.shape
        n = b * seq_len
        h = _layer0_tail(x.reshape(n, D_MODEL), a, y, w_glu, row2(ssm_b_glu[0]), w_out_a, w_out_s,
                         ln4[0], ff_w1b, ff_w2b)
        h = _layer1(h, c_w_in_b, cln2, c_w_s_b, bs_full, c_w_out_b, ln4[1], ff_w1b, ff_w2b)
        outs.append(h.reshape(b, seq_len, D_MODEL))
    return tuple(outs)
```

```python
import functools
import math

import numpy as np
import jax
import jax.numpy as jnp
from jax import lax
from jax.experimental import pallas as pl
from jax.experimental.pallas import tpu as pltpu

F32 = jnp.float32
BF16 = jnp.bfloat16

D_MODEL = 1024
GRID_W = 64
N_HEADS = 8
N_KV_HEADS = 2
HEAD_DIM = 64
ATTN_WIDTH = N_HEADS * HEAD_DIM
KV_WIDTH = N_KV_HEADS * HEAD_DIM
QK_WIDTH = ATTN_WIDTH + KV_WIDTH
ROPE_THETA = 10000.0
SSM_WIDTH = D_MODEL - ATTN_WIDTH
SSM_GROUP = 16
SSM_GROUPS = SSM_WIDTH // SSM_GROUP
SSM_STATE = 64
SGU_HEADS = 8
SGU_CHUNK = 128
D_FF = 4 * D_MODEL
DEPTH = 2
AB_IN = ATTN_WIDTH + 2 * KV_WIDTH + SSM_WIDTH
ALPHA = (2 * DEPTH) ** 0.25
EPS = 1e-6

LANES = 128
VMEM_LIMIT_BYTES = 56 * 1024 * 1024

SSM_CHUNK = 64
SSM_ROW = SSM_CHUNK * SSM_GROUP
SSM_LAGS = 2 * SSM_CHUNK
SSM_TAPS = SSM_LAGS * SSM_GROUP
SSM_SDIM = 4 * SSM_STATE

LAYER0_TAIL_TILE = 1024
LAYER1_TILE = 512


def _cparams(sem):
    return pltpu.CompilerParams(dimension_semantics=sem, vmem_limit_bytes=VMEM_LIMIT_BYTES)


def _const_spec(shape):
    nd = len(shape)
    return pl.BlockSpec(shape, lambda *_: (0,) * nd, pipeline_mode=pl.Buffered(1))


def _layer_norm(r, g, b):
    mu = jnp.mean(r, axis=-1, keepdims=True)
    d = r - mu
    var = jnp.mean(d * d, axis=-1, keepdims=True)
    return d * lax.rsqrt(var + EPS) * g + b


def _split_bf16(x):
    hi = x.astype(BF16)
    lo = (x - hi.astype(F32)).astype(BF16)
    return hi, lo


def _select_cols(x, e):
    hi, lo = _split_bf16(x)
    y = jnp.dot(jnp.concatenate([hi, lo], axis=0), e, preferred_element_type=F32)
    return y[:x.shape[0]] + y[x.shape[0]:]


def _select_rows(e, x):
    hi, lo = _split_bf16(x)
    y = jnp.dot(e, jnp.concatenate([hi, lo], axis=1), preferred_element_type=F32)
    return y[:, :x.shape[1]] + y[:, x.shape[1]:]


def _inproj_kernel(x_ref, w_ref, gain_ref, cos_ref, sin_ref, hmean_ref, q_ref, k_ref, v_ref, u_ref):
    subs = [slice(r0, r0 + INPROJ_SUB) for r0 in range(0, x_ref.shape[0], INPROJ_SUB)]
    hs = [jnp.dot(x_ref[sl, :].astype(BF16), w_ref[...], preferred_element_type=F32) for sl in subs]
    mss = []
    for h in hs:
        sq = (h[:, :QK_WIDTH] * h[:, :QK_WIDTH]).astype(BF16)
        mss.append(jnp.concatenate(
            [jnp.dot(sq[:, j * LANES:(j + 1) * LANES], hmean_ref[...], preferred_element_type=F32)
             for j in range(QK_WIDTH // LANES)], axis=1))
    qkns = [h[:, :QK_WIDTH] * lax.rsqrt(ms + EPS) * gain_ref[...] for h, ms in zip(hs, mss)]
    even = (lax.broadcasted_iota(jnp.int32, (INPROJ_SUB, LANES), 1) % 2) == 0
    for sl, qkn in zip(subs, qkns):
        c = cos_ref[sl, :]
        s = sin_ref[sl, :]
        for j in range(QK_WIDTH // LANES):
            xs = qkn[:, j * LANES:(j + 1) * LANES]
            nxt = pltpu.roll(xs, LANES - 1, 1)
            prv = pltpu.roll(xs, 1, 1)
            roped = (xs * c + jnp.where(even, nxt, prv) * s).astype(BF16)
            if j < ATTN_WIDTH // LANES:
                q_ref[sl, j * LANES:(j + 1) * LANES] = roped
            else:
                k_ref[sl, :] = roped
    for sl, h in zip(subs, hs):
        v_ref[:, sl] = h[:, QK_WIDTH:QK_WIDTH + KV_WIDTH].T.astype(BF16)
        for r in range(INPROJ_SUB // LANES):
            blk = h[r * LANES:(r + 1) * LANES, QK_WIDTH + KV_WIDTH:]
            u_ref[:, sl.start // LANES + r, :] = blk.T


INPROJ_TILE = 1024
INPROJ_SUB = 256


def _inproj(x2, w_in, gain, cos_t, sin_t, hmean, seq_len):
    n = x2.shape[0]
    tm = INPROJ_TILE
    nper = seq_len // tm
    return pl.pallas_call(
        _inproj_kernel,
        grid=(n // tm,),
        in_specs=[
            pl.BlockSpec((tm, D_MODEL), lambda i: (i, 0)),
            _const_spec((D_MODEL, AB_IN)),
            _const_spec((1, QK_WIDTH)),
            pl.BlockSpec((tm, LANES), lambda i: (i % nper, 0)),
            pl.BlockSpec((tm, LANES), lambda i: (i % nper, 0)),
            _const_spec((LANES, LANES)),
        ],
        out_specs=[
            pl.BlockSpec((tm, ATTN_WIDTH), lambda i: (i, 0)),
            pl.BlockSpec((tm, KV_WIDTH), lambda i: (i, 0)),
            pl.BlockSpec((KV_WIDTH, tm), lambda i: (0, i)),
            pl.BlockSpec((SSM_WIDTH, tm // LANES, LANES), lambda i: (0, i, 0)),
        ],
        out_shape=[
            jax.ShapeDtypeStruct((n, ATTN_WIDTH), BF16),
            jax.ShapeDtypeStruct((n, KV_WIDTH), BF16),
            jax.ShapeDtypeStruct((KV_WIDTH, n), BF16),
            jax.ShapeDtypeStruct((SSM_WIDTH, n // LANES, LANES), F32),
        ],
        compiler_params=_cparams(("parallel",)),
        name="inproj",
    )(x2, w_in, gain, cos_t, sin_t, hmean)


ATT_UNIT_Q = 64


ATT_SHIFT_LIMIT = 60.0


def _attn_kernel(bound_ref, q_ref, k_ref, vt_ref, o_ref,
                 s0_ref, s1_ref, p0_ref, p1_ref, m0_ref, m1_ref, l0_ref, l1_ref):
    seq_len = k_ref.shape[1]
    n_units = seq_len // ATT_UNIT_Q
    rep = N_HEADS // N_KV_HEADS
    lane = lax.broadcasted_iota(jnp.int32, (ATT_UNIT_Q, LANES), 1)
    lo_half = lane < HEAD_DIM
    zero = jnp.zeros((ATT_UNIT_Q, LANES), BF16)
    bound = bound_ref[0, 0]

    def rows_of(t):
        return pl.ds(pl.multiple_of(t * ATT_UNIT_Q, ATT_UNIT_Q), ATT_UNIT_Q)

    def raw_scores(t):
        q = q_ref[0, rows_of(t), :]
        slots = [q[:, m * LANES:(m + 1) * LANES] for m in range(rep)]
        qs = jnp.concatenate([jnp.where(lo_half, x, zero) for x in slots]
                             + [jnp.where(lo_half, zero, x) for x in slots], axis=0)
        return lax.dot_general(k_ref[0], qs, (((1,), (1,)), ((), ())), preferred_element_type=F32)

    def scores(t, s_ref, m_ref):
        s = raw_scores(t)
        s_ref[...] = s
        m_ref[...] = jnp.max(s, axis=0, keepdims=True)

    def probs(s_ref, m_ref, p_ref, l_ref):
        p = jnp.exp2(s_ref[...] - m_ref[...])
        l_ref[...] = jnp.sum(p, axis=0, keepdims=True)
        p_ref[...] = p.astype(BF16)

    def bounded_probs(t, p_ref, l_ref):
        p = jnp.exp2(raw_scores(t) - bound)
        l_ref[...] = jnp.sum(p, axis=0, keepdims=True)
        p_ref[...] = p.astype(BF16)

    def output(t, p_ref, l_ref):
        o_t = jnp.dot(vt_ref[...], p_ref[...], preferred_element_type=F32) / l_ref[...]
        o = o_t.T
        for m in range(rep):
            g0 = o[m * ATT_UNIT_Q:(m + 1) * ATT_UNIT_Q]
            g1 = o[(rep + m) * ATT_UNIT_Q:(rep + m + 1) * ATT_UNIT_Q]
            o_ref[0, rows_of(t), m * LANES:(m + 1) * LANES] = jnp.where(lo_half, g0, g1).astype(BF16)

    @pl.when(bound <= ATT_SHIFT_LIMIT)
    def _bounded_shift():
        bounded_probs(0, p0_ref, l0_ref)

        def body(i, carry):
            t = 2 * i + 1
            bounded_probs(t, p1_ref, l1_ref)
            output(t - 1, p0_ref, l0_ref)
            bounded_probs(t + 1, p0_ref, l0_ref)
            output(t, p1_ref, l1_ref)
            return carry

        lax.fori_loop(0, (n_units - 2) // 2, body, 0)
        bounded_probs(n_units - 1, p1_ref, l1_ref)
        output(n_units - 2, p0_ref, l0_ref)
        output(n_units - 1, p1_ref, l1_ref)

    @pl.when(bound > ATT_SHIFT_LIMIT)
    def _row_max_shift():
        scores(0, s0_ref, m0_ref)
        scores(1, s1_ref, m1_ref)
        probs(s0_ref, m0_ref, p0_ref, l0_ref)

        def body(i, carry):
            t = 2 * i + 2
            scores(t, s0_ref, m0_ref)
            probs(s1_ref, m1_ref, p1_ref, l1_ref)
            output(t - 2, p0_ref, l0_ref)
            scores(t + 1, s1_ref, m1_ref)
            probs(s0_ref, m0_ref, p0_ref, l0_ref)
            output(t - 1, p1_ref, l1_ref)
            return carry

        lax.fori_loop(0, (n_units - 2) // 2, body, 0)
        probs(s1_ref, m1_ref, p1_ref, l1_ref)
        output(n_units - 2, p0_ref, l0_ref)
        output(n_units - 1, p1_ref, l1_ref)


def _attention(bound, q, k, v_t):
    b, seq_len, _ = q.shape
    cols = ATT_UNIT_Q * N_HEADS
    seq = lambda w: pl.BlockSpec((1, seq_len, w), lambda i: (i, 0, 0))
    return pl.pallas_call(
        _attn_kernel,
        grid=(b,),
        in_specs=[pl.BlockSpec(memory_space=pltpu.SMEM), seq(ATTN_WIDTH), seq(KV_WIDTH),
                  pl.BlockSpec((KV_WIDTH, seq_len), lambda i: (0, i))],
        out_specs=seq(ATTN_WIDTH),
        out_shape=jax.ShapeDtypeStruct((b, seq_len, ATTN_WIDTH), BF16),
        scratch_shapes=[pltpu.VMEM((seq_len, cols), F32), pltpu.VMEM((seq_len, cols), F32),
                        pltpu.VMEM((seq_len, cols), BF16), pltpu.VMEM((seq_len, cols), BF16),
                        pltpu.VMEM((1, cols), F32), pltpu.VMEM((1, cols), F32),
                        pltpu.VMEM((1, cols), F32), pltpu.VMEM((1, cols), F32)],
        compiler_params=_cparams(("parallel",)),
        name="attention",
    )(bound, q, k, v_t)


SSM_PITCH = 132


def _s5_group(xp_ref, xs_ref, pc_ref, pr2_ref, bcr_ref, bci_ref, ccat_ref, cr4_ref, ci4_ref, dt_ref,
              e16_ref, x16_ref, yp_ref, ys_ref, s_ref, zt_ref, m_ref, yt_ref, pairs):
    ns = SSM_STATE
    pc = pc_ref[0]
    are_c, aim_c, dt_c = pc[:, 0:1], pc[:, 1:2], jnp.exp(pc[:, 2:3])
    row = lax.broadcasted_iota(jnp.int32, (2 * ns, LANES), 0)
    lan = lax.broadcasted_iota(jnp.int32, (2 * ns, LANES), 1)
    b_rows = row >= ns
    n_g = jnp.where(b_rows, lan - (SSM_CHUNK - 1), (SSM_CHUNK - 1) - lan)
    mask = n_g >= 0
    nf = jnp.where(mask, n_g, 0).astype(F32)
    rho = jnp.exp(are_c * dt_c * nf)
    ang = aim_c * dt_c * nf
    cs = jnp.where(mask, rho * jnp.cos(ang), 0.0)
    sn = jnp.where(mask, rho * jnp.sin(ang), 0.0)
    b_col = b_rows[:, 0:1]
    a1r = jnp.where(b_col, cs[:, SSM_CHUNK:SSM_CHUNK + 1], cs[:, SSM_CHUNK - 2:SSM_CHUNK - 1]) - 1.0
    a1i = jnp.where(b_col, sn[:, SSM_CHUNK:SSM_CHUNK + 1], sn[:, SSM_CHUNK - 2:SSM_CHUNK - 1])
    den = are_c * are_c + aim_c * aim_c
    cr = (a1r * are_c + a1i * aim_c) / den
    ci = (a1i * are_c - a1r * aim_c) / den
    yield

    def tile_lanes(x):
        return jnp.concatenate([x] * (SSM_TAPS // LANES), axis=1)

    bre, bim = bcr_ref[0], bci_ref[0]
    bbr = tile_lanes(cr * bre - ci * bim)
    bbi = tile_lanes(cr * bim + ci * bre)
    base = jnp.concatenate([cs[:ns], sn[:ns], cs[ns:], sn[ns:]], axis=0)
    ex = _select_cols(base, e16_ref[...])
    ecf, esf, ecb, esb = ex[:ns], ex[ns:2 * ns], ex[2 * ns:3 * ns], ex[3 * ns:]
    z_fre = ecf * bbr[:ns] - esf * bbi[:ns]
    z_fim = esf * bbr[:ns] + ecf * bbi[:ns]
    z_bre = ecb * bbr[ns:] - esb * bbi[ns:]
    z_bim = esb * bbr[ns:] + ecb * bbi[ns:]
    yield

    g_tab = jnp.concatenate([z_fre, -z_fim, z_bre, -z_bim], axis=0)
    ch, cl = _split_bf16(ccat_ref[0])
    gh, gl = _split_bf16(g_tab)
    w = (jnp.dot(ch, gh, preferred_element_type=F32) + jnp.dot(ch, gl, preferred_element_type=F32)
         + jnp.dot(cl, gh, preferred_element_type=F32))
    wrow = lax.broadcasted_iota(jnp.int32, (SSM_GROUP, SSM_TAPS), 0)
    wcol = lax.broadcasted_iota(jnp.int32, (SSM_GROUP, SSM_TAPS), 1)
    on_diag = (wcol // SSM_GROUP == SSM_CHUNK - 1) & (wcol % SSM_GROUP == wrow)
    w = w + jnp.where(on_diag, dt_ref[0], 0.0)
    yield
    per_vreg = LANES // SSM_GROUP
    rolled = [w] + [pltpu.roll(w, SSM_TAPS - SSM_GROUP * kk, 1) for kk in range(1, per_vreg)]
    for j in range(SSM_CHUNK):
        r = SSM_CHUNK - 1 - j
        blk = rolled[r % per_vreg][:, (r // per_vreg) * LANES:(r // per_vreg) * LANES + SSM_ROW]
        m_ref[j * SSM_GROUP:(j + 1) * SSM_GROUP, :] = blk.astype(BF16)

    p2 = jnp.concatenate([z_fre[:, :SSM_ROW], z_fim[:, :SSM_ROW], z_bre[:, SSM_ROW:], z_bim[:, SSM_ROW:]],
                         axis=0).astype(BF16)
    yield

    pr2 = pr2_ref[0]
    j_r = lax.broadcasted_iota(jnp.int32, (SSM_CHUNK, LANES), 0)
    l_r = lax.broadcasted_iota(jnp.int32, (SSM_CHUNK, LANES), 1)
    lo_r = l_r < ns
    n_q = jnp.where(lo_r, j_r + 1, (SSM_CHUNK - 1) - j_r).astype(F32)
    dt2 = jnp.exp(pr2[2:3, :])
    rho = jnp.exp(pr2[0:1, :] * dt2 * n_q)
    ang = pr2[1:2, :] * dt2 * n_q
    c2 = rho * jnp.cos(ang)
    s2 = rho * jnp.sin(ang)
    base_q = jnp.concatenate([jnp.where(lo_r, c2, -pltpu.roll(s2, ns, 1)),
                              jnp.where(lo_r, pltpu.roll(c2, ns, 1), -s2)], axis=1)
    a1x = _select_rows(x16_ref[...], base_q)
    sign = jnp.where(lax.broadcasted_iota(jnp.int32, (SSM_ROW, LANES), 1) < ns, 1.0, -1.0)
    a2x = jnp.concatenate([pltpu.roll(a1x[:, :LANES], ns, 1) * sign,
                           pltpu.roll(a1x[:, LANES:], ns, 1) * sign], axis=1)
    q2 = (a1x * jnp.concatenate([cr4_ref[0]] * SSM_CHUNK, axis=0)
          + a2x * jnp.concatenate([ci4_ref[0]] * SSM_CHUNK, axis=0)).astype(BF16)

    n_s = (SSM_CHUNK * jnp.left_shift(1, jnp.minimum(lan, 8))).astype(F32)
    rho = jnp.exp(are_c * dt_c * n_s)
    ang = aim_c * dt_c * n_s
    sc_r = rho * jnp.cos(ang)
    sc_i = rho * jnp.sin(ang)
    yield

    for seg, x_ref in enumerate((xp_ref, xs_ref)):
        for hi in range(SSM_GROUP):
            s_ref[seg, hi * SSM_PITCH:hi * SSM_PITCH + LANES, :] = x_ref[hi].T
    yield
    for par in range(2):
        for i in range(SSM_CHUNK):
            for seg in range(2):
                rows16 = s_ref[seg, pl.ds(par * SSM_CHUNK + i, SSM_GROUP, stride=SSM_PITCH), :]
                blk = 2 * par + seg
                zt_ref[i * SSM_GROUP:(i + 1) * SSM_GROUP, blk * LANES:(blk + 1) * LANES] = rows16.astype(BF16)
    yield

    zt = zt_ref[...]
    yt = jnp.dot(m_ref[...], zt, preferred_element_type=F32)
    st = jnp.dot(p2, zt, preferred_element_type=F32)
    yield

    def cmul(kk, lo, v):
        ar, ai = sc_r[lo:lo + ns, kk:kk + 1], sc_i[lo:lo + ns, kk:kk + 1]
        return jnp.concatenate([ar * v[:ns] - ai * v[ns:], ar * v[ns:] + ai * v[:ns]], axis=0)

    carried = [None] * 4
    for seg, npairs in enumerate(pairs):
        c2i = lax.broadcasted_iota(jnp.int32, (2 * ns, LANES), 1) % npairs
        right = lambda v, d: jnp.where(c2i >= d, pltpu.roll(v, d, 1), 0.0)
        left = lambda v, d: jnp.where(c2i < npairs - d, pltpu.roll(v, LANES - d, 1), 0.0)
        ev, od = st[:, seg * LANES:(seg + 1) * LANES], st[:, (2 + seg) * LANES:(3 + seg) * LANES]
        ef, of, eb, ob = ev[:2 * ns], od[:2 * ns], ev[2 * ns:], od[2 * ns:]
        ef, of = ef + cmul(0, 0, right(of, 1)), of + cmul(0, 0, ef)
        eb, ob = eb + cmul(0, ns, ob), ob + cmul(0, ns, left(eb, 1))
        for kk in range(1, int(math.log2(npairs)) + 1):
            d = 1 << (kk - 1)
            ef, of = ef + cmul(kk, 0, right(ef, d)), of + cmul(kk, 0, right(of, d))
            eb, ob = eb + cmul(kk, ns, left(eb, d)), ob + cmul(kk, ns, left(ob, d))
        carried[seg] = jnp.concatenate([right(of, 1), ob], axis=0)
        carried[2 + seg] = jnp.concatenate([ef, left(eb, 1)], axis=0)
    xin = jnp.concatenate(carried, axis=1).astype(BF16)
    yield
    yt = yt + jnp.dot(q2, xin, preferred_element_type=F32)

    for blk in range(4):
        yt_ref[blk] = yt[:, blk * LANES:(blk + 1) * LANES]
    for seg, y_ref in enumerate((yp_ref, ys_ref)):
        for ho in range(SSM_GROUP):
            ev = yt_ref[seg, pl.ds(ho, SSM_CHUNK, stride=SSM_GROUP), :]
            od = yt_ref[2 + seg, pl.ds(ho, SSM_CHUNK, stride=SSM_GROUP), :]
            y_ref[ho] = jnp.concatenate([ev, od], axis=0).T


S5_GROUPS_PER_STEP = 2


def _s5_kernel(xp_ref, xs_ref, pc_ref, pr2_ref, bcr_ref, bci_ref, ccat_ref, cr4_ref, ci4_ref, dt_ref,
               e16_ref, x16_ref, yp_ref, ys_ref, s_ref, zt_ref, m_ref, yt_ref, *, pairs):
    one = lambda ref, g: ref.at[pl.ds(g, 1)]
    chans = lambda ref, g: ref.at[pl.ds(g * SSM_GROUP, SSM_GROUP)]
    streams = [
        _s5_group(chans(xp_ref, g), chans(xs_ref, g), one(pc_ref, g), one(pr2_ref, g), one(bcr_ref, g),
                  one(bci_ref, g), one(ccat_ref, g), one(cr4_ref, g), one(ci4_ref, g), one(dt_ref, g),
                  e16_ref, x16_ref, chans(yp_ref, g), chans(ys_ref, g),
                  s_ref.at[g], zt_ref.at[g], m_ref.at[g], yt_ref.at[g], pairs)
        for g in range(S5_GROUPS_PER_STEP)]
    while streams:
        streams = [s for s in streams if next(s, StopIteration) is not StopIteration]


def _s5(u_p, u_s, pc, pr2, bcr, bci, ccat, cr4, ci4, dtile, e16, x16, pairs):
    g = SSM_GROUPS // S5_GROUPS_PER_STEP
    gs = S5_GROUPS_PER_STEP
    rows = u_p.shape[1]
    assert u_s.shape[1] == rows == LANES
    grp = lambda shape: pl.BlockSpec((gs,) + shape, lambda i: (i, 0, 0))
    chan = pl.BlockSpec((gs * SSM_GROUP, rows, LANES), lambda i: (i, 0, 0))
    return pl.pallas_call(
        functools.partial(_s5_kernel, pairs=pairs),
        grid=(g,),
        in_specs=[chan, chan, grp((2 * SSM_STATE, 8)), grp((8, LANES)),
                  grp((2 * SSM_STATE, LANES)), grp((2 * SSM_STATE, LANES)),
                  grp((SSM_GROUP, SSM_SDIM)), grp((SSM_GROUP, SSM_SDIM)), grp((SSM_GROUP, SSM_SDIM)),
                  grp((1, SSM_TAPS)), _const_spec((LANES, SSM_TAPS)), _const_spec((SSM_ROW, SSM_CHUNK))],
        out_specs=[chan, chan],
        out_shape=[jax.ShapeDtypeStruct(u_p.shape, F32), jax.ShapeDtypeStruct(u_s.shape, F32)],
        scratch_shapes=[pltpu.VMEM((gs, 2, SSM_GROUP * SSM_PITCH, LANES), F32),
                        pltpu.VMEM((gs, SSM_ROW, 4 * LANES), BF16),
                        pltpu.VMEM((gs, SSM_ROW, SSM_ROW), BF16),
                        pltpu.VMEM((gs, 4, SSM_ROW, LANES), F32)],
        compiler_params=_cparams(("parallel",)),
        name="s5",
    )(u_p, u_s, pc, pr2, bcr, bci, ccat, cr4, ci4, dtile, e16, x16)


FF_CHUNK = 1024
FF_ROWS = 256
MIXER_ROWS = 128


def _ffn_ln(x, w1_ref, w2_ref, g, b):
    xs = [x[r0:r0 + FF_ROWS] for r0 in range(0, x.shape[0], FF_ROWS)]
    xbs = [v.astype(BF16) for v in xs]
    accs = [ALPHA * v for v in xs]
    for c in range(D_FF // FF_CHUNK):
        for i in range(len(xs)):
            h = jnp.dot(xbs[i], w1_ref[0, :, c * FF_CHUNK:(c + 1) * FF_CHUNK], preferred_element_type=F32)
            h = jnp.maximum(h, 0.0)
            accs[i] = accs[i] + jnp.dot((h * h).astype(BF16), w2_ref[0, c * FF_CHUNK:(c + 1) * FF_CHUNK, :],
                                        preferred_element_type=F32)
    return jnp.concatenate([_layer_norm(a, g, b) for a in accs], axis=0)


def _layer0_tail_kernel(x_ref, a_ref, y_ref, wg_ref, bg_ref, woa_ref, wos_ref, ln_ref, w1_ref, w2_ref, o_ref):
    rows_per_tile = x_ref.shape[0] // LANES
    row0 = (pl.program_id(0) % (y_ref.shape[1] // rows_per_tile)) * rows_per_tile
    ln = ln_ref[...]
    subs = [slice(r * LANES, (r + 1) * LANES) for r in range(rows_per_tile)]
    ys = [jax.nn.gelu(y_ref[:, row0 + r, :].T) for r in range(rows_per_tile)]
    gates = [jnp.dot(y.astype(BF16), wg_ref[...], preferred_element_type=F32) + bg_ref[...] for y in ys]
    zs = [(y * jax.nn.sigmoid(g)).astype(BF16) for y, g in zip(ys, gates)]
    mixes = [jnp.dot(a_ref[sl, :], woa_ref[...], preferred_element_type=F32)
             + jnp.dot(z, wos_ref[...], preferred_element_type=F32) for sl, z in zip(subs, zs)]
    x1 = jnp.concatenate([_layer_norm(ALPHA * x_ref[sl, :] + m, ln[0:1], ln[1:2]) for sl, m in zip(subs, mixes)],
                         axis=0)
    o_ref[...] = _ffn_ln(x1, w1_ref, w2_ref, ln[2:3], ln[3:4])


def _layer_weights_spec(shape, layer):
    return pl.BlockSpec((1,) + shape, lambda i: (layer, 0, 0), pipeline_mode=pl.Buffered(1))


def _layer0_tail(x2, attn, y, w_glu, b_glu, w_out_a, w_out_s, ln4, w1, w2):
    n = x2.shape[0]
    tm = LAYER0_TAIL_TILE
    tok = lambda w: pl.BlockSpec((tm, w), lambda i: (i, 0))
    y_rows = max(8, tm // LANES)
    return pl.pallas_call(
        _layer0_tail_kernel,
        grid=(n // tm,),
        in_specs=[tok(D_MODEL), tok(ATTN_WIDTH),
                  pl.BlockSpec((SSM_WIDTH, y_rows, LANES), lambda i: (0, i // (y_rows * LANES // tm), 0)),
                  _const_spec((SSM_WIDTH, SSM_WIDTH)), _const_spec((1, SSM_WIDTH)),
                  _const_spec((ATTN_WIDTH, D_MODEL)), _const_spec((SSM_WIDTH, D_MODEL)),
                  _const_spec((4, D_MODEL)),
                  _layer_weights_spec((D_MODEL, D_FF), 0), _layer_weights_spec((D_FF, D_MODEL), 0)],
        out_specs=tok(D_MODEL),
        out_shape=jax.ShapeDtypeStruct((n, D_MODEL), F32),
        compiler_params=_cparams(("parallel",)),
        name="layer0_tail",
    )(x2, attn, y, w_glu, b_glu, w_out_a, w_out_s, ln4, w1, w2)


def _layer1_kernel(x_ref, win_ref, cln_ref, ws_ref, bs_ref, wout_ref, ln_ref, w1_ref, w2_ref, o_ref):
    cln = cln_ref[...]
    ln = ln_ref[...]
    halves = [x_ref[r0:r0 + MIXER_ROWS, :] for r0 in range(0, x_ref.shape[0], MIXER_ROWS)]
    hs = [jax.nn.gelu(jnp.dot(x.astype(BF16), win_ref[...], preferred_element_type=F32)) for x in halves]
    vbs = [_layer_norm(h[:, D_MODEL:], cln[0:1], cln[1:2]).astype(BF16) for h in hs]
    gated = []
    for h, vb in zip(hs, vbs):
        rows = []
        for c in range(MIXER_ROWS // SGU_CHUNK):
            cols = []
            for hd in range(SGU_HEADS):
                blk = vb[c * SGU_CHUNK:(c + 1) * SGU_CHUNK, hd * LANES:(hd + 1) * LANES]
                cols.append(jnp.dot(ws_ref[hd], blk, preferred_element_type=F32) + bs_ref[hd])
            rows.append(jnp.concatenate(cols, axis=1))
        gated.append((h[:, :D_MODEL] * jnp.concatenate(rows, axis=0)).astype(BF16))
    outs = [jnp.dot(g, wout_ref[...], preferred_element_type=F32) for g in gated]
    x1 = jnp.concatenate([_layer_norm(ALPHA * x + o, ln[0:1], ln[1:2]) for x, o in zip(halves, outs)], axis=0)
    o_ref[...] = _ffn_ln(x1, w1_ref, w2_ref, ln[2:3], ln[3:4])


def _layer1(x2, w_in, cln2, w_s, b_s_full, w_out, ln4, w1, w2):
    n = x2.shape[0]
    tm = LAYER1_TILE
    tok = pl.BlockSpec((tm, D_MODEL), lambda i: (i, 0))
    return pl.pallas_call(
        _layer1_kernel,
        grid=(n // tm,),
        in_specs=[tok, _const_spec((D_MODEL, 2 * D_MODEL)), _const_spec((2, D_MODEL)),
                  _const_spec((SGU_HEADS, SGU_CHUNK, SGU_CHUNK)), _const_spec((SGU_HEADS, SGU_CHUNK, LANES)),
                  _const_spec((D_MODEL, D_MODEL)), _const_spec((4, D_MODEL)),
                  _layer_weights_spec((D_MODEL, D_FF), 1), _layer_weights_spec((D_FF, D_MODEL), 1)],
        out_specs=tok,
        out_shape=jax.ShapeDtypeStruct((n, D_MODEL), F32),
        compiler_params=_cparams(("parallel",)),
        name="layer1",
    )(x2, w_in, cln2, w_s, b_s_full, w_out, ln4, w1, w2)


_HEAD_OF_SLOT = np.array([0, 4, 1, 5, 2, 6, 3, 7])


def _rope_tables(seq_len):
    t = np.arange(seq_len)
    axis_dim = HEAD_DIM // 2
    inv_freq = ROPE_THETA ** (-np.arange(0, axis_dim, 2, dtype=np.float64) / axis_dim)
    ang = np.concatenate([(t // GRID_W)[:, None] * inv_freq, (t % GRID_W)[:, None] * inv_freq], axis=-1)
    cos = np.repeat(np.cos(ang), 2, axis=-1)
    sin = np.repeat(np.sin(ang), 2, axis=-1) * np.tile([-1.0, 1.0], HEAD_DIM // 2)
    reps = LANES // HEAD_DIM
    return (jnp.asarray(np.tile(cos, (1, reps)), F32), jnp.asarray(np.tile(sin, (1, reps)), F32))


def _selectors():
    col = np.arange(SSM_TAPS)
    row = np.arange(LANES)
    e16 = (col[None, :] // SSM_GROUP == row[:, None])
    r2 = np.arange(SSM_ROW)
    x16 = (r2[:, None] // SSM_GROUP == np.arange(SSM_CHUNK)[None, :])
    as_bf16 = lambda a: jnp.asarray(a.astype(np.float32), BF16)
    return as_bf16(e16), as_bf16(x16)


def _s5_param_layout(a_re_f, a_im_f, ls_f, a_re_b, a_im_b, ls_b, b_re, b_im, c_re, c_im, d):
    g, ns = SSM_GROUPS, SSM_STATE
    zeros = jnp.zeros((g, ns), F32)
    per = lambda a_re, a_im, ls: jnp.stack([a_re, a_im, jnp.broadcast_to(ls[:, None], (g, ns))] + [zeros] * 5,
                                           axis=1)
    pr2 = jnp.concatenate([per(a_re_f, a_im_f, ls_f), per(a_re_b, a_im_b, ls_b)], axis=2)
    pc = jnp.transpose(pr2, (0, 2, 1))
    tile_b = lambda b: jnp.tile(jnp.concatenate([b, b], axis=1), (1, 1, LANES // SSM_GROUP))
    ccat = jnp.concatenate([c_re, c_im, c_re, c_im], axis=2)
    cr4 = jnp.concatenate([c_re] * 4, axis=2)
    ci4 = jnp.concatenate([c_im] * 4, axis=2)
    dtile = jnp.tile(d.reshape(g, 1, SSM_GROUP), (1, 1, SSM_LAGS))
    return pc, pr2, tile_b(b_re), tile_b(b_im), ccat, cr4, ci4, dtile


def kernel(x_prompt, x_sample, ab_w_in, ab_q_norm, ab_k_norm, ssm_a_re_f, ssm_a_im_f, ssm_log_step_f,
           ssm_a_re_b, ssm_a_im_b, ssm_log_step_b, ssm_b_re, ssm_b_im, ssm_c_re, ssm_c_im, ssm_d, ssm_w_glu,
           ssm_b_glu, ab_w_out, c_w_in, c_ln_g, c_ln_b, c_w_s, c_b_s, c_w_out, ln_mix_g, ln_mix_b, ff_w1, ff_w2,
           ln_ff_g, ln_ff_b):
    xs = [x_prompt, x_sample]
    row2 = lambda v: v.reshape(1, -1)

    slot_cols = (_HEAD_OF_SLOT[:, None] * HEAD_DIM + np.arange(HEAD_DIM)[None, :]).reshape(-1)
    w_in = ab_w_in[0]
    w_in = jnp.concatenate([w_in[:, slot_cols], w_in[:, ATTN_WIDTH:]], axis=1).astype(BF16)
    scale = HEAD_DIM ** -0.5 * math.log2(math.e)
    gain = jnp.concatenate([jnp.tile(ab_q_norm[0] * scale, N_HEADS), jnp.tile(ab_k_norm[0], N_KV_HEADS)])
    score_bound = (1.05 * HEAD_DIM * scale * jnp.max(jnp.abs(ab_q_norm[0])) * jnp.max(jnp.abs(ab_k_norm[0])))
    score_bound = score_bound.reshape(1, 1).astype(F32)
    hid = np.arange(LANES) // HEAD_DIM
    hmean = jnp.asarray((hid[:, None] == hid[None, :]).astype(np.float32) / HEAD_DIM, BF16)
    w_out = ab_w_out[0]
    w_out_a = w_out[:ATTN_WIDTH][slot_cols].astype(BF16)
    w_out_s = w_out[ATTN_WIDTH:].astype(BF16)
    w_glu = ssm_w_glu[0].astype(BF16)

    attn, us, pairs = [], [], []
    cos_t, sin_t = _rope_tables(max(x.shape[1] for x in xs))
    for x in xs:
        b, seq_len, _ = x.shape
        q, k, v_t, u = _inproj(x.reshape(b * seq_len, D_MODEL), w_in, row2(gain), cos_t, sin_t, hmean, seq_len)
        attn.append(_attention(score_bound, q.reshape(b, seq_len, ATTN_WIDTH), k.reshape(b, seq_len, KV_WIDTH),
                               v_t).reshape(b * seq_len, ATTN_WIDTH))
        us.append(u)
        pairs.append(seq_len // (2 * SSM_CHUNK))
    ys = _s5(us[0], us[1], *_s5_param_layout(
        ssm_a_re_f[0], ssm_a_im_f[0], ssm_log_step_f[0], ssm_a_re_b[0], ssm_a_im_b[0], ssm_log_step_b[0],
        ssm_b_re[0], ssm_b_im[0], ssm_c_re[0], ssm_c_im[0], ssm_d[0]), *_selectors(), tuple(pairs))

    ff_w1b = ff_w1.astype(BF16)
    ff_w2b = ff_w2.astype(BF16)
    ln4 = [jnp.stack([ln_mix_g[i], ln_mix_b[i], ln_ff_g[i], ln_ff_b[i]]) for i in range(DEPTH)]
    cln2 = jnp.stack([c_ln_g[0], c_ln_b[0]])
    bs_full = jnp.broadcast_to(c_b_s[0][:, :, None], (SGU_HEADS, SGU_CHUNK, LANES))
    c_w_in_b, c_w_s_b, c_w_out_b = c_w_in[0].astype(BF16), c_w_s[0].astype(BF16), c_w_out[0].astype(BF16)
    outs = []
    for x, a, y in zip(xs, attn, ys):
        b, seq_len, _ = x.shape
        n = b * seq_len
        h = _layer0_tail(x.reshape(n, D_MODEL), a, y, w_glu, row2(ssm_b_glu[0]), w_out_a, w_out_s,
                         ln4[0], ff_w1b, ff_w2b)
        h = _layer1(h, c_w_in_b, cln2, c_w_s_b, bs_full, c_w_out_b, ln4[1], ff_w1b, ff_w2b)
        outs.append(h.reshape(b, seq_len, D_MODEL))
    return tuple(outs)
```

```python
import functools
import math

import numpy as np
import jax
import jax.numpy as jnp
from jax import lax
from jax.experimental import pallas as pl
from jax.experimental.pallas import tpu as pltpu

F32 = jnp.float32
BF16 = jnp.bfloat16

D_MODEL = 1024
GRID_W = 64
N_HEADS = 8
N_KV_HEADS = 2
HEAD_DIM = 64
ATTN_WIDTH = N_HEADS * HEAD_DIM
KV_WIDTH = N_KV_HEADS * HEAD_DIM
QK_WIDTH = ATTN_WIDTH + KV_WIDTH
ROPE_THETA = 10000.0
SSM_WIDTH = D_MODEL - ATTN_WIDTH
SSM_GROUP = 16
SSM_GROUPS = SSM_WIDTH // SSM_GROUP
SSM_STATE = 64
SGU_HEADS = 8
SGU_CHUNK = 128
D_FF = 4 * D_MODEL
DEPTH = 2
AB_IN = ATTN_WIDTH + 2 * KV_WIDTH + SSM_WIDTH
ALPHA = (2 * DEPTH) ** 0.25
EPS = 1e-6

LANES = 128
F32_SUBLANES = 8
VMEM_LIMIT_BYTES = 56 * 1024 * 1024

SSM_CHUNK = 64
SSM_ROW = SSM_CHUNK * SSM_GROUP
SSM_LAGS = 2 * SSM_CHUNK
SSM_TAPS = SSM_LAGS * SSM_GROUP
SSM_SDIM = 4 * SSM_STATE

LAYER0_TAIL_TILE = 1024
LAYER1_TILE = 512


def _cparams(sem):
    return pltpu.CompilerParams(dimension_semantics=sem, vmem_limit_bytes=VMEM_LIMIT_BYTES)


def _const_spec(shape):
    nd = len(shape)
    return pl.BlockSpec(shape, lambda *_: (0,) * nd, pipeline_mode=pl.Buffered(1))


def _layer_norm(r, g, b):
    mu = jnp.mean(r, axis=-1, keepdims=True)
    d = r - mu
    var = jnp.mean(d * d, axis=-1, keepdims=True)
    return d * lax.rsqrt(var + EPS) * g + b


def _split_bf16(x):
    hi = x.astype(BF16)
    lo = (x - hi.astype(F32)).astype(BF16)
    return hi, lo


def _select_cols(x, e):
    hi, lo = _split_bf16(x)
    y = jnp.dot(jnp.concatenate([hi, lo], axis=0), e, preferred_element_type=F32)
    return y[:x.shape[0]] + y[x.shape[0]:]


def _select_rows(e, x):
    hi, lo = _split_bf16(x)
    y = jnp.dot(e, jnp.concatenate([hi, lo], axis=1), preferred_element_type=F32)
    return y[:, :x.shape[1]] + y[:, x.shape[1]:]


def _inproj_kernel(x_ref, w_ref, gain_ref, cos_ref, sin_ref, hmean_ref, q_ref, k_ref, v_ref, u_ref):
    subs = [slice(r0, r0 + INPROJ_SUB) for r0 in range(0, x_ref.shape[0], INPROJ_SUB)]
    hs = [jnp.dot(x_ref[sl, :].astype(BF16), w_ref[...], preferred_element_type=F32) for sl in subs]
    mss = []
    for h in hs:
        sq = (h[:, :QK_WIDTH] * h[:, :QK_WIDTH]).astype(BF16)
        mss.append(jnp.concatenate(
            [jnp.dot(sq[:, j * LANES:(j + 1) * LANES], hmean_ref[...], preferred_element_type=F32)
             for j in range(QK_WIDTH // LANES)], axis=1))
    qkns = [h[:, :QK_WIDTH] * lax.rsqrt(ms + EPS) * gain_ref[...] for h, ms in zip(hs, mss)]
    even = (lax.broadcasted_iota(jnp.int32, (INPROJ_SUB, LANES), 1) % 2) == 0
    for sl, qkn in zip(subs, qkns):
        c = cos_ref[sl, :]
        s = sin_ref[sl, :]
        for j in range(QK_WIDTH // LANES):
            xs = qkn[:, j * LANES:(j + 1) * LANES]
            nxt = pltpu.roll(xs, LANES - 1, 1)
            prv = pltpu.roll(xs, 1, 1)
            roped = (xs * c + jnp.where(even, nxt, prv) * s).astype(BF16)
            if j < ATTN_WIDTH // LANES:
                q_ref[sl, j * LANES:(j + 1) * LANES] = roped
            else:
                k_ref[sl, :] = roped
    for sl, h in zip(subs, hs):
        v_ref[:, sl] = h[:, QK_WIDTH:QK_WIDTH + KV_WIDTH].T.astype(BF16)
        for r in range(INPROJ_SUB // LANES):
            blk = h[r * LANES:(r + 1) * LANES, QK_WIDTH + KV_WIDTH:]
            u_ref[:, sl.start // LANES + r, :] = blk.T


INPROJ_TILE = 1024
INPROJ_SUB = 256


def _inproj(x2, w_in, gain, cos_t, sin_t, hmean, seq_len):
    n = x2.shape[0]
    tm = INPROJ_TILE
    nper = seq_len // tm
    return pl.pallas_call(
        _inproj_kernel,
        grid=(n // tm,),
        in_specs=[
            pl.BlockSpec((tm, D_MODEL), lambda i: (i, 0)),
            _const_spec((D_MODEL, AB_IN)),
            _const_spec((1, QK_WIDTH)),
            pl.BlockSpec((tm, LANES), lambda i: (i % nper, 0)),
            pl.BlockSpec((tm, LANES), lambda i: (i % nper, 0)),
            _const_spec((LANES, LANES)),
        ],
        out_specs=[
            pl.BlockSpec((tm, ATTN_WIDTH), lambda i: (i, 0)),
            pl.BlockSpec((tm, KV_WIDTH), lambda i: (i, 0)),
            pl.BlockSpec((KV_WIDTH, tm), lambda i: (0, i)),
            pl.BlockSpec((SSM_WIDTH, tm // LANES, LANES), lambda i: (0, i, 0)),
        ],
        out_shape=[
            jax.ShapeDtypeStruct((n, ATTN_WIDTH), BF16),
            jax.ShapeDtypeStruct((n, KV_WIDTH), BF16),
            jax.ShapeDtypeStruct((KV_WIDTH, n), BF16),
            jax.ShapeDtypeStruct((SSM_WIDTH, n // LANES, LANES), F32),
        ],
        compiler_params=_cparams(("parallel",)),
        name="inproj",
    )(x2, w_in, gain, cos_t, sin_t, hmean)


ATT_UNIT_Q = 64


ATT_SHIFT_LIMIT = 60.0


def _attn_kernel(bound_ref, q_ref, k_ref, vt_ref, o_ref,
                 s0_ref, s1_ref, p0_ref, p1_ref, m0_ref, m1_ref, l0_ref, l1_ref):
    seq_len = k_ref.shape[1]
    n_units = seq_len // ATT_UNIT_Q
    rep = N_HEADS // N_KV_HEADS
    lane = lax.broadcasted_iota(jnp.int32, (ATT_UNIT_Q, LANES), 1)
    lo_half = lane < HEAD_DIM
    zero = jnp.zeros((ATT_UNIT_Q, LANES), BF16)
    bound = bound_ref[0, 0]

    def rows_of(t):
        return pl.ds(pl.multiple_of(t * ATT_UNIT_Q, ATT_UNIT_Q), ATT_UNIT_Q)

    def raw_scores(t):
        q = q_ref[0, rows_of(t), :]
        slots = [q[:, m * LANES:(m + 1) * LANES] for m in range(rep)]
        qs = jnp.concatenate([jnp.where(lo_half, x, zero) for x in slots]
                             + [jnp.where(lo_half, zero, x) for x in slots], axis=0)
        return lax.dot_general(k_ref[0], qs, (((1,), (1,)), ((), ())), preferred_element_type=F32)

    def scores(t, s_ref, m_ref):
        s = raw_scores(t)
        s_ref[...] = s
        m_ref[...] = jnp.max(s, axis=0, keepdims=True)

    def probs(s_ref, m_ref, p_ref, l_ref):
        p = jnp.exp2(s_ref[...] - m_ref[...])
        l_ref[...] = jnp.sum(p, axis=0, keepdims=True)
        p_ref[...] = p.astype(BF16)

    def bounded_probs(t, p_ref, l_ref):
        p = jnp.exp2(raw_scores(t) - bound)
        l_ref[...] = jnp.sum(p, axis=0, keepdims=True)
        p_ref[...] = p.astype(BF16)

    def output(t, p_ref, l_ref):
        o_t = jnp.dot(vt_ref[...], p_ref[...], preferred_element_type=F32) / l_ref[...]
        o = o_t.T
        for m in range(rep):
            g0 = o[m * ATT_UNIT_Q:(m + 1) * ATT_UNIT_Q]
            g1 = o[(rep + m) * ATT_UNIT_Q:(rep + m + 1) * ATT_UNIT_Q]
            o_ref[0, rows_of(t), m * LANES:(m + 1) * LANES] = jnp.where(lo_half, g0, g1).astype(BF16)

    @pl.when(bound <= ATT_SHIFT_LIMIT)
    def _bounded_shift():
        bounded_probs(0, p0_ref, l0_ref)

        def body(i, carry):
            t = 2 * i + 1
            bounded_probs(t, p1_ref, l1_ref)
            output(t - 1, p0_ref, l0_ref)
            bounded_probs(t + 1, p0_ref, l0_ref)
            output(t, p1_ref, l1_ref)
            return carry

        lax.fori_loop(0, (n_units - 2) // 2, body, 0)
        bounded_probs(n_units - 1, p1_ref, l1_ref)
        output(n_units - 2, p0_ref, l0_ref)
        output(n_units - 1, p1_ref, l1_ref)

    @pl.when(bound > ATT_SHIFT_LIMIT)
    def _row_max_shift():
        scores(0, s0_ref, m0_ref)
        scores(1, s1_ref, m1_ref)
        probs(s0_ref, m0_ref, p0_ref, l0_ref)

        def body(i, carry):
            t = 2 * i + 2
            scores(t, s0_ref, m0_ref)
            probs(s1_ref, m1_ref, p1_ref, l1_ref)
            output(t - 2, p0_ref, l0_ref)
            scores(t + 1, s1_ref, m1_ref)
            probs(s0_ref, m0_ref, p0_ref, l0_ref)
            output(t - 1, p1_ref, l1_ref)
            return carry

        lax.fori_loop(0, (n_units - 2) // 2, body, 0)
        probs(s1_ref, m1_ref, p1_ref, l1_ref)
        output(n_units - 2, p0_ref, l0_ref)
        output(n_units - 1, p1_ref, l1_ref)


def _attention(bound, q, k, v_t):
    b, seq_len, _ = q.shape
    cols = ATT_UNIT_Q * N_HEADS
    seq = lambda w: pl.BlockSpec((1, seq_len, w), lambda i: (i, 0, 0))
    return pl.pallas_call(
        _attn_kernel,
        grid=(b,),
        in_specs=[pl.BlockSpec(memory_space=pltpu.SMEM), seq(ATTN_WIDTH), seq(KV_WIDTH),
                  pl.BlockSpec((KV_WIDTH, seq_len), lambda i: (0, i))],
        out_specs=seq(ATTN_WIDTH),
        out_shape=jax.ShapeDtypeStruct((b, seq_len, ATTN_WIDTH), BF16),
        scratch_shapes=[pltpu.VMEM((seq_len, cols), F32), pltpu.VMEM((seq_len, cols), F32),
                        pltpu.VMEM((seq_len, cols), BF16), pltpu.VMEM((seq_len, cols), BF16),
                        pltpu.VMEM((1, cols), F32), pltpu.VMEM((1, cols), F32),
                        pltpu.VMEM((1, cols), F32), pltpu.VMEM((1, cols), F32)],
        compiler_params=_cparams(("parallel",)),
        name="attention",
    )(bound, q, k, v_t)


SSM_PITCH = 132


def _s5_group(xp_ref, xs_ref, pc_ref, pr2_ref, bcr_ref, bci_ref, ccat_ref, cr4_ref, ci4_ref, dt_ref,
              e16_ref, x16_ref, yp_ref, ys_ref, s_ref, zt_ref, m_ref, yt_ref, pairs):
    ns = SSM_STATE
    pc = pc_ref[0]
    are_c, aim_c, dt_c = pc[:, 0:1], pc[:, 1:2], jnp.exp(pc[:, 2:3])
    row = lax.broadcasted_iota(jnp.int32, (2 * ns, LANES), 0)
    lan = lax.broadcasted_iota(jnp.int32, (2 * ns, LANES), 1)
    b_rows = row >= ns
    n_g = jnp.where(b_rows, lan - (SSM_CHUNK - 1), (SSM_CHUNK - 1) - lan)
    mask = n_g >= 0
    nf = jnp.where(mask, n_g, 0).astype(F32)
    rho = jnp.exp(are_c * dt_c * nf)
    ang = aim_c * dt_c * nf
    cs = jnp.where(mask, rho * jnp.cos(ang), 0.0)
    sn = jnp.where(mask, rho * jnp.sin(ang), 0.0)
    b_col = b_rows[:, 0:1]
    a1r = jnp.where(b_col, cs[:, SSM_CHUNK:SSM_CHUNK + 1], cs[:, SSM_CHUNK - 2:SSM_CHUNK - 1]) - 1.0
    a1i = jnp.where(b_col, sn[:, SSM_CHUNK:SSM_CHUNK + 1], sn[:, SSM_CHUNK - 2:SSM_CHUNK - 1])
    den = are_c * are_c + aim_c * aim_c
    cr = (a1r * are_c + a1i * aim_c) / den
    ci = (a1i * are_c - a1r * aim_c) / den
    yield

    def tile_lanes(x):
        return jnp.concatenate([x] * (SSM_TAPS // LANES), axis=1)

    bre, bim = bcr_ref[0], bci_ref[0]
    bbr = tile_lanes(cr * bre - ci * bim)
    bbi = tile_lanes(cr * bim + ci * bre)
    base = jnp.concatenate([cs[:ns], sn[:ns], cs[ns:], sn[ns:]], axis=0)
    ex = _select_cols(base, e16_ref[...])
    ecf, esf, ecb, esb = ex[:ns], ex[ns:2 * ns], ex[2 * ns:3 * ns], ex[3 * ns:]
    z_fre = ecf * bbr[:ns] - esf * bbi[:ns]
    z_fim = esf * bbr[:ns] + ecf * bbi[:ns]
    z_bre = ecb * bbr[ns:] - esb * bbi[ns:]
    z_bim = esb * bbr[ns:] + ecb * bbi[ns:]
    yield

    g_tab = jnp.concatenate([z_fre, -z_fim, z_bre, -z_bim], axis=0)
    ch, cl = _split_bf16(ccat_ref[0])
    gh, gl = _split_bf16(g_tab)
    w = (jnp.dot(ch, gh, preferred_element_type=F32) + jnp.dot(ch, gl, preferred_element_type=F32)
         + jnp.dot(cl, gh, preferred_element_type=F32))
    wrow = lax.broadcasted_iota(jnp.int32, (SSM_GROUP, SSM_TAPS), 0)
    wcol = lax.broadcasted_iota(jnp.int32, (SSM_GROUP, SSM_TAPS), 1)
    on_diag = (wcol // SSM_GROUP == SSM_CHUNK - 1) & (wcol % SSM_GROUP == wrow)
    w = w + jnp.where(on_diag, dt_ref[0], 0.0)
    yield
    per_vreg = LANES // SSM_GROUP
    rolled = [w] + [pltpu.roll(w, SSM_TAPS - SSM_GROUP * kk, 1) for kk in range(1, per_vreg)]
    for j in range(SSM_CHUNK):
        r = SSM_CHUNK - 1 - j
        blk = rolled[r % per_vreg][:, (r // per_vreg) * LANES:(r // per_vreg) * LANES + SSM_ROW]
        m_ref[j * SSM_GROUP:(j + 1) * SSM_GROUP, :] = blk.astype(BF16)

    p2 = jnp.concatenate([z_fre[:, :SSM_ROW], z_fim[:, :SSM_ROW], z_bre[:, SSM_ROW:], z_bim[:, SSM_ROW:]],
                         axis=0).astype(BF16)
    yield

    pr2 = pr2_ref[0]
    j_r = lax.broadcasted_iota(jnp.int32, (SSM_CHUNK, LANES), 0)
    l_r = lax.broadcasted_iota(jnp.int32, (SSM_CHUNK, LANES), 1)
    lo_r = l_r < ns
    n_q = jnp.where(lo_r, j_r + 1, (SSM_CHUNK - 1) - j_r).astype(F32)
    dt2 = jnp.exp(pr2[2:3, :])
    rho = jnp.exp(pr2[0:1, :] * dt2 * n_q)
    ang = pr2[1:2, :] * dt2 * n_q
    c2 = rho * jnp.cos(ang)
    s2 = rho * jnp.sin(ang)
    base_q = jnp.concatenate([jnp.where(lo_r, c2, -pltpu.roll(s2, ns, 1)),
                              jnp.where(lo_r, pltpu.roll(c2, ns, 1), -s2)], axis=1)
    a1x = _select_rows(x16_ref[...], base_q)
    sign = jnp.where(lax.broadcasted_iota(jnp.int32, (SSM_ROW, LANES), 1) < ns, 1.0, -1.0)
    a2x = jnp.concatenate([pltpu.roll(a1x[:, :LANES], ns, 1) * sign,
                           pltpu.roll(a1x[:, LANES:], ns, 1) * sign], axis=1)
    q2 = (a1x * jnp.concatenate([cr4_ref[0]] * SSM_CHUNK, axis=0)
          + a2x * jnp.concatenate([ci4_ref[0]] * SSM_CHUNK, axis=0)).astype(BF16)

    n_s = (SSM_CHUNK * jnp.left_shift(1, jnp.minimum(lan, 8))).astype(F32)
    rho = jnp.exp(are_c * dt_c * n_s)
    ang = aim_c * dt_c * n_s
    sc_r = rho * jnp.cos(ang)
    sc_i = rho * jnp.sin(ang)
    yield

    for seg, x_ref in enumerate((xp_ref, xs_ref)):
        for hi in range(SSM_GROUP):
            s_ref[seg, hi * SSM_PITCH:hi * SSM_PITCH + LANES, :] = x_ref[hi].T
    yield
    for par in range(2):
        for i in range(SSM_CHUNK):
            for seg in range(2):
                rows16 = s_ref[seg, pl.ds(par * SSM_CHUNK + i, SSM_GROUP, stride=SSM_PITCH), :]
                blk = 2 * par + seg
                zt_ref[i * SSM_GROUP:(i + 1) * SSM_GROUP, blk * LANES:(blk + 1) * LANES] = rows16.astype(BF16)
    yield

    zt = zt_ref[...]
    yt = jnp.dot(m_ref[...], zt, preferred_element_type=F32)
    st = jnp.dot(p2, zt, preferred_element_type=F32)
    yield

    def cmul(kk, lo, v):
        ar, ai = sc_r[lo:lo + ns, kk:kk + 1], sc_i[lo:lo + ns, kk:kk + 1]
        return jnp.concatenate([ar * v[:ns] - ai * v[ns:], ar * v[ns:] + ai * v[:ns]], axis=0)

    carried = [None] * 4
    for seg, npairs in enumerate(pairs):
        c2i = lax.broadcasted_iota(jnp.int32, (2 * ns, LANES), 1) % npairs
        right = lambda v, d: jnp.where(c2i >= d, pltpu.roll(v, d, 1), 0.0)
        left = lambda v, d: jnp.where(c2i < npairs - d, pltpu.roll(v, LANES - d, 1), 0.0)
        ev, od = st[:, seg * LANES:(seg + 1) * LANES], st[:, (2 + seg) * LANES:(3 + seg) * LANES]
        ef, of, eb, ob = ev[:2 * ns], od[:2 * ns], ev[2 * ns:], od[2 * ns:]
        ef, of = ef + cmul(0, 0, right(of, 1)), of + cmul(0, 0, ef)
        eb, ob = eb + cmul(0, ns, ob), ob + cmul(0, ns, left(eb, 1))
        for kk in range(1, int(math.log2(npairs)) + 1):
            d = 1 << (kk - 1)
            ef, of = ef + cmul(kk, 0, right(ef, d)), of + cmul(kk, 0, right(of, d))
            eb, ob = eb + cmul(kk, ns, left(eb, d)), ob + cmul(kk, ns, left(ob, d))
        carried[seg] = jnp.concatenate([right(of, 1), ob], axis=0)
        carried[2 + seg] = jnp.concatenate([ef, left(eb, 1)], axis=0)
    xin = jnp.concatenate(carried, axis=1).astype(BF16)
    yield
    yt = yt + jnp.dot(q2, xin, preferred_element_type=F32)

    for blk in range(4):
        yt_ref[blk] = yt[:, blk * LANES:(blk + 1) * LANES]
    for seg, y_ref in enumerate((yp_ref, ys_ref)):
        for ho in range(SSM_GROUP):
            ev = yt_ref[seg, pl.ds(ho, SSM_CHUNK, stride=SSM_GROUP), :]
            od = yt_ref[2 + seg, pl.ds(ho, SSM_CHUNK, stride=SSM_GROUP), :]
            y_ref[ho] = jnp.concatenate([ev, od], axis=0).T


S5_GROUPS_PER_STEP = 2


def _s5_kernel(xp_ref, xs_ref, pc_ref, pr2_ref, bcr_ref, bci_ref, ccat_ref, cr4_ref, ci4_ref, dt_ref,
               e16_ref, x16_ref, yp_ref, ys_ref, s_ref, zt_ref, m_ref, yt_ref, *, pairs):
    one = lambda ref, g: ref.at[pl.ds(g, 1)]
    chans = lambda ref, g: ref.at[pl.ds(g * SSM_GROUP, SSM_GROUP)]
    streams = [
        _s5_group(chans(xp_ref, g), chans(xs_ref, g), one(pc_ref, g), one(pr2_ref, g), one(bcr_ref, g),
                  one(bci_ref, g), one(ccat_ref, g), one(cr4_ref, g), one(ci4_ref, g), one(dt_ref, g),
                  e16_ref, x16_ref, chans(yp_ref, g), chans(ys_ref, g),
                  s_ref.at[g], zt_ref.at[g], m_ref.at[g], yt_ref.at[g], pairs)
        for g in range(S5_GROUPS_PER_STEP)]
    while streams:
        streams = [s for s in streams if next(s, StopIteration) is not StopIteration]


def _s5(u_p, u_s, pc, pr2, bcr, bci, ccat, cr4, ci4, dtile, e16, x16, pairs):
    g = SSM_GROUPS // S5_GROUPS_PER_STEP
    gs = S5_GROUPS_PER_STEP
    rows = u_p.shape[1]
    assert u_s.shape[1] == rows == LANES
    grp = lambda shape: pl.BlockSpec((gs,) + shape, lambda i: (i, 0, 0))
    chan = pl.BlockSpec((gs * SSM_GROUP, rows, LANES), lambda i: (i, 0, 0))
    return pl.pallas_call(
        functools.partial(_s5_kernel, pairs=pairs),
        grid=(g,),
        in_specs=[chan, chan, grp((2 * SSM_STATE, 8)), grp((8, LANES)),
                  grp((2 * SSM_STATE, LANES)), grp((2 * SSM_STATE, LANES)),
                  grp((SSM_GROUP, SSM_SDIM)), grp((SSM_GROUP, SSM_SDIM)), grp((SSM_GROUP, SSM_SDIM)),
                  grp((1, SSM_TAPS)), _const_spec((LANES, SSM_TAPS)), _const_spec((SSM_ROW, SSM_CHUNK))],
        out_specs=[chan, chan],
        out_shape=[jax.ShapeDtypeStruct(u_p.shape, F32), jax.ShapeDtypeStruct(u_s.shape, F32)],
        scratch_shapes=[pltpu.VMEM((gs, 2, SSM_GROUP * SSM_PITCH, LANES), F32),
                        pltpu.VMEM((gs, SSM_ROW, 4 * LANES), BF16),
                        pltpu.VMEM((gs, SSM_ROW, SSM_ROW), BF16),
                        pltpu.VMEM((gs, 4, SSM_ROW, LANES), F32)],
        compiler_params=_cparams(("parallel",)),
        name="s5",
    )(u_p, u_s, pc, pr2, bcr, bci, ccat, cr4, ci4, dtile, e16, x16)


FF_CHUNK = 1024
FF_ROWS = 256
MIXER_ROWS = 128


def _ffn_ln(x, w1_ref, w2_ref, g, b):
    xs = [x[r0:r0 + FF_ROWS] for r0 in range(0, x.shape[0], FF_ROWS)]
    xbs = [v.astype(BF16) for v in xs]
    accs = [ALPHA * v for v in xs]
    for c in range(D_FF // FF_CHUNK):
        for i in range(len(xs)):
            h = jnp.dot(xbs[i], w1_ref[0, :, c * FF_CHUNK:(c + 1) * FF_CHUNK], preferred_element_type=F32)
            h = jnp.maximum(h, 0.0)
            accs[i] = accs[i] + jnp.dot((h * h).astype(BF16), w2_ref[0, c * FF_CHUNK:(c + 1) * FF_CHUNK, :],
                                        preferred_element_type=F32)
    return jnp.concatenate([_layer_norm(a, g, b) for a in accs], axis=0)


def _layer0_tail_kernel(x_ref, a_ref, y_ref, wg_ref, bg_ref, woa_ref, wos_ref, ln_ref, w1_ref, w2_ref, o_ref):
    rows_per_tile = x_ref.shape[0] // LANES
    row0 = (pl.program_id(0) % (y_ref.shape[1] // rows_per_tile)) * rows_per_tile
    ln = ln_ref[...]
    subs = [slice(r * LANES, (r + 1) * LANES) for r in range(rows_per_tile)]
    ys = [jax.nn.gelu(y_ref[:, row0 + r, :].T) for r in range(rows_per_tile)]
    gates = [jnp.dot(y.astype(BF16), wg_ref[...], preferred_element_type=F32) + bg_ref[...] for y in ys]
    zs = [(y * jax.nn.sigmoid(g)).astype(BF16) for y, g in zip(ys, gates)]
    mixes = [jnp.dot(a_ref[sl, :], woa_ref[...], preferred_element_type=F32)
             + jnp.dot(z, wos_ref[...], preferred_element_type=F32) for sl, z in zip(subs, zs)]
    x1 = jnp.concatenate([_layer_norm(ALPHA * x_ref[sl, :] + m, ln[0:1], ln[1:2]) for sl, m in zip(subs, mixes)],
                         axis=0)
    o_ref[...] = _ffn_ln(x1, w1_ref, w2_ref, ln[2:3], ln[3:4])


def _layer_weights_spec(shape, layer):
    return pl.BlockSpec((1,) + shape, lambda i: (layer, 0, 0), pipeline_mode=pl.Buffered(1))


def _layer0_tail(x2, attn, y, w_glu, b_glu, w_out_a, w_out_s, ln4, w1, w2):
    n = x2.shape[0]
    tm = LAYER0_TAIL_TILE
    tok = lambda w: pl.BlockSpec((tm, w), lambda i: (i, 0))
    y_rows = max(F32_SUBLANES, tm // LANES)
    return pl.pallas_call(
        _layer0_tail_kernel,
        grid=(n // tm,),
        in_specs=[tok(D_MODEL), tok(ATTN_WIDTH),
                  pl.BlockSpec((SSM_WIDTH, y_rows, LANES), lambda i: (0, i // (y_rows * LANES // tm), 0)),
                  _const_spec((SSM_WIDTH, SSM_WIDTH)), _const_spec((1, SSM_WIDTH)),
                  _const_spec((ATTN_WIDTH, D_MODEL)), _const_spec((SSM_WIDTH, D_MODEL)),
                  _const_spec((4, D_MODEL)),
                  _layer_weights_spec((D_MODEL, D_FF), 0), _layer_weights_spec((D_FF, D_MODEL), 0)],
        out_specs=tok(D_MODEL),
        out_shape=jax.ShapeDtypeStruct((n, D_MODEL), F32),
        compiler_params=_cparams(("parallel",)),
        name="layer0_tail",
    )(x2, attn, y, w_glu, b_glu, w_out_a, w_out_s, ln4, w1, w2)


def _layer1_kernel(x_ref, win_ref, cln_ref, ws_ref, bs_ref, wout_ref, ln_ref, w1_ref, w2_ref, o_ref):
    cln = cln_ref[...]
    ln = ln_ref[...]
    halves = [x_ref[r0:r0 + MIXER_ROWS, :] for r0 in range(0, x_ref.shape[0], MIXER_ROWS)]
    hs = [jax.nn.gelu(jnp.dot(x.astype(BF16), win_ref[...], preferred_element_type=F32)) for x in halves]
    vbs = [_layer_norm(h[:, D_MODEL:], cln[0:1], cln[1:2]).astype(BF16) for h in hs]
    gated = []
    for h, vb in zip(hs, vbs):
        rows = []
        for c in range(MIXER_ROWS // SGU_CHUNK):
            cols = []
            for hd in range(SGU_HEADS):
                blk = vb[c * SGU_CHUNK:(c + 1) * SGU_CHUNK, hd * LANES:(hd + 1) * LANES]
                cols.append(jnp.dot(ws_ref[hd], blk, preferred_element_type=F32) + bs_ref[hd])
            rows.append(jnp.concatenate(cols, axis=1))
        gated.append((h[:, :D_MODEL] * jnp.concatenate(rows, axis=0)).astype(BF16))
    outs = [jnp.dot(g, wout_ref[...], preferred_element_type=F32) for g in gated]
    x1 = jnp.concatenate([_layer_norm(ALPHA * x + o, ln[0:1], ln[1:2]) for x, o in zip(halves, outs)], axis=0)
    o_ref[...] = _ffn_ln(x1, w1_ref, w2_ref, ln[2:3], ln[3:4])


def _layer1(x2, w_in, cln2, w_s, b_s_full, w_out, ln4, w1, w2):
    n = x2.shape[0]
    tm = LAYER1_TILE
    tok = pl.BlockSpec((tm, D_MODEL), lambda i: (i, 0))
    return pl.pallas_call(
        _layer1_kernel,
        grid=(n // tm,),
        in_specs=[tok, _const_spec((D_MODEL, 2 * D_MODEL)), _const_spec((2, D_MODEL)),
                  _const_spec((SGU_HEADS, SGU_CHUNK, SGU_CHUNK)), _const_spec((SGU_HEADS, SGU_CHUNK, LANES)),
                  _const_spec((D_MODEL, D_MODEL)), _const_spec((4, D_MODEL)),
                  _layer_weights_spec((D_MODEL, D_FF), 1), _layer_weights_spec((D_FF, D_MODEL), 1)],
        out_specs=tok,
        out_shape=jax.ShapeDtypeStruct((n, D_MODEL), F32),
        compiler_params=_cparams(("parallel",)),
        name="layer1",
    )(x2, w_in, cln2, w_s, b_s_full, w_out, ln4, w1, w2)


_HEAD_OF_SLOT = np.array([0, 4, 1, 5, 2, 6, 3, 7])


def _rope_tables(seq_len):
    t = np.arange(seq_len)
    axis_dim = HEAD_DIM // 2
    inv_freq = ROPE_THETA ** (-np.arange(0, axis_dim, 2, dtype=np.float64) / axis_dim)
    ang = np.concatenate([(t // GRID_W)[:, None] * inv_freq, (t % GRID_W)[:, None] * inv_freq], axis=-1)
    cos = np.repeat(np.cos(ang), 2, axis=-1)
    sin = np.repeat(np.sin(ang), 2, axis=-1) * np.tile([-1.0, 1.0], HEAD_DIM // 2)
    reps = LANES // HEAD_DIM
    return (jnp.asarray(np.tile(cos, (1, reps)), F32), jnp.asarray(np.tile(sin, (1, reps)), F32))


def _selectors():
    col = np.arange(SSM_TAPS)
    row = np.arange(LANES)
    e16 = (col[None, :] // SSM_GROUP == row[:, None])
    r2 = np.arange(SSM_ROW)
    x16 = (r2[:, None] // SSM_GROUP == np.arange(SSM_CHUNK)[None, :])
    as_bf16 = lambda a: jnp.asarray(a.astype(np.float32), BF16)
    return as_bf16(e16), as_bf16(x16)


def _s5_param_layout(a_re_f, a_im_f, ls_f, a_re_b, a_im_b, ls_b, b_re, b_im, c_re, c_im, d):
    g, ns = SSM_GROUPS, SSM_STATE
    zeros = jnp.zeros((g, ns), F32)
    per = lambda a_re, a_im, ls: jnp.stack([a_re, a_im, jnp.broadcast_to(ls[:, None], (g, ns))] + [zeros] * 5,
                                           axis=1)
    pr2 = jnp.concatenate([per(a_re_f, a_im_f, ls_f), per(a_re_b, a_im_b, ls_b)], axis=2)
    pc = jnp.transpose(pr2, (0, 2, 1))
    tile_b = lambda b: jnp.tile(jnp.concatenate([b, b], axis=1), (1, 1, LANES // SSM_GROUP))
    ccat = jnp.concatenate([c_re, c_im, c_re, c_im], axis=2)
    cr4 = jnp.concatenate([c_re] * 4, axis=2)
    ci4 = jnp.concatenate([c_im] * 4, axis=2)
    dtile = jnp.tile(d.reshape(g, 1, SSM_GROUP), (1, 1, SSM_LAGS))
    return pc, pr2, tile_b(b_re), tile_b(b_im), ccat, cr4, ci4, dtile


def kernel(x_prompt, x_sample, ab_w_in, ab_q_norm, ab_k_norm, ssm_a_re_f, ssm_a_im_f, ssm_log_step_f,
           ssm_a_re_b, ssm_a_im_b, ssm_log_step_b, ssm_b_re, ssm_b_im, ssm_c_re, ssm_c_im, ssm_d, ssm_w_glu,
           ssm_b_glu, ab_w_out, c_w_in, c_ln_g, c_ln_b, c_w_s, c_b_s, c_w_out, ln_mix_g, ln_mix_b, ff_w1, ff_w2,
           ln_ff_g, ln_ff_b):
    xs = [x_prompt, x_sample]
    row2 = lambda v: v.reshape(1, -1)

    slot_cols = (_HEAD_OF_SLOT[:, None] * HEAD_DIM + np.arange(HEAD_DIM)[None, :]).reshape(-1)
    w_in = ab_w_in[0]
    w_in = jnp.concatenate([w_in[:, slot_cols], w_in[:, ATTN_WIDTH:]], axis=1).astype(BF16)
    scale = HEAD_DIM ** -0.5 * math.log2(math.e)
    gain = jnp.concatenate([jnp.tile(ab_q_norm[0] * scale, N_HEADS), jnp.tile(ab_k_norm[0], N_KV_HEADS)])
    score_bound = (1.05 * HEAD_DIM * scale * jnp.max(jnp.abs(ab_q_norm[0])) * jnp.max(jnp.abs(ab_k_norm[0])))
    score_bound = score_bound.reshape(1, 1).astype(F32)
    hid = np.arange(LANES) // HEAD_DIM
    hmean = jnp.asarray((hid[:, None] == hid[None, :]).astype(np.float32) / HEAD_DIM, BF16)
    w_out = ab_w_out[0]
    w_out_a = w_out[:ATTN_WIDTH][slot_cols].astype(BF16)
    w_out_s = w_out[ATTN_WIDTH:].astype(BF16)
    w_glu = ssm_w_glu[0].astype(BF16)

    attn, us, pairs = [], [], []
    cos_t, sin_t = _rope_tables(max(x.shape[1] for x in xs))
    for x in xs:
        b, seq_len, _ = x.shape
        q, k, v_t, u = _inproj(x.reshape(b * seq_len, D_MODEL), w_in, row2(gain), cos_t, sin_t, hmean, seq_len)
        attn.append(_attention(score_bound, q.reshape(b, seq_len, ATTN_WIDTH), k.reshape(b, seq_len, KV_WIDTH),
                               v_t).reshape(b * seq_len, ATTN_WIDTH))
        us.append(u)
        pairs.append(seq_len // (2 * SSM_CHUNK))
    ys = _s5(us[0], us[1], *_s5_param_layout(
        ssm_a_re_f[0], ssm_a_im_f[0], ssm_log_step_f[0], ssm_a_re_b[0], ssm_a_im_b[0], ssm_log_step_b[0],
        ssm_b_re[0], ssm_b_im[0], ssm_c_re[0], ssm_c_im[0], ssm_d[0]), *_selectors(), tuple(pairs))

    ff_w1b = ff_w1.astype(BF16)
    ff_w2b = ff_w2.astype(BF16)
    ln4 = [jnp.stack([ln_mix_g[i], ln_mix_b[i], ln_ff_g[i], ln_ff_b[i]]) for i in range(DEPTH)]
    cln2 = jnp.stack([c_ln_g[0], c_ln_b[0]])
    bs_full = jnp.broadcast_to(c_b_s[0][:, :, None], (SGU_HEADS, SGU_CHUNK, LANES))
    c_w_in_b, c_w_s_b, c_w_out_b = c_w_in[0].astype(BF16), c_w_s[0].astype(BF16), c_w_out[0].astype(BF16)
    outs = []
    for x, a, y in zip(xs, attn, ys):
        b, seq_len, _ = x.shape
        n = b * seq_len
        h = _layer0_tail(x.reshape(n, D_MODEL), a, y, w_glu, row2(ssm_b_glu[0]), w_out_a, w_out_s,
                         ln4[0], ff_w1b, ff_w2b)
        h = _layer1(h, c_w_in_b, cln2, c_w_s_b, bs_full, c_w_out_b, ln4[1], ff_w1b, ff_w2b)
        outs.append(h.reshape(b, seq_len, D_MODEL))
    return tuple(outs)
```

```python
import functools
import math

import numpy as np
import jax
import jax.numpy as jnp
from jax import lax
from jax.experimental import pallas as pl
from jax.experimental.pallas import tpu as pltpu

F32 = jnp.float32
BF16 = jnp.bfloat16

D_MODEL = 1024
GRID_W = 64
N_HEADS = 8
N_KV_HEADS = 2
HEAD_DIM = 64
ATTN_WIDTH = N_HEADS * HEAD_DIM
KV_WIDTH = N_KV_HEADS * HEAD_DIM
QK_WIDTH = ATTN_WIDTH + KV_WIDTH
ROPE_THETA = 10000.0
SSM_WIDTH = D_MODEL - ATTN_WIDTH
SSM_GROUP = 16
SSM_GROUPS = SSM_WIDTH // SSM_GROUP
SSM_STATE = 64
SGU_HEADS = 8
SGU_CHUNK = 128
D_FF = 4 * D_MODEL
DEPTH = 2
AB_IN = ATTN_WIDTH + 2 * KV_WIDTH + SSM_WIDTH
ALPHA = (2 * DEPTH) ** 0.25
EPS = 1e-6

LANES = 128
F32_SUBLANES = 8
VMEM_LIMIT_BYTES = 56 * 1024 * 1024

SSM_CHUNK = 64
SSM_ROW = SSM_CHUNK * SSM_GROUP
SSM_LAGS = 2 * SSM_CHUNK
SSM_TAPS = SSM_LAGS * SSM_GROUP
SSM_SDIM = 4 * SSM_STATE

LAYER0_TAIL_TILE = 512
LAYER1_TILE = 512


def _cparams(sem):
    return pltpu.CompilerParams(dimension_semantics=sem, vmem_limit_bytes=VMEM_LIMIT_BYTES)


def _const_spec(shape):
    nd = len(shape)
    return pl.BlockSpec(shape, lambda *_: (0,) * nd, pipeline_mode=pl.Buffered(1))


def _layer_norm(r, g, b):
    mu = jnp.mean(r, axis=-1, keepdims=True)
    d = r - mu
    var = jnp.mean(d * d, axis=-1, keepdims=True)
    return d * lax.rsqrt(var + EPS) * g + b


def _split_bf16(x):
    hi = x.astype(BF16)
    lo = (x - hi.astype(F32)).astype(BF16)
    return hi, lo


def _select_cols(x, e):
    hi, lo = _split_bf16(x)
    y = jnp.dot(jnp.concatenate([hi, lo], axis=0), e, preferred_element_type=F32)
    return y[:x.shape[0]] + y[x.shape[0]:]


def _select_rows(e, x):
    hi, lo = _split_bf16(x)
    y = jnp.dot(e, jnp.concatenate([hi, lo], axis=1), preferred_element_type=F32)
    return y[:, :x.shape[1]] + y[:, x.shape[1]:]


def _inproj_kernel(x_ref, w_ref, gain_ref, cos_ref, sin_ref, hmean_ref, q_ref, k_ref, v_ref, u_ref):
    subs = [slice(r0, r0 + INPROJ_SUB) for r0 in range(0, x_ref.shape[0], INPROJ_SUB)]
    hs = [jnp.dot(x_ref[sl, :].astype(BF16), w_ref[...], preferred_element_type=F32) for sl in subs]
    mss = []
    for h in hs:
        sq = (h[:, :QK_WIDTH] * h[:, :QK_WIDTH]).astype(BF16)
        mss.append(jnp.concatenate(
            [jnp.dot(sq[:, j * LANES:(j + 1) * LANES], hmean_ref[...], preferred_element_type=F32)
             for j in range(QK_WIDTH // LANES)], axis=1))
    qkns = [h[:, :QK_WIDTH] * lax.rsqrt(ms + EPS) * gain_ref[...] for h, ms in zip(hs, mss)]
    even = (lax.broadcasted_iota(jnp.int32, (INPROJ_SUB, LANES), 1) % 2) == 0
    for sl, qkn in zip(subs, qkns):
        c = cos_ref[sl, :]
        s = sin_ref[sl, :]
        for j in range(QK_WIDTH // LANES):
            xs = qkn[:, j * LANES:(j + 1) * LANES]
            nxt = pltpu.roll(xs, LANES - 1, 1)
            prv = pltpu.roll(xs, 1, 1)
            roped = (xs * c + jnp.where(even, nxt, prv) * s).astype(BF16)
            if j < ATTN_WIDTH // LANES:
                q_ref[sl, j * LANES:(j + 1) * LANES] = roped
            else:
                k_ref[sl, :] = roped
    for sl, h in zip(subs, hs):
        v_ref[:, sl] = h[:, QK_WIDTH:QK_WIDTH + KV_WIDTH].T.astype(BF16)
        for r in range(INPROJ_SUB // LANES):
            blk = h[r * LANES:(r + 1) * LANES, QK_WIDTH + KV_WIDTH:]
            u_ref[:, sl.start // LANES + r, :] = blk.T


INPROJ_TILE = 1024
INPROJ_SUB = 256


def _inproj(x2, w_in, gain, cos_t, sin_t, hmean, seq_len):
    n = x2.shape[0]
    tm = INPROJ_TILE
    nper = seq_len // tm
    return pl.pallas_call(
        _inproj_kernel,
        grid=(n // tm,),
        in_specs=[
            pl.BlockSpec((tm, D_MODEL), lambda i: (i, 0)),
            _const_spec((D_MODEL, AB_IN)),
            _const_spec((1, QK_WIDTH)),
            pl.BlockSpec((tm, LANES), lambda i: (i % nper, 0)),
            pl.BlockSpec((tm, LANES), lambda i: (i % nper, 0)),
            _const_spec((LANES, LANES)),
        ],
        out_specs=[
            pl.BlockSpec((tm, ATTN_WIDTH), lambda i: (i, 0)),
            pl.BlockSpec((tm, KV_WIDTH), lambda i: (i, 0)),
            pl.BlockSpec((KV_WIDTH, tm), lambda i: (0, i)),
            pl.BlockSpec((SSM_WIDTH, tm // LANES, LANES), lambda i: (0, i, 0)),
        ],
        out_shape=[
            jax.ShapeDtypeStruct((n, ATTN_WIDTH), BF16),
            jax.ShapeDtypeStruct((n, KV_WIDTH), BF16),
            jax.ShapeDtypeStruct((KV_WIDTH, n), BF16),
            jax.ShapeDtypeStruct((SSM_WIDTH, n // LANES, LANES), F32),
        ],
        compiler_params=_cparams(("parallel",)),
        name="inproj",
    )(x2, w_in, gain, cos_t, sin_t, hmean)


ATT_UNIT_Q = 64


ATT_SHIFT_LIMIT = 60.0


def _attn_kernel(bound_ref, q_ref, k_ref, vt_ref, o_ref,
                 s0_ref, s1_ref, p0_ref, p1_ref, m0_ref, m1_ref, l0_ref, l1_ref):
    seq_len = k_ref.shape[1]
    n_units = seq_len // ATT_UNIT_Q
    rep = N_HEADS // N_KV_HEADS
    lane = lax.broadcasted_iota(jnp.int32, (ATT_UNIT_Q, LANES), 1)
    lo_half = lane < HEAD_DIM
    zero = jnp.zeros((ATT_UNIT_Q, LANES), BF16)
    bound = bound_ref[0, 0]

    def rows_of(t):
        return pl.ds(pl.multiple_of(t * ATT_UNIT_Q, ATT_UNIT_Q), ATT_UNIT_Q)

    def raw_scores(t):
        q = q_ref[0, rows_of(t), :]
        slots = [q[:, m * LANES:(m + 1) * LANES] for m in range(rep)]
        qs = jnp.concatenate([jnp.where(lo_half, x, zero) for x in slots]
                             + [jnp.where(lo_half, zero, x) for x in slots], axis=0)
        return lax.dot_general(k_ref[0], qs, (((1,), (1,)), ((), ())), preferred_element_type=F32)

    def scores(t, s_ref, m_ref):
        s = raw_scores(t)
        s_ref[...] = s
        m_ref[...] = jnp.max(s, axis=0, keepdims=True)

    def probs(s_ref, m_ref, p_ref, l_ref):
        p = jnp.exp2(s_ref[...] - m_ref[...])
        l_ref[...] = jnp.sum(p, axis=0, keepdims=True)
        p_ref[...] = p.astype(BF16)

    def bounded_probs(t, p_ref, l_ref):
        p = jnp.exp2(raw_scores(t) - bound)
        l_ref[...] = jnp.sum(p, axis=0, keepdims=True)
        p_ref[...] = p.astype(BF16)

    def output(t, p_ref, l_ref):
        o_t = jnp.dot(vt_ref[...], p_ref[...], preferred_element_type=F32) / l_ref[...]
        o = o_t.T
        for m in range(rep):
            g0 = o[m * ATT_UNIT_Q:(m + 1) * ATT_UNIT_Q]
            g1 = o[(rep + m) * ATT_UNIT_Q:(rep + m + 1) * ATT_UNIT_Q]
            o_ref[0, rows_of(t), m * LANES:(m + 1) * LANES] = jnp.where(lo_half, g0, g1).astype(BF16)

    @pl.when(bound <= ATT_SHIFT_LIMIT)
    def _bounded_shift():
        bounded_probs(0, p0_ref, l0_ref)

        def body(i, carry):
            t = 2 * i + 1
            bounded_probs(t, p1_ref, l1_ref)
            output(t - 1, p0_ref, l0_ref)
            bounded_probs(t + 1, p0_ref, l0_ref)
            output(t, p1_ref, l1_ref)
            return carry

        lax.fori_loop(0, (n_units - 2) // 2, body, 0)
        bounded_probs(n_units - 1, p1_ref, l1_ref)
        output(n_units - 2, p0_ref, l0_ref)
        output(n_units - 1, p1_ref, l1_ref)

    @pl.when(bound > ATT_SHIFT_LIMIT)
    def _row_max_shift():
        scores(0, s0_ref, m0_ref)
        scores(1, s1_ref, m1_ref)
        probs(s0_ref, m0_ref, p0_ref, l0_ref)

        def body(i, carry):
            t = 2 * i + 2
            scores(t, s0_ref, m0_ref)
            probs(s1_ref, m1_ref, p1_ref, l1_ref)
            output(t - 2, p0_ref, l0_ref)
            scores(t + 1, s1_ref, m1_ref)
            probs(s0_ref, m0_ref, p0_ref, l0_ref)
            output(t - 1, p1_ref, l1_ref)
            return carry

        lax.fori_loop(0, (n_units - 2) // 2, body, 0)
        probs(s1_ref, m1_ref, p1_ref, l1_ref)
        output(n_units - 2, p0_ref, l0_ref)
        output(n_units - 1, p1_ref, l1_ref)


def _attention(bound, q, k, v_t):
    b, seq_len, _ = q.shape
    cols = ATT_UNIT_Q * N_HEADS
    seq = lambda w: pl.BlockSpec((1, seq_len, w), lambda i: (i, 0, 0))
    return pl.pallas_call(
        _attn_kernel,
        grid=(b,),
        in_specs=[pl.BlockSpec(memory_space=pltpu.SMEM), seq(ATTN_WIDTH), seq(KV_WIDTH),
                  pl.BlockSpec((KV_WIDTH, seq_len), lambda i: (0, i))],
        out_specs=seq(ATTN_WIDTH),
        out_shape=jax.ShapeDtypeStruct((b, seq_len, ATTN_WIDTH), BF16),
        scratch_shapes=[pltpu.VMEM((seq_len, cols), F32), pltpu.VMEM((seq_len, cols), F32),
                        pltpu.VMEM((seq_len, cols), BF16), pltpu.VMEM((seq_len, cols), BF16),
                        pltpu.VMEM((1, cols), F32), pltpu.VMEM((1, cols), F32),
                        pltpu.VMEM((1, cols), F32), pltpu.VMEM((1, cols), F32)],
        compiler_params=_cparams(("parallel",)),
        name="attention",
    )(bound, q, k, v_t)


SSM_PITCH = 132


def _s5_group(xp_ref, xs_ref, pc_ref, pr2_ref, bcr_ref, bci_ref, ccat_ref, cr4_ref, ci4_ref, dt_ref,
              e16_ref, x16_ref, yp_ref, ys_ref, s_ref, zt_ref, m_ref, yt_ref, pairs):
    ns = SSM_STATE
    pc = pc_ref[0]
    are_c, aim_c, dt_c = pc[:, 0:1], pc[:, 1:2], jnp.exp(pc[:, 2:3])
    row = lax.broadcasted_iota(jnp.int32, (2 * ns, LANES), 0)
    lan = lax.broadcasted_iota(jnp.int32, (2 * ns, LANES), 1)
    b_rows = row >= ns
    n_g = jnp.where(b_rows, lan - (SSM_CHUNK - 1), (SSM_CHUNK - 1) - lan)
    mask = n_g >= 0
    nf = jnp.where(mask, n_g, 0).astype(F32)
    rho = jnp.exp(are_c * dt_c * nf)
    ang = aim_c * dt_c * nf
    cs = jnp.where(mask, rho * jnp.cos(ang), 0.0)
    sn = jnp.where(mask, rho * jnp.sin(ang), 0.0)
    b_col = b_rows[:, 0:1]
    a1r = jnp.where(b_col, cs[:, SSM_CHUNK:SSM_CHUNK + 1], cs[:, SSM_CHUNK - 2:SSM_CHUNK - 1]) - 1.0
    a1i = jnp.where(b_col, sn[:, SSM_CHUNK:SSM_CHUNK + 1], sn[:, SSM_CHUNK - 2:SSM_CHUNK - 1])
    den = are_c * are_c + aim_c * aim_c
    cr = (a1r * are_c + a1i * aim_c) / den
    ci = (a1i * are_c - a1r * aim_c) / den
    yield

    def tile_lanes(x):
        return jnp.concatenate([x] * (SSM_TAPS // LANES), axis=1)

    bre, bim = bcr_ref[0], bci_ref[0]
    bbr = tile_lanes(cr * bre - ci * bim)
    bbi = tile_lanes(cr * bim + ci * bre)
    base = jnp.concatenate([cs[:ns], sn[:ns], cs[ns:], sn[ns:]], axis=0)
    ex = _select_cols(base, e16_ref[...])
    ecf, esf, ecb, esb = ex[:ns], ex[ns:2 * ns], ex[2 * ns:3 * ns], ex[3 * ns:]
    z_fre = ecf * bbr[:ns] - esf * bbi[:ns]
    z_fim = esf * bbr[:ns] + ecf * bbi[:ns]
    z_bre = ecb * bbr[ns:] - esb * bbi[ns:]
    z_bim = esb * bbr[ns:] + ecb * bbi[ns:]
    yield

    g_tab = jnp.concatenate([z_fre, -z_fim, z_bre, -z_bim], axis=0)
    ch, cl = _split_bf16(ccat_ref[0])
    gh, gl = _split_bf16(g_tab)
    w = (jnp.dot(ch, gh, preferred_element_type=F32) + jnp.dot(ch, gl, preferred_element_type=F32)
         + jnp.dot(cl, gh, preferred_element_type=F32))
    wrow = lax.broadcasted_iota(jnp.int32, (SSM_GROUP, SSM_TAPS), 0)
    wcol = lax.broadcasted_iota(jnp.int32, (SSM_GROUP, SSM_TAPS), 1)
    on_diag = (wcol // SSM_GROUP == SSM_CHUNK - 1) & (wcol % SSM_GROUP == wrow)
    w = w + jnp.where(on_diag, dt_ref[0], 0.0)
    yield
    per_vreg = LANES // SSM_GROUP
    rolled = [w] + [pltpu.roll(w, SSM_TAPS - SSM_GROUP * kk, 1) for kk in range(1, per_vreg)]
    for j in range(SSM_CHUNK):
        r = SSM_CHUNK - 1 - j
        blk = rolled[r % per_vreg][:, (r // per_vreg) * LANES:(r // per_vreg) * LANES + SSM_ROW]
        m_ref[j * SSM_GROUP:(j + 1) * SSM_GROUP, :] = blk.astype(BF16)

    p2 = jnp.concatenate([z_fre[:, :SSM_ROW], z_fim[:, :SSM_ROW], z_bre[:, SSM_ROW:], z_bim[:, SSM_ROW:]],
                         axis=0).astype(BF16)
    yield

    pr2 = pr2_ref[0]
    j_r = lax.broadcasted_iota(jnp.int32, (SSM_CHUNK, LANES), 0)
    l_r = lax.broadcasted_iota(jnp.int32, (SSM_CHUNK, LANES), 1)
    lo_r = l_r < ns
    n_q = jnp.where(lo_r, j_r + 1, (SSM_CHUNK - 1) - j_r).astype(F32)
    dt2 = jnp.exp(pr2[2:3, :])
    rho = jnp.exp(pr2[0:1, :] * dt2 * n_q)
    ang = pr2[1:2, :] * dt2 * n_q
    c2 = rho * jnp.cos(ang)
    s2 = rho * jnp.sin(ang)
    base_q = jnp.concatenate([jnp.where(lo_r, c2, -pltpu.roll(s2, ns, 1)),
                              jnp.where(lo_r, pltpu.roll(c2, ns, 1), -s2)], axis=1)
    a1x = _select_rows(x16_ref[...], base_q)
    sign = jnp.where(lax.broadcasted_iota(jnp.int32, (SSM_ROW, LANES), 1) < ns, 1.0, -1.0)
    a2x = jnp.concatenate([pltpu.roll(a1x[:, :LANES], ns, 1) * sign,
                           pltpu.roll(a1x[:, LANES:], ns, 1) * sign], axis=1)
    q2 = (a1x * jnp.concatenate([cr4_ref[0]] * SSM_CHUNK, axis=0)
          + a2x * jnp.concatenate([ci4_ref[0]] * SSM_CHUNK, axis=0)).astype(BF16)

    n_s = (SSM_CHUNK * jnp.left_shift(1, jnp.minimum(lan, 8))).astype(F32)
    rho = jnp.exp(are_c * dt_c * n_s)
    ang = aim_c * dt_c * n_s
    sc_r = rho * jnp.cos(ang)
    sc_i = rho * jnp.sin(ang)
    yield

    for seg, x_ref in enumerate((xp_ref, xs_ref)):
        for hi in range(SSM_GROUP):
            s_ref[seg, hi * SSM_PITCH:hi * SSM_PITCH + LANES, :] = x_ref[hi].T
    yield
    for par in range(2):
        for i in range(SSM_CHUNK):
            for seg in range(2):
                rows16 = s_ref[seg, pl.ds(par * SSM_CHUNK + i, SSM_GROUP, stride=SSM_PITCH), :]
                blk = 2 * par + seg
                zt_ref[i * SSM_GROUP:(i + 1) * SSM_GROUP, blk * LANES:(blk + 1) * LANES] = rows16.astype(BF16)
    yield

    zt = zt_ref[...]
    yt = jnp.dot(m_ref[...], zt, preferred_element_type=F32)
    st = jnp.dot(p2, zt, preferred_element_type=F32)
    yield

    def cmul(kk, lo, v):
        ar, ai = sc_r[lo:lo + ns, kk:kk + 1], sc_i[lo:lo + ns, kk:kk + 1]
        return jnp.concatenate([ar * v[:ns] - ai * v[ns:], ar * v[ns:] + ai * v[:ns]], axis=0)

    carried = [None] * 4
    for seg, npairs in enumerate(pairs):
        c2i = lax.broadcasted_iota(jnp.int32, (2 * ns, LANES), 1) % npairs
        right = lambda v, d: jnp.where(c2i >= d, pltpu.roll(v, d, 1), 0.0)
        left = lambda v, d: jnp.where(c2i < npairs - d, pltpu.roll(v, LANES - d, 1), 0.0)
        ev, od = st[:, seg * LANES:(seg + 1) * LANES], st[:, (2 + seg) * LANES:(3 + seg) * LANES]
        ef, of, eb, ob = ev[:2 * ns], od[:2 * ns], ev[2 * ns:], od[2 * ns:]
        ef, of = ef + cmul(0, 0, right(of, 1)), of + cmul(0, 0, ef)
        eb, ob = eb + cmul(0, ns, ob), ob + cmul(0, ns, left(eb, 1))
        for kk in range(1, int(math.log2(npairs)) + 1):
            d = 1 << (kk - 1)
            ef, of = ef + cmul(kk, 0, right(ef, d)), of + cmul(kk, 0, right(of, d))
            eb, ob = eb + cmul(kk, ns, left(eb, d)), ob + cmul(kk, ns, left(ob, d))
        carried[seg] = jnp.concatenate([right(of, 1), ob], axis=0)
        carried[2 + seg] = jnp.concatenate([ef, left(eb, 1)], axis=0)
    xin = jnp.concatenate(carried, axis=1).astype(BF16)
    yield
    yt = yt + jnp.dot(q2, xin, preferred_element_type=F32)

    for blk in range(4):
        yt_ref[blk] = yt[:, blk * LANES:(blk + 1) * LANES]
    for seg, y_ref in enumerate((yp_ref, ys_ref)):
        for ho in range(SSM_GROUP):
            ev = yt_ref[seg, pl.ds(ho, SSM_CHUNK, stride=SSM_GROUP), :]
            od = yt_ref[2 + seg, pl.ds(ho, SSM_CHUNK, stride=SSM_GROUP), :]
            y_ref[ho] = jnp.concatenate([ev, od], axis=0).T


S5_GROUPS_PER_STEP = 2


def _s5_kernel(xp_ref, xs_ref, pc_ref, pr2_ref, bcr_ref, bci_ref, ccat_ref, cr4_ref, ci4_ref, dt_ref,
               e16_ref, x16_ref, yp_ref, ys_ref, s_ref, zt_ref, m_ref, yt_ref, *, pairs):
    one = lambda ref, g: ref.at[pl.ds(g, 1)]
    chans = lambda ref, g: ref.at[pl.ds(g * SSM_GROUP, SSM_GROUP)]
    streams = [
        _s5_group(chans(xp_ref, g), chans(xs_ref, g), one(pc_ref, g), one(pr2_ref, g), one(bcr_ref, g),
                  one(bci_ref, g), one(ccat_ref, g), one(cr4_ref, g), one(ci4_ref, g), one(dt_ref, g),
                  e16_ref, x16_ref, chans(yp_ref, g), chans(ys_ref, g),
                  s_ref.at[g], zt_ref.at[g], m_ref.at[g], yt_ref.at[g], pairs)
        for g in range(S5_GROUPS_PER_STEP)]
    while streams:
        streams = [s for s in streams if next(s, StopIteration) is not StopIteration]


def _s5(u_p, u_s, pc, pr2, bcr, bci, ccat, cr4, ci4, dtile, e16, x16, pairs):
    g = SSM_GROUPS // S5_GROUPS_PER_STEP
    gs = S5_GROUPS_PER_STEP
    rows = u_p.shape[1]
    assert u_s.shape[1] == rows == LANES
    grp = lambda shape: pl.BlockSpec((gs,) + shape, lambda i: (i, 0, 0))
    chan = pl.BlockSpec((gs * SSM_GROUP, rows, LANES), lambda i: (i, 0, 0))
    return pl.pallas_call(
        functools.partial(_s5_kernel, pairs=pairs),
        grid=(g,),
        in_specs=[chan, chan, grp((2 * SSM_STATE, 8)), grp((8, LANES)),
                  grp((2 * SSM_STATE, LANES)), grp((2 * SSM_STATE, LANES)),
                  grp((SSM_GROUP, SSM_SDIM)), grp((SSM_GROUP, SSM_SDIM)), grp((SSM_GROUP, SSM_SDIM)),
                  grp((1, SSM_TAPS)), _const_spec((LANES, SSM_TAPS)), _const_spec((SSM_ROW, SSM_CHUNK))],
        out_specs=[chan, chan],
        out_shape=[jax.ShapeDtypeStruct(u_p.shape, F32), jax.ShapeDtypeStruct(u_s.shape, F32)],
        scratch_shapes=[pltpu.VMEM((gs, 2, SSM_GROUP * SSM_PITCH, LANES), F32),
                        pltpu.VMEM((gs, SSM_ROW, 4 * LANES), BF16),
                        pltpu.VMEM((gs, SSM_ROW, SSM_ROW), BF16),
                        pltpu.VMEM((gs, 4, SSM_ROW, LANES), F32)],
        compiler_params=_cparams(("parallel",)),
        name="s5",
    )(u_p, u_s, pc, pr2, bcr, bci, ccat, cr4, ci4, dtile, e16, x16)


FF_CHUNK = 1024
FF_ROWS = 256
MIXER_ROWS = 128


def _ffn_ln(x, w1_ref, w2_ref, g, b):
    xs = [x[r0:r0 + FF_ROWS] for r0 in range(0, x.shape[0], FF_ROWS)]
    xbs = [v.astype(BF16) for v in xs]
    accs = [ALPHA * v for v in xs]
    for c in range(D_FF // FF_CHUNK):
        for i in range(len(xs)):
            h = jnp.dot(xbs[i], w1_ref[0, :, c * FF_CHUNK:(c + 1) * FF_CHUNK], preferred_element_type=F32)
            h = jnp.maximum(h, 0.0)
            accs[i] = accs[i] + jnp.dot((h * h).astype(BF16), w2_ref[0, c * FF_CHUNK:(c + 1) * FF_CHUNK, :],
                                        preferred_element_type=F32)
    return jnp.concatenate([_layer_norm(a, g, b) for a in accs], axis=0)


def _layer0_tail_kernel(xp_ref, xs_ref, ap_ref, as_ref, yp_ref, ys_ref, wg_ref, bg_ref, woa_ref, wos_ref, ln_ref,
                        w1_ref, w2_ref, o_ref, *, steps_p):
    first = pl.program_id(0) < steps_p
    pick = lambda p, s: jnp.where(first, p, s)
    rows_per_tile = xp_ref.shape[0] // LANES
    local_step = jnp.where(first, pl.program_id(0), pl.program_id(0) - steps_p)
    row0 = (local_step % (yp_ref.shape[1] // rows_per_tile)) * rows_per_tile
    ln = ln_ref[...]
    subs = [slice(r * LANES, (r + 1) * LANES) for r in range(rows_per_tile)]
    ys = [jax.nn.gelu(pick(yp_ref[:, row0 + r, :], ys_ref[:, row0 + r, :]).T) for r in range(rows_per_tile)]
    gates = [jnp.dot(y.astype(BF16), wg_ref[...], preferred_element_type=F32) + bg_ref[...] for y in ys]
    zs = [(y * jax.nn.sigmoid(g)).astype(BF16) for y, g in zip(ys, gates)]
    mixes = [jnp.dot(pick(ap_ref[sl, :], as_ref[sl, :]), woa_ref[...], preferred_element_type=F32)
             + jnp.dot(z, wos_ref[...], preferred_element_type=F32) for sl, z in zip(subs, zs)]
    x1 = jnp.concatenate([_layer_norm(ALPHA * pick(xp_ref[sl, :], xs_ref[sl, :]) + m, ln[0:1], ln[1:2])
                          for sl, m in zip(subs, mixes)], axis=0)
    o_ref[...] = _ffn_ln(x1, w1_ref, w2_ref, ln[2:3], ln[3:4])


def _layer_weights_spec(shape, layer):
    return pl.BlockSpec((1,) + shape, lambda i: (layer, 0, 0), pipeline_mode=pl.Buffered(1))


def _layer0_tail(xs2, attns, ys, w_glu, b_glu, w_out_a, w_out_s, ln4, w1, w2):
    tm = LAYER0_TAIL_TILE
    y_rows = max(F32_SUBLANES, tm // LANES)
    tiles_per_y = y_rows * LANES // tm
    steps_p, steps_s = xs2[0].shape[0] // tm, xs2[1].shape[0] // tm
    at_p = lambda i: jnp.minimum(i, steps_p - 1)
    at_s = lambda i: jnp.maximum(i - steps_p, 0)
    tok = lambda w, at: pl.BlockSpec((tm, w), lambda i: (at(i), 0))
    chan = lambda at: pl.BlockSpec((SSM_WIDTH, y_rows, LANES), lambda i: (0, at(i) // tiles_per_y, 0))
    n = xs2[0].shape[0] + xs2[1].shape[0]
    return pl.pallas_call(
        functools.partial(_layer0_tail_kernel, steps_p=steps_p),
        grid=(steps_p + steps_s,),
        in_specs=[tok(D_MODEL, at_p), tok(D_MODEL, at_s), tok(ATTN_WIDTH, at_p), tok(ATTN_WIDTH, at_s),
                  chan(at_p), chan(at_s),
                  _const_spec((SSM_WIDTH, SSM_WIDTH)), _const_spec((1, SSM_WIDTH)),
                  _const_spec((ATTN_WIDTH, D_MODEL)), _const_spec((SSM_WIDTH, D_MODEL)),
                  _const_spec((4, D_MODEL)),
                  _layer_weights_spec((D_MODEL, D_FF), 0), _layer_weights_spec((D_FF, D_MODEL), 0)],
        out_specs=pl.BlockSpec((tm, D_MODEL), lambda i: (i, 0)),
        out_shape=jax.ShapeDtypeStruct((n, D_MODEL), F32),
        compiler_params=_cparams(("parallel",)),
        name="layer0_tail",
    )(xs2[0], xs2[1], attns[0], attns[1], ys[0], ys[1], w_glu, b_glu, w_out_a, w_out_s, ln4, w1, w2)


def _layer1_kernel(x_ref, win_ref, cln_ref, ws_ref, bs_ref, wout_ref, ln_ref, w1_ref, w2_ref, op_ref, os_ref, *,
                   steps_p):
    cln = cln_ref[...]
    ln = ln_ref[...]
    halves = [x_ref[r0:r0 + MIXER_ROWS, :] for r0 in range(0, x_ref.shape[0], MIXER_ROWS)]
    hs = [jax.nn.gelu(jnp.dot(x.astype(BF16), win_ref[...], preferred_element_type=F32)) for x in halves]
    vbs = [_layer_norm(h[:, D_MODEL:], cln[0:1], cln[1:2]).astype(BF16) for h in hs]
    gated = []
    for h, vb in zip(hs, vbs):
        rows = []
        for c in range(MIXER_ROWS // SGU_CHUNK):
            cols = []
            for hd in range(SGU_HEADS):
                blk = vb[c * SGU_CHUNK:(c + 1) * SGU_CHUNK, hd * LANES:(hd + 1) * LANES]
                cols.append(jnp.dot(ws_ref[hd], blk, preferred_element_type=F32) + bs_ref[hd])
            rows.append(jnp.concatenate(cols, axis=1))
        gated.append((h[:, :D_MODEL] * jnp.concatenate(rows, axis=0)).astype(BF16))
    outs = [jnp.dot(g, wout_ref[...], preferred_element_type=F32) for g in gated]
    x1 = jnp.concatenate([_layer_norm(ALPHA * x + o, ln[0:1], ln[1:2]) for x, o in zip(halves, outs)], axis=0)
    res = _ffn_ln(x1, w1_ref, w2_ref, ln[2:3], ln[3:4])

    @pl.when(pl.program_id(0) < steps_p)
    def _():
        op_ref[...] = res

    @pl.when(pl.program_id(0) >= steps_p)
    def _():
        os_ref[...] = res


def _layer1(x2, n_p, w_in, cln2, w_s, b_s_full, w_out, ln4, w1, w2):
    n = x2.shape[0]
    tm = LAYER1_TILE
    steps_p = n_p // tm
    return pl.pallas_call(
        functools.partial(_layer1_kernel, steps_p=steps_p),
        grid=(n // tm,),
        in_specs=[pl.BlockSpec((tm, D_MODEL), lambda i: (i, 0)),
                  _const_spec((D_MODEL, 2 * D_MODEL)), _const_spec((2, D_MODEL)),
                  _const_spec((SGU_HEADS, SGU_CHUNK, SGU_CHUNK)), _const_spec((SGU_HEADS, SGU_CHUNK, LANES)),
                  _const_spec((D_MODEL, D_MODEL)), _const_spec((4, D_MODEL)),
                  _layer_weights_spec((D_MODEL, D_FF), 1), _layer_weights_spec((D_FF, D_MODEL), 1)],
        out_specs=[pl.BlockSpec((tm, D_MODEL), lambda i: (jnp.minimum(i, steps_p - 1), 0)),
                   pl.BlockSpec((tm, D_MODEL), lambda i: (jnp.maximum(i - steps_p, 0), 0))],
        out_shape=[jax.ShapeDtypeStruct((n_p, D_MODEL), F32), jax.ShapeDtypeStruct((n - n_p, D_MODEL), F32)],
        compiler_params=_cparams(("arbitrary",)),
        name="layer1",
    )(x2, w_in, cln2, w_s, b_s_full, w_out, ln4, w1, w2)


_HEAD_OF_SLOT = np.array([0, 4, 1, 5, 2, 6, 3, 7])


def _rope_tables(seq_len):
    t = np.arange(seq_len)
    axis_dim = HEAD_DIM // 2
    inv_freq = ROPE_THETA ** (-np.arange(0, axis_dim, 2, dtype=np.float64) / axis_dim)
    ang = np.concatenate([(t // GRID_W)[:, None] * inv_freq, (t % GRID_W)[:, None] * inv_freq], axis=-1)
    cos = np.repeat(np.cos(ang), 2, axis=-1)
    sin = np.repeat(np.sin(ang), 2, axis=-1) * np.tile([-1.0, 1.0], HEAD_DIM // 2)
    reps = LANES // HEAD_DIM
    return (jnp.asarray(np.tile(cos, (1, reps)), F32), jnp.asarray(np.tile(sin, (1, reps)), F32))


def _selectors():
    col = np.arange(SSM_TAPS)
    row = np.arange(LANES)
    e16 = (col[None, :] // SSM_GROUP == row[:, None])
    r2 = np.arange(SSM_ROW)
    x16 = (r2[:, None] // SSM_GROUP == np.arange(SSM_CHUNK)[None, :])
    as_bf16 = lambda a: jnp.asarray(a.astype(np.float32), BF16)
    return as_bf16(e16), as_bf16(x16)


def _s5_param_layout(a_re_f, a_im_f, ls_f, a_re_b, a_im_b, ls_b, b_re, b_im, c_re, c_im, d):
    g, ns = SSM_GROUPS, SSM_STATE
    zeros = jnp.zeros((g, ns), F32)
    per = lambda a_re, a_im, ls: jnp.stack([a_re, a_im, jnp.broadcast_to(ls[:, None], (g, ns))] + [zeros] * 5,
                                           axis=1)
    pr2 = jnp.concatenate([per(a_re_f, a_im_f, ls_f), per(a_re_b, a_im_b, ls_b)], axis=2)
    pc = jnp.transpose(pr2, (0, 2, 1))
    tile_b = lambda b: jnp.tile(jnp.concatenate([b, b], axis=1), (1, 1, LANES // SSM_GROUP))
    ccat = jnp.concatenate([c_re, c_im, c_re, c_im], axis=2)
    cr4 = jnp.concatenate([c_re] * 4, axis=2)
    ci4 = jnp.concatenate([c_im] * 4, axis=2)
    dtile = jnp.tile(d.reshape(g, 1, SSM_GROUP), (1, 1, SSM_LAGS))
    return pc, pr2, tile_b(b_re), tile_b(b_im), ccat, cr4, ci4, dtile


def kernel(x_prompt, x_sample, ab_w_in, ab_q_norm, ab_k_norm, ssm_a_re_f, ssm_a_im_f, ssm_log_step_f,
           ssm_a_re_b, ssm_a_im_b, ssm_log_step_b, ssm_b_re, ssm_b_im, ssm_c_re, ssm_c_im, ssm_d, ssm_w_glu,
           ssm_b_glu, ab_w_out, c_w_in, c_ln_g, c_ln_b, c_w_s, c_b_s, c_w_out, ln_mix_g, ln_mix_b, ff_w1, ff_w2,
           ln_ff_g, ln_ff_b):
    xs = [x_prompt, x_sample]
    row2 = lambda v: v.reshape(1, -1)

    slot_cols = (_HEAD_OF_SLOT[:, None] * HEAD_DIM + np.arange(HEAD_DIM)[None, :]).reshape(-1)
    w_in = ab_w_in[0]
    w_in = jnp.concatenate([w_in[:, slot_cols], w_in[:, ATTN_WIDTH:]], axis=1).astype(BF16)
    scale = HEAD_DIM ** -0.5 * math.log2(math.e)
    gain = jnp.concatenate([jnp.tile(ab_q_norm[0] * scale, N_HEADS), jnp.tile(ab_k_norm[0], N_KV_HEADS)])
    score_bound = (1.05 * HEAD_DIM * scale * jnp.max(jnp.abs(ab_q_norm[0])) * jnp.max(jnp.abs(ab_k_norm[0])))
    score_bound = score_bound.reshape(1, 1).astype(F32)
    hid = np.arange(LANES) // HEAD_DIM
    hmean = jnp.asarray((hid[:, None] == hid[None, :]).astype(np.float32) / HEAD_DIM, BF16)
    w_out = ab_w_out[0]
    w_out_a = w_out[:ATTN_WIDTH][slot_cols].astype(BF16)
    w_out_s = w_out[ATTN_WIDTH:].astype(BF16)
    w_glu = ssm_w_glu[0].astype(BF16)

    attn, us, pairs = [], [], []
    cos_t, sin_t = _rope_tables(max(x.shape[1] for x in xs))
    for x in xs:
        b, seq_len, _ = x.shape
        q, k, v_t, u = _inproj(x.reshape(b * seq_len, D_MODEL), w_in, row2(gain), cos_t, sin_t, hmean, seq_len)
        attn.append(_attention(score_bound, q.reshape(b, seq_len, ATTN_WIDTH), k.reshape(b, seq_len, KV_WIDTH),
                               v_t).reshape(b * seq_len, ATTN_WIDTH))
        us.append(u)
        pairs.append(seq_len // (2 * SSM_CHUNK))
    ys = _s5(us[0], us[1], *_s5_param_layout(
        ssm_a_re_f[0], ssm_a_im_f[0], ssm_log_step_f[0], ssm_a_re_b[0], ssm_a_im_b[0], ssm_log_step_b[0],
        ssm_b_re[0], ssm_b_im[0], ssm_c_re[0], ssm_c_im[0], ssm_d[0]), *_selectors(), tuple(pairs))

    ff_w1b = ff_w1.astype(BF16)
    ff_w2b = ff_w2.astype(BF16)
    ln4 = [jnp.stack([ln_mix_g[i], ln_mix_b[i], ln_ff_g[i], ln_ff_b[i]]) for i in range(DEPTH)]
    cln2 = jnp.stack([c_ln_g[0], c_ln_b[0]])
    bs_full = jnp.broadcast_to(c_b_s[0][:, :, None], (SGU_HEADS, SGU_CHUNK, LANES))
    c_w_in_b, c_w_s_b, c_w_out_b = c_w_in[0].astype(BF16), c_w_s[0].astype(BF16), c_w_out[0].astype(BF16)
    h = _layer0_tail([x.reshape(-1, D_MODEL) for x in xs], attn, ys, w_glu, row2(ssm_b_glu[0]), w_out_a, w_out_s,
                     ln4[0], ff_w1b, ff_w2b)
    outs = _layer1(h, xs[0].shape[0] * xs[0].shape[1], c_w_in_b, cln2, c_w_s_b, bs_full, c_w_out_b, ln4[1],
                   ff_w1b, ff_w2b)
    return tuple(o.reshape(x.shape) for o, x in zip(outs, xs))
```
